```python
import jax, jax.numpy as jnp
from jax import lax
import numpy as np

D_MODEL = 1024
BATCH = 16
SEQ = 256
DEPTH = 1
DEC_BATCH = 4
DEC_SEQ = 4096
PAST_LEN = 256

GRID_W = 64
D_MIX = D_MODEL
D_CONV = D_MIX // 2
D_MLSTM = D_MIX - D_CONV
N_HEADS = 4
HEAD_DIM = D_MLSTM // N_HEADS
N_GATES = 4 * N_HEADS
D_IN = 3 * D_CONV + 4 * D_MLSTM + N_GATES
D_FF = ((8 * D_MODEL // 3 + 127) // 128) * 128
CHUNK = 128
EPS = 1e-6

kernel_name = 'hymba_conv_mlstm_diffusion_step'


def rmsnorm(x, g):
    xf = x.astype(jnp.float32)
    y = xf * lax.rsqrt(jnp.mean(xf * xf, axis=-1, keepdims=True) + EPS)
    return (y * g.astype(jnp.float32)).astype(x.dtype)


def conv3(x4, w, b, axis):
    n = x4.shape[axis]
    pad = [(0, 0)] * 4
    pad[axis] = (1, 1)
    xp = jnp.pad(x4, pad)
    prev = lax.slice_in_dim(xp, 0, n, axis=axis)
    nxt = lax.slice_in_dim(xp, 2, n + 2, axis=axis)
    return w[0] * prev + w[1] * x4 + w[2] * nxt + b


def mlstm_scan(q, k, v, log_i, log_f, C0, n0, m0):
    B, T, H, Dh = q.shape
    nc = T // CHUNK
    f32 = jnp.float32

    def blocks(a):
        a = a.astype(f32).reshape((B, nc, CHUNK) + a.shape[2:])
        return jnp.moveaxis(a, (1, 3), (0, 2))

    qb = blocks(q) * (Dh ** -0.5)
    kb, vb, ib, fb = blocks(k), blocks(v), blocks(log_i), blocks(log_f)
    mask = jnp.tril(jnp.ones((CHUNK, CHUNK), dtype=bool))

    def step(carry, inp):
        C, n, m = carry
        qc, kc, vc, ic, fc = inp
        b = jnp.cumsum(fc, axis=-1)
        dlog = b[..., :, None] - b[..., None, :] + ic[..., None, :]
        dlog = jnp.where(mask, dlog, -jnp.inf)
        inter = b + m[..., None]
        m_t = jnp.maximum(inter, jnp.max(dlog, axis=-1))
        w_intra = jnp.exp(dlog - m_t[..., None])
        w_inter = jnp.exp(inter - m_t)
        s = jnp.einsum('bhtd,bhsd->bhts', qc, kc) * w_intra
        num = jnp.einsum('bhts,bhsd->bhtd', s, vc) + w_inter[..., None] * jnp.einsum('bhvk,bhtk->bhtv', C, qc)
        den = jnp.sum(s, axis=-1) + w_inter * jnp.einsum('bhk,bhtk->bht', n, qc)
        h = num / jnp.maximum(jnp.abs(den), jnp.exp(-m_t))[..., None]
        bl = b[..., -1]
        g = bl[..., None] - b + ic
        m_new = jnp.maximum(bl + m, jnp.max(g, axis=-1))
        wg = jnp.exp(g - m_new[..., None])
        decay = jnp.exp(bl + m - m_new)
        C_new = decay[..., None, None] * C + jnp.einsum('bhs,bhsv,bhsk->bhvk', wg, vc, kc)
        n_new = decay[..., None] * n + jnp.einsum('bhs,bhsk->bhk', wg, kc)
        return (C_new, n_new, m_new), h

    (C, n, m), hb = lax.scan(step, (C0.astype(f32), n0.astype(f32), m0.astype(f32)), (qb, kb, vb, ib, fb))
    h = jnp.moveaxis(hb, (0, 2), (1, 3)).reshape(B, T, H, Dh)
    return h, C, n, m


def mlstm_bidir(q, k, v, gates, C0, n0, m0):
    flip = lambda a: a[:, ::-1]
    h_f, Cf, nf, mf = mlstm_scan(q, k, v, gates[:, :, 0], jax.nn.log_sigmoid(gates[:, :, 1]),
                                 C0[:, 0], n0[:, 0], m0[:, 0])
    h_b, Cb, nb, mb = mlstm_scan(flip(q), flip(k), flip(v), flip(gates[:, :, 2]),
                                 flip(jax.nn.log_sigmoid(gates[:, :, 3])), C0[:, 1], n0[:, 1], m0[:, 1])
    return h_f + flip(h_b), jnp.stack([Cf, Cb], 1), jnp.stack([nf, nb], 1), jnp.stack([mf, mb], 1)


def trunk_layer(x, mod, row_len, ffn_axis, C0, n0, m0, norm1, w_in, b_gate, conv_sc_w, conv_sc_b,
                mh_norm, w_out, norm2, w_up, conv_ffn_w, conv_ffn_b, w_down):
    B, T, _ = x.shape
    rows = T // row_len
    shift1, scale1, gate1, shift2, scale2, gate2 = jnp.split(mod, 6, axis=-1)
    h = rmsnorm(x, norm1) * (1 + scale1) + shift1
    z = h @ w_in
    idx = [D_CONV, 2 * D_CONV, 3 * D_CONV, 3 * D_CONV + D_MLSTM, 3 * D_CONV + 2 * D_MLSTM,
           3 * D_CONV + 3 * D_MLSTM, 3 * D_CONV + 4 * D_MLSTM]
    sc_b, sc_c, sc_x, q, k, v, o, g = jnp.split(z, idx, axis=-1)
    u = (sc_c * sc_x).reshape(B, rows, row_len, D_CONV)
    y_sc = sc_b * conv3(u, conv_sc_w, conv_sc_b, axis=2).reshape(B, T, D_CONV)
    heads = lambda a: a.reshape(B, T, N_HEADS, HEAD_DIM)
    gates = g.astype(jnp.float32).reshape(B, T, 4, N_HEADS) + b_gate.astype(jnp.float32)
    h_m, C, n, m = mlstm_bidir(heads(q), heads(k), heads(v), gates, C0, n0, m0)
    h_m = h_m * lax.rsqrt(jnp.mean(h_m * h_m, axis=-1, keepdims=True) + EPS)
    h_m = h_m * mh_norm.reshape(N_HEADS, HEAD_DIM).astype(jnp.float32)
    h_m = h_m.reshape(B, T, D_MLSTM).astype(x.dtype) * jax.nn.sigmoid(o)
    x = x + gate1 * (jnp.concatenate([y_sc, h_m], axis=-1) @ w_out)
    h2 = rmsnorm(x, norm2) * (1 + scale2) + shift2
    up = (h2 @ w_up).reshape(B, rows, row_len, 2 * D_FF)
    up = conv3(up, conv_ffn_w, conv_ffn_b, axis=ffn_axis).reshape(B, T, 2 * D_FF)
    a, gg = jnp.split(up, 2, axis=-1)
    x = x + gate2 * ((jax.nn.silu(gg) * a) @ w_down)
    return x, C, n, m


def setup_inputs(seed: int = 0) -> dict:
    key = jax.random.key(seed)
    ks = jax.random.split(key, 24)
    nrm = lambda k, shape, scale: scale * jax.random.normal(k, shape, jnp.float32)
    forget_bias = jnp.linspace(3.0, 6.0, N_HEADS)
    gate_offset = jnp.zeros((4, N_HEADS), jnp.float32).at[1].set(forget_bias).at[3].set(forget_bias)
    return {
        'x_prompt': nrm(ks[0], (BATCH, SEQ, D_MODEL), 1.0),
        'x_sample': nrm(ks[1], (DEC_BATCH, DEC_SEQ, D_MODEL), 1.0),
        'state_C': nrm(ks[2], (DEC_BATCH, DEPTH, 2, N_HEADS, HEAD_DIM, HEAD_DIM), 0.5),
        'state_n': nrm(ks[3], (DEC_BATCH, DEPTH, 2, N_HEADS, HEAD_DIM), 0.5),
        'state_m': nrm(ks[4], (DEC_BATCH, DEPTH, 2, N_HEADS), 1.0),
        'c': nrm(ks[5], (DEC_BATCH, D_MODEL), 1.0),
        'c_ctx': nrm(ks[6], (D_MODEL,), 1.0),
        'w_mod': nrm(ks[7], (DEPTH, D_MODEL, 6 * D_MODEL), D_MODEL ** -0.5),
        'b_mod': nrm(ks[8], (DEPTH, 6 * D_MODEL), 0.02),
        'norm1': 1.0 + nrm(ks[9], (DEPTH, D_MODEL), 0.02),
        'w_in': nrm(ks[10], (DEPTH, D_MODEL, D_IN), D_MODEL ** -0.5),
        'b_gate': gate_offset + nrm(ks[11], (DEPTH, 4, N_HEADS), 0.1),
        'conv_sc_w': nrm(ks[12], (DEPTH, 3, D_CONV), 3 ** -0.5),
        'conv_sc_b': nrm(ks[13], (DEPTH, D_CONV), 0.02),
        'mh_norm': 1.0 + nrm(ks[14], (DEPTH, D_MLSTM), 0.02),
        'w_out': nrm(ks[15], (DEPTH, D_MIX, D_MODEL), D_MIX ** -0.5),
        'norm2': 1.0 + nrm(ks[16], (DEPTH, D_MODEL), 0.02),
        'w_up': nrm(ks[17], (DEPTH, D_MODEL, 2 * D_FF), D_MODEL ** -0.5),
        'conv_ffn_w': nrm(ks[18], (DEPTH, 3, 2 * D_FF), 3 ** -0.5),
        'conv_ffn_b': nrm(ks[19], (DEPTH, 2 * D_FF), 0.02),
        'w_down': nrm(ks[20], (DEPTH, D_FF, D_MODEL), D_FF ** -0.5),
        'final_norm': 1.0 + nrm(ks[21], (D_MODEL,), 0.02),
    }


def reference(x_prompt, x_sample, state_C, state_n, state_m, c, c_ctx, w_mod, b_mod, norm1, w_in, b_gate,
              conv_sc_w, conv_sc_b, mh_norm, w_out, norm2, w_up, conv_ffn_w, conv_ffn_b, w_down, final_norm):
    B, S, _ = x_prompt.shape
    f32 = jnp.float32
    zC = jnp.zeros((B, 2, N_HEADS, HEAD_DIM, HEAD_DIM), f32)
    zn = jnp.zeros((B, 2, N_HEADS, HEAD_DIM), f32)
    zm = jnp.zeros((B, 2, N_HEADS), f32)
    xp, xs = x_prompt, x_sample
    new_C, new_n, new_m = [], [], []
    for l in range(DEPTH):
        mod_ctx = (jax.nn.silu(c_ctx) @ w_mod[l] + b_mod[l])[None, None, :]
        mod_lat = (jax.nn.silu(c) @ w_mod[l] + b_mod[l])[:, None, :]
        lw = (norm1[l], w_in[l], b_gate[l], conv_sc_w[l], conv_sc_b[l], mh_norm[l], w_out[l],
              norm2[l], w_up[l], conv_ffn_w[l], conv_ffn_b[l], w_down[l])
        xp, Cc, ncc, mc = trunk_layer(xp, mod_ctx, S, 2, zC, zn, zm, *lw)
        xs, _, _, _ = trunk_layer(xs, mod_lat, GRID_W, 1, state_C[:, l], state_n[:, l], state_m[:, l], *lw)
        new_C.append(Cc.astype(x_prompt.dtype))
        new_n.append(ncc.astype(x_prompt.dtype))
        new_m.append(mc.astype(x_prompt.dtype))
    y_prompt = rmsnorm(xp, final_norm)
    y_sample = rmsnorm(xs, final_norm)
    return (y_prompt, y_sample, jnp.stack(new_C, 1), jnp.stack(new_n, 1), jnp.stack(new_m, 1))
```

```python
import functools

import jax
import jax.numpy as jnp
from jax import lax
from jax.experimental import pallas as pl
from jax.experimental.pallas import tpu as pltpu

F32 = jnp.float32
BF16 = jnp.bfloat16

D_MODEL = 1024
GRID_W = 64
D_CONV = 512
D_MLSTM = 512
N_HEADS = 4
HEAD_DIM = 128
N_GATES = 16
D_FF = 2816
CHUNK = 128
EPS = 1e-6

LANES = 128
FF_BLK = 256
N_FF_BLK = D_FF // FF_BLK
VMEM_LIMIT = 56 * 1024 * 1024


def _cparams(n_axes):
    return pltpu.CompilerParams(
        dimension_semantics=("arbitrary",) * n_axes, vmem_limit_bytes=VMEM_LIMIT)


def _rms_scale(x):
    return x * lax.rsqrt(jnp.mean(x * x, axis=-1, keepdims=True) + EPS)


def _dot(a, b):
    return jnp.dot(a, b, preferred_element_type=F32)


def _mod_kernel(c_ref, w_ref, b_ref, o_ref):
    c = c_ref[...]
    s = c * jax.nn.sigmoid(c)
    o_ref[...] = _dot(s.astype(BF16), w_ref[...].astype(BF16)) + b_ref[...]


def _mod_call(c8, w_mod, b_mod):
    n_out = w_mod.shape[1]
    blk = 1024
    return pl.pallas_call(
        _mod_kernel,
        grid=(n_out // blk,),
        in_specs=[
            pl.BlockSpec((8, D_MODEL), lambda i: (0, 0)),
            pl.BlockSpec((D_MODEL, blk), lambda i: (0, i)),
            pl.BlockSpec((1, blk), lambda i: (0, i)),
        ],
        out_specs=pl.BlockSpec((8, blk), lambda i: (0, i)),
        out_shape=jax.ShapeDtypeStruct((8, n_out), F32),
        compiler_params=_cparams(1),
        name="mod",
    )(c8, w_mod, b_mod)


def _chunk_scan(x, pos, op, fill, reverse):
    n = x.shape[0]
    k = 1
    while k < CHUNK:
        if reverse:
            shifted = pltpu.roll(x, n - k, axis=0)
            ok = pos < CHUNK - k
        else:
            shifted = pltpu.roll(x, k, axis=0)
            ok = pos >= k
        x = op(x, jnp.where(ok, shifted, fill))
        k *= 2
    return x


def _in_proj_kernel(row_len, x_ref, mod_ref, n1_ref, wsc_ref, wm_ref, wg_ref, bg_ref, cw_ref, cb_ref,
                    ysc_ref, q_ref, k_ref, v_ref, so_ref, gcol_ref, grow_ref):
    tt = x_ref.shape[0]
    x = x_ref[...]
    mod = mod_ref[...]
    shift1, scale1 = mod[0:1], mod[1:2]
    h = (_rms_scale(x) * n1_ref[...]) * (1.0 + scale1) + shift1
    hb = h.astype(BF16)

    zb = _dot(hb, wsc_ref[:, 0:D_CONV])
    zc = _dot(hb, wsc_ref[:, D_CONV:2 * D_CONV])
    zx = _dot(hb, wsc_ref[:, 2 * D_CONV:3 * D_CONV])
    u = zc * zx
    pos = lax.broadcasted_iota(jnp.int32, u.shape, 0) % row_len
    prev = jnp.where(pos == 0, 0.0, pltpu.roll(u, 1, axis=0))
    nxt = jnp.where(pos == row_len - 1, 0.0, pltpu.roll(u, tt - 1, axis=0))
    cw = cw_ref[...]
    conv = cw[0:1] * prev + cw[1:2] * u + cw[2:3] * nxt + cb_ref[...]
    ysc_ref[...] = (zb * conv).astype(BF16)

    q_ref[...] = (_dot(hb, wm_ref[:, 0:D_MLSTM]) * (HEAD_DIM ** -0.5)).astype(BF16)
    k_ref[...] = _dot(hb, wm_ref[:, D_MLSTM:2 * D_MLSTM]).astype(BF16)
    v_ref[...] = _dot(hb, wm_ref[:, 2 * D_MLSTM:3 * D_MLSTM]).astype(BF16)
    so_ref[...] = jax.nn.sigmoid(_dot(hb, wm_ref[:, 3 * D_MLSTM:4 * D_MLSTM]))

    g = _dot(hb, wg_ref[...]) + bg_ref[...]
    gi = g[:, 0:LANES]
    gf = g[:, LANES:2 * LANES]
    logf = jnp.minimum(gf, 0.0) - jnp.log1p(jnp.exp(-jnp.abs(gf)))
    lane = lax.broadcasted_iota(jnp.int32, gi.shape, 1)
    cpos = lax.broadcasted_iota(jnp.int32, gi.shape, 0) % CHUNK
    fwd = lane < N_HEADS
    b = jnp.where(fwd, _chunk_scan(logf, cpos, jnp.add, 0.0, False),
                  _chunk_scan(logf, cpos, jnp.add, 0.0, True))
    r = gi - b
    cm = jnp.where(fwd, _chunk_scan(r, cpos, jnp.maximum, -jnp.inf, False),
                   _chunk_scan(r, cpos, jnp.maximum, -jnp.inf, True))
    gcol_ref[...] = jnp.where(lane < 8, cm,
                              jnp.where(lane < 16, pltpu.roll(r, 8, axis=1), pltpu.roll(b, 16, axis=1)))
    for c in range(tt // CHUNK):
        grow_ref[c] = r[c * CHUNK:(c + 1) * CHUNK].T[0:8]


def _in_proj_call(x2d, mod3, mod_row, row_len, tt, norm1, w_sc, w_m, w_g, b_g, conv_w, conv_b):
    n_tok = x2d.shape[0]
    n_tiles = n_tok // tt
    const = lambda i: (0, 0)
    tok = lambda i: (i, 0)
    out_shapes = (
        jax.ShapeDtypeStruct((n_tok, D_CONV), BF16),
        jax.ShapeDtypeStruct((n_tok, D_MLSTM), BF16),
        jax.ShapeDtypeStruct((n_tok, D_MLSTM), BF16),
        jax.ShapeDtypeStruct((n_tok, D_MLSTM), BF16),
        jax.ShapeDtypeStruct((n_tok, D_MLSTM), F32),
        jax.ShapeDtypeStruct((n_tok, LANES), F32),
        jax.ShapeDtypeStruct((n_tok // CHUNK, 8, CHUNK), F32),
    )
    return pl.pallas_call(
        functools.partial(_in_proj_kernel, row_len),
        grid=(n_tiles,),
        in_specs=[
            pl.BlockSpec((tt, D_MODEL), tok),
            pl.BlockSpec((None, 6, D_MODEL), lambda i: (mod_row(i), 0, 0)),
            pl.BlockSpec((1, D_MODEL), const),
            pl.BlockSpec(w_sc.shape, const),
            pl.BlockSpec(w_m.shape, const),
            pl.BlockSpec(w_g.shape, const),
            pl.BlockSpec(b_g.shape, const),
            pl.BlockSpec(conv_w.shape, const),
            pl.BlockSpec(conv_b.shape, const),
        ],
        out_specs=[
            pl.BlockSpec((tt, D_CONV), tok),
            pl.BlockSpec((tt, D_MLSTM), tok),
            pl.BlockSpec((tt, D_MLSTM), tok),
            pl.BlockSpec((tt, D_MLSTM), tok),
            pl.BlockSpec((tt, D_MLSTM), tok),
            pl.BlockSpec((tt, LANES), tok),
            pl.BlockSpec((tt // CHUNK, 8, CHUNK), lambda i: (i, 0, 0)),
        ],
        out_shape=out_shapes,
        compiler_params=_cparams(1),
        name="in_proj",
    )(x2d, mod3, norm1, w_sc, w_m, w_g, b_g, conv_w, conv_b)


def _mlstm_kernel(n_chunks, has_init, want_state, *refs):
    refs = list(refs)
    q_ref, k_ref, v_ref, so_ref, gcol_ref, grow_ref, mh_ref = refs[:7]
    refs = refs[7:]
    if has_init:
        c0_ref, n0_ref, m0_ref = refs[:3]
        refs = refs[3:]
    hm_ref = refs[0]
    refs = refs[1:]
    if want_state:
        cout_ref, nout_ref, mout_ref = refs[:3]
        refs = refs[3:]
    e_ref, hacc_ref = refs

    head = pl.program_id(1)
    lane_shift = jnp.where(head == 0, 0, LANES - head)
    t_idx = lax.broadcasted_iota(jnp.int32, (CHUNK, CHUNK), 0)
    s_idx = lax.broadcasted_iota(jnp.int32, (CHUNK, CHUNK), 1)
    ones_b = jnp.ones((CHUNK, HEAD_DIM), BF16)

    if has_init:
        for d in range(2):
            e_ref[d, :, 0:HEAD_DIM] = c0_ref[d].T
            e_ref[d, :, HEAD_DIM:] = jnp.broadcast_to(n0_ref[d], (HEAD_DIM, HEAD_DIM)).T
        m_init = (m0_ref[0][:, 0:1], m0_ref[1][:, 0:1])
    else:
        e_ref[...] = jnp.zeros(e_ref.shape, F32)
        m_init = (jnp.zeros((1, 1), F32), jnp.zeros((1, 1), F32))

    def chunk_step(c, d, m):
        rows = pl.ds(pl.multiple_of(c * CHUNK, CHUNK), CHUNK)
        qc = q_ref[rows, :]
        kc = k_ref[rows, :]
        vc = v_ref[rows, :]
        gc = pltpu.roll(gcol_ref[rows, :], lane_shift, axis=1)
        cm = gc[:, 4 * d:4 * d + 1]
        r = gc[:, 8 + 4 * d:9 + 4 * d]
        b = gc[:, 16 + 4 * d:17 + 4 * d]
        last = CHUNK - 1 if d == 0 else 0
        mask = (s_idx <= t_idx) if d == 0 else (s_idx >= t_idx)
        r_row = grow_ref[c, pl.ds(4 * d + head, 1), :]

        big_m = jnp.maximum(m, cm)
        m_last = jnp.maximum(m, cm[last:last + 1])
        m_new = b[last:last + 1] + m_last
        w = jnp.exp(jnp.where(mask, r_row - big_m, -jnp.inf))
        qk = lax.dot_general(qc, kc, (((1,), (1,)), ((), ())), preferred_element_type=F32)
        s = (qk * w).astype(BF16)
        sv = _dot(s, jnp.concatenate([vc, ones_b], axis=1))
        e = e_ref[d]
        qe = _dot(qc, e.astype(BF16))
        nd = sv + jnp.exp(m - big_m) * qe
        num = nd[:, 0:HEAD_DIM]
        den = nd[:, HEAD_DIM:]
        hout = num / jnp.maximum(jnp.abs(den), jnp.exp(-(b + big_m)))

        wg = jnp.exp(r - m_last)
        wv = jnp.concatenate([(wg * vc.astype(F32)).astype(BF16),
                              jnp.broadcast_to(wg, (CHUNK, HEAD_DIM)).astype(BF16)], axis=1)
        upd = lax.dot_general(kc, wv, (((0,), (0,)), ((), ())), preferred_element_type=F32)
        e_ref[d] = jnp.exp(m - m_last) * e + upd
        return hout, m_new, rows

    def finalize(hsum, rows):
        hn = _rms_scale(hsum) * mh_ref[...]
        hm_ref[rows, :] = (hn * so_ref[rows, :]).astype(BF16)

    def first_half(c, carry):
        m_f, m_b = carry
        h_f, m_f, rows_f = chunk_step(c, 0, m_f)
        hacc_ref[rows_f, :] = h_f
        h_b, m_b, rows_b = chunk_step(n_chunks - 1 - c, 1, m_b)
        hacc_ref[rows_b, :] = h_b
        return m_f, m_b

    def second_half(c, carry):
        m_f, m_b = carry
        h_f, m_f, rows_f = chunk_step(c, 0, m_f)
        finalize(hacc_ref[rows_f, :] + h_f, rows_f)
        h_b, m_b, rows_b = chunk_step(n_chunks - 1 - c, 1, m_b)
        finalize(hacc_ref[rows_b, :] + h_b, rows_b)
        return m_f, m_b

    half = n_chunks // 2
    carry = lax.fori_loop(0, half, first_half, m_init)
    m_f, m_b = lax.fori_loop(half, n_chunks, second_half, carry)

    if want_state:
        for d, m_d in ((0, m_f), (1, m_b)):
            e = e_ref[d]
            cout_ref[d] = e[:, 0:HEAD_DIM].T
            nout_ref[d] = e[:, HEAD_DIM:].T[0:1]
            mout_ref[d] = jnp.broadcast_to(m_d, (1, LANES))


def _mlstm_call(q, k, v, so, gcol, grow, mh_norm, init, want_state):
    bsz, t = q.shape[0], q.shape[1]
    n_chunks = t // CHUNK
    head_blk = lambda b, h: (b, 0, h)
    in_specs = [
        pl.BlockSpec((None, t, HEAD_DIM), head_blk),
        pl.BlockSpec((None, t, HEAD_DIM), head_blk),
        pl.BlockSpec((None, t, HEAD_DIM), head_blk),
        pl.BlockSpec((None, t, HEAD_DIM), head_blk),
        pl.BlockSpec((None, t, LANES), lambda b, h: (b, 0, 0)),
        pl.BlockSpec((None, n_chunks, 8, CHUNK), lambda b, h: (b, 0, 0, 0)),
        pl.BlockSpec((1, HEAD_DIM), lambda b, h: (0, h)),
    ]
    args = [q, k, v, so, gcol, grow, mh_norm]
    if init is not None:
        c0, n0, m0 = init
        in_specs += [
            pl.BlockSpec((None, 2, None, HEAD_DIM, HEAD_DIM), lambda b, h: (b, 0, h, 0, 0)),
            pl.BlockSpec((None, 2, None, 1, HEAD_DIM), lambda b, h: (b, 0, h, 0, 0)),
            pl.BlockSpec((None, 2, None, 1, LANES), lambda b, h: (b, 0, h, 0, 0)),
        ]
        args += [c0, n0, m0]
    out_specs = [pl.BlockSpec((None, t, HEAD_DIM), head_blk)]
    out_shape = [jax.ShapeDtypeStruct((bsz, t, D_MLSTM), BF16)]
    if want_state:
        out_specs += [
            pl.BlockSpec((None, None, 2, None, HEAD_DIM, HEAD_DIM), lambda b, h: (b, 0, 0, h, 0, 0)),
            pl.BlockSpec((None, None, 2, None, 1, HEAD_DIM), lambda b, h: (b, 0, 0, h, 0, 0)),
            pl.BlockSpec((None, None, 2, None, 1, LANES), lambda b, h: (b, 0, 0, h, 0, 0)),
        ]
        out_shape += [
            jax.ShapeDtypeStruct((bsz, 1, 2, N_HEADS, HEAD_DIM, HEAD_DIM), F32),
            jax.ShapeDtypeStruct((bsz, 1, 2, N_HEADS, 1, HEAD_DIM), F32),
            jax.ShapeDtypeStruct((bsz, 1, 2, N_HEADS, 1, LANES), F32),
        ]
    return pl.pallas_call(
        functools.partial(_mlstm_kernel, n_chunks, init is not None, want_state),
        grid=(bsz, N_HEADS),
        in_specs=in_specs,
        out_specs=out_specs,
        out_shape=out_shape,
        scratch_shapes=[
            pltpu.VMEM((2, HEAD_DIM, 2 * HEAD_DIM), F32),
            pltpu.VMEM((t, HEAD_DIM), F32),
        ],
        compiler_params=_cparams(2),
        name="mlstm",
    )(*args)


def _out_proj_kernel(x_ref, ysc_ref, hm_ref, mod_ref, n2_ref, wa_ref, wb_ref, x1_ref, h2_ref):
    mod = mod_ref[...]
    gate1, shift2, scale2 = mod[2:3], mod[3:4], mod[4:5]
    mix = _dot(ysc_ref[...], wa_ref[...]) + _dot(hm_ref[...], wb_ref[...])
    x1 = x_ref[...] + gate1 * mix
    x1_ref[...] = x1
    h2_ref[...] = ((_rms_scale(x1) * n2_ref[...]) * (1.0 + scale2) + shift2).astype(BF16)


def _out_proj_call(x2d, ysc, hm, mod3, mod_row, tt, norm2, w_out_a, w_out_b):
    n_tok = x2d.shape[0]
    const = lambda i: (0, 0)
    tok = lambda i: (i, 0)
    return pl.pallas_call(
        _out_proj_kernel,
        grid=(n_tok // tt,),
        in_specs=[
            pl.BlockSpec((tt, D_MODEL), tok),
            pl.BlockSpec((tt, D_CONV), tok),
            pl.BlockSpec((tt, D_MLSTM), tok),
            pl.BlockSpec((None, 6, D_MODEL), lambda i: (mod_row(i), 0, 0)),
            pl.BlockSpec((1, D_MODEL), const),
            pl.BlockSpec(w_out_a.shape, const),
            pl.BlockSpec(w_out_b.shape, const),
        ],
        out_specs=[pl.BlockSpec((tt, D_MODEL), tok), pl.BlockSpec((tt, D_MODEL), tok)],
        out_shape=[jax.ShapeDtypeStruct((n_tok, D_MODEL), F32),
                   jax.ShapeDtypeStruct((n_tok, D_MODEL), BF16)],
        compiler_params=_cparams(1),
        name="out_proj",
    )(x2d, ysc, hm, mod3, norm2, w_out_a, w_out_b)


def _ffn_kernel(halo, seq_len, tiles_per_image, *refs):
    if halo:
        (h2_ref, prev_ref, next_ref, x1_ref, mod_ref, fn_ref, wa_ref, wg_ref, cwa_ref, cwg_ref,
         cba_ref, cbg_ref, wd_ref, y_ref, acc_ref, h2e_ref) = refs
    else:
        (h2_ref, x1_ref, mod_ref, fn_ref, wa_ref, wg_ref, cwa_ref, cwg_ref,
         cba_ref, cbg_ref, wd_ref, y_ref, acc_ref) = refs
    tt = x1_ref.shape[0]
    i = pl.program_id(0)
    j = pl.program_id(1)

    @pl.when(j == 0)
    def _():
        acc_ref[...] = jnp.zeros(acc_ref.shape, F32)
        if halo:
            h2e_ref[halo:halo + tt] = h2_ref[...]
            h2e_ref[0:halo] = prev_ref[...]
            h2e_ref[halo + tt:] = next_ref[...]

            @pl.when(i % tiles_per_image == 0)
            def _():
                h2e_ref[0:halo] = jnp.zeros((halo, D_MODEL), BF16)

            @pl.when(i % tiles_per_image == tiles_per_image - 1)
            def _():
                h2e_ref[halo + tt:] = jnp.zeros((halo, D_MODEL), BF16)

    hin = h2e_ref[...] if halo else h2_ref[...]
    a = _dot(hin, wa_ref[...])
    g = _dot(hin, wg_ref[...])
    cwa, cwg = cwa_ref[...], cwg_ref[...]
    if halo:
        ac = cwa[0:1] * a[0:tt] + cwa[1:2] * a[halo:halo + tt] + cwa[2:3] * a[2 * halo:] + cba_ref[...]
        gc = cwg[0:1] * g[0:tt] + cwg[1:2] * g[halo:halo + tt] + cwg[2:3] * g[2 * halo:] + cbg_ref[...]
    else:
        pos = lax.broadcasted_iota(jnp.int32, a.shape, 0) % seq_len
        first, last = pos == 0, pos == seq_len - 1

        def conv(z, cw, cb):
            prev = jnp.where(first, 0.0, pltpu.roll(z, 1, axis=0))
            nxt = jnp.where(last, 0.0, pltpu.roll(z, tt - 1, axis=0))
            return cw[0:1] * prev + cw[1:2] * z + cw[2:3] * nxt + cb

        ac = conv(a, cwa, cba_ref[...])
        gc = conv(g, cwg, cbg_ref[...])
    act = (gc * jax.nn.sigmoid(gc) * ac).astype(BF16)
    acc_ref[...] += _dot(act, wd_ref[...])

    @pl.when(j == N_FF_BLK - 1)
    def _():
        gate2 = mod_ref[5:6]
        x2 = x1_ref[...] + gate2 * acc_ref[...]
        y_ref[...] = _rms_scale(x2) * fn_ref[...]


def _ffn_call(h2, x1, mod3, mod_row, tt, halo, seq_len, final_norm, w_up, conv_w, conv_b, w_down):
    n_tok = x1.shape[0]
    n_tiles = n_tok // tt
    tok = lambda i, j: (i, 0)
    in_specs = [pl.BlockSpec((tt, D_MODEL), tok)]
    args = [h2]
    scratch = [pltpu.VMEM((tt, D_MODEL), F32)]
    tiles_per_image = 1
    if halo:
        tiles_per_image = seq_len // tt
        per = tt // halo
        n_halo_blk = n_tok // halo
        in_specs += [
            pl.BlockSpec((halo, D_MODEL), lambda i, j: (jnp.maximum(i * per - 1, 0), 0)),
            pl.BlockSpec((halo, D_MODEL), lambda i, j: (jnp.minimum((i + 1) * per, n_halo_blk - 1), 0)),
        ]
        args += [h2, h2]
        scratch.append(pltpu.VMEM((tt + 2 * halo, D_MODEL), BF16))
    in_specs += [
        pl.BlockSpec((tt, D_MODEL), tok),
        pl.BlockSpec((None, 6, D_MODEL), lambda i, j: (mod_row(i), 0, 0)),
        pl.BlockSpec((1, D_MODEL), lambda i, j: (0, 0)),
        pl.BlockSpec((D_MODEL, FF_BLK), lambda i, j: (0, j)),
        pl.BlockSpec((D_MODEL, FF_BLK), lambda i, j: (0, N_FF_BLK + j)),
        pl.BlockSpec((3, FF_BLK), lambda i, j: (0, j)),
        pl.BlockSpec((3, FF_BLK), lambda i, j: (0, N_FF_BLK + j)),
        pl.BlockSpec((1, FF_BLK), lambda i, j: (0, j)),
        pl.BlockSpec((1, FF_BLK), lambda i, j: (0, N_FF_BLK + j)),
        pl.BlockSpec((FF_BLK, D_MODEL), lambda i, j: (j, 0)),
    ]
    args += [x1, mod3, final_norm, w_up, w_up, conv_w, conv_w, conv_b, conv_b, w_down]
    return pl.pallas_call(
        functools.partial(_ffn_kernel, halo, seq_len, tiles_per_image),
        grid=(n_tiles, N_FF_BLK),
        in_specs=in_specs,
        out_specs=pl.BlockSpec((tt, D_MODEL), tok),
        out_shape=jax.ShapeDtypeStruct((n_tok, D_MODEL), F32),
        scratch_shapes=scratch,
        compiler_params=_cparams(2),
        name="ffn",
    )(*args)


def _trunk(x, mod3, mod_row_of_batch, row_len, ffn_halo, init, want_state, tt_in, tt_ffn, w):
    bsz, t, _ = x.shape
    x2d = x.reshape(bsz * t, D_MODEL)

    def mod_row(tile_tokens):
        return lambda i: mod_row_of_batch((i * tile_tokens) // t)

    ysc, q, k, v, so, gcol, grow = _in_proj_call(
        x2d, mod3, mod_row(tt_in), row_len, tt_in, w["norm1"], w["w_sc"], w["w_m"], w["w_g"], w["b_g"],
        w["conv_sc_w"], w["conv_sc_b"])
    r3 = lambda a: a.reshape(bsz, t, a.shape[-1])
    outs = _mlstm_call(r3(q), r3(k), r3(v), r3(so), r3(gcol),
                       grow.reshape(bsz, t // CHUNK, 8, CHUNK), w["mh_norm"], init, want_state)
    hm = outs[0].reshape(bsz * t, D_MLSTM)
    x1, h2 = _out_proj_call(x2d, ysc, hm, mod3, mod_row(tt_in), tt_in, w["norm2"], w["w_out_a"], w["w_out_b"])
    y = _ffn_call(h2, x1, mod3, mod_row(tt_ffn), tt_ffn, ffn_halo, t, w["final_norm"],
                  w["w_up"], w["conv_ffn_w"], w["conv_ffn_b"], w["w_down"])
    return y.reshape(bsz, t, D_MODEL), outs[1:]


def kernel(x_prompt, x_sample, state_C, state_n, state_m, c, c_ctx, w_mod, b_mod, norm1, w_in, b_gate,
           conv_sc_w, conv_sc_b, mh_norm, w_out, norm2, w_up, conv_ffn_w, conv_ffn_b, w_down, final_norm):
    n_lat = c.shape[0]
    n_ctx = x_prompt.shape[0]
    c8 = jnp.concatenate([c, c_ctx[None], jnp.zeros((8 - n_lat - 1, D_MODEL), F32)], axis=0)
    l = 0
    mod3 = _mod_call(c8, w_mod[l], b_mod[l][None]).reshape(8, 6, D_MODEL)

    wi = w_in[l]
    g0 = 3 * D_CONV + 4 * D_MLSTM
    zpad = jnp.zeros((D_MODEL, LANES - 2 * N_HEADS), F32)
    cols = lambda a: wi[:, g0 + a * N_HEADS:g0 + (a + 1) * N_HEADS]
    w_g = jnp.concatenate([cols(0), cols(2), zpad, cols(1), cols(3), zpad], axis=1).astype(BF16)
    bpad = jnp.zeros((LANES - 2 * N_HEADS,), F32)
    bg = b_gate[l].astype(F32)
    b_g = jnp.concatenate([bg[0], bg[2], bpad, bg[1], bg[3], bpad])[None]
    w = dict(
        norm1=norm1[l][None], norm2=norm2[l][None], final_norm=final_norm[None],
        w_sc=wi[:, 0:3 * D_CONV].astype(BF16),
        w_m=wi[:, 3 * D_CONV:g0].astype(BF16),
        w_g=w_g, b_g=b_g,
        conv_sc_w=conv_sc_w[l], conv_sc_b=conv_sc_b[l][None],
        mh_norm=mh_norm[l][None],
        w_out_a=w_out[l][0:D_CONV].astype(BF16), w_out_b=w_out[l][D_CONV:].astype(BF16),
        w_up=w_up[l].astype(BF16), conv_ffn_w=conv_ffn_w[l], conv_ffn_b=conv_ffn_b[l][None],
        w_down=w_down[l].astype(BF16),
    )

    seq = x_prompt.shape[1]
    y_prompt, (new_c, new_n, new_m) = _trunk(
        x_prompt, mod3, lambda b: n_lat, seq, 0, None, True, 512, 1024, w)

    init = (state_C[:, l], state_n[:, l][:, :, :, None, :],
            jnp.broadcast_to(state_m[:, l][:, :, :, None, None], state_m[:, l].shape + (1, LANES)))
    y_sample, _ = _trunk(x_sample, mod3, lambda b: b, GRID_W, GRID_W, init, False, 512, 1024, w)

    new_n = new_n.reshape(n_ctx, 1, 2, N_HEADS, HEAD_DIM)
    new_m = new_m[..., 0, 0]
    return y_prompt, y_sample, new_c, new_n, new_m
```

```python
import functools

import jax
import jax.numpy as jnp
from jax import lax
from jax.experimental import pallas as pl
from jax.experimental.pallas import tpu as pltpu

F32 = jnp.float32
BF16 = jnp.bfloat16

D_MODEL = 1024
GRID_W = 64
D_CONV = 512
D_MLSTM = 512
N_HEADS = 4
HEAD_DIM = 128
D_FF = 2816
CHUNK = 128
EPS = 1e-6

LANES = 128
BF16_ROWS = 16
FF_BLK = 256
N_FF_BLK = D_FF // FF_BLK
VMEM_LIMIT = 56 * 1024 * 1024
N_CHAINS = 2 * N_HEADS
E_ROWS = HEAD_DIM + BF16_ROWS


def _cparams(n_axes):
    return pltpu.CompilerParams(
        dimension_semantics=("arbitrary",) * n_axes, vmem_limit_bytes=VMEM_LIMIT)


def _rms_scale(x):
    return x * lax.rsqrt(jnp.mean(x * x, axis=-1, keepdims=True) + EPS)


def _dot(a, b):
    return jnp.dot(a, b, preferred_element_type=F32)


def _dot_nt(a, b):
    return lax.dot_general(a, b, (((1,), (1,)), ((), ())), preferred_element_type=F32)


def _dot_tn(a, b):
    return lax.dot_general(a, b, (((0,), (0,)), ((), ())), preferred_element_type=F32)


def _mod_kernel(c_ref, w_ref, b_ref, o_ref):
    c = c_ref[...]
    s = c * jax.nn.sigmoid(c)
    o_ref[...] = _dot(s.astype(BF16), w_ref[...].astype(BF16)) + b_ref[...]


def _mod_call(c8, w_mod, b_mod):
    n_out = w_mod.shape[1]
    blk = 1024
    return pl.pallas_call(
        _mod_kernel,
        grid=(n_out // blk,),
        in_specs=[
            pl.BlockSpec((8, D_MODEL), lambda i: (0, 0)),
            pl.BlockSpec((D_MODEL, blk), lambda i: (0, i)),
            pl.BlockSpec((1, blk), lambda i: (0, i)),
        ],
        out_specs=pl.BlockSpec((8, blk), lambda i: (0, i)),
        out_shape=jax.ShapeDtypeStruct((8, n_out), F32),
        compiler_params=_cparams(1),
        name="mod",
    )(c8, w_mod, b_mod)


def _chunk_scan(x, pos, op, fill, reverse):
    n = x.shape[0]
    k = 1
    while k < CHUNK:
        if reverse:
            shifted = pltpu.roll(x, n - k, axis=0)
            ok = pos < CHUNK - k
        else:
            shifted = pltpu.roll(x, k, axis=0)
            ok = pos >= k
        x = op(x, jnp.where(ok, shifted, fill))
        k *= 2
    return x


def _in_proj_kernel(row_len, x_ref, mod_ref, n1_ref, wsc_ref, wqk_ref, wvot_ref, wg_ref, bg_ref, cw_ref, cb_ref,
                    ysc_ref, q_ref, k_ref, vt_ref, sot_ref, gcol_ref, grow_ref):
    tt = x_ref.shape[0]
    x = x_ref[...]
    mod = mod_ref[...]
    shift1, scale1 = mod[0:1], mod[1:2]
    h = (_rms_scale(x) * n1_ref[...]) * (1.0 + scale1) + shift1
    hb = h.astype(BF16)

    zb = _dot(hb, wsc_ref[:, 0:D_CONV])
    zc = _dot(hb, wsc_ref[:, D_CONV:2 * D_CONV])
    zx = _dot(hb, wsc_ref[:, 2 * D_CONV:3 * D_CONV])
    u = zc * zx
    pos = lax.broadcasted_iota(jnp.int32, u.shape, 0) % row_len
    prev = jnp.where(pos == 0, 0.0, pltpu.roll(u, 1, axis=0))
    nxt = jnp.where(pos == row_len - 1, 0.0, pltpu.roll(u, tt - 1, axis=0))
    cw = cw_ref[...]
    conv = cw[0:1] * prev + cw[1:2] * u + cw[2:3] * nxt + cb_ref[...]
    ysc_ref[...] = (zb * conv).astype(BF16)

    q_ref[...] = (_dot(hb, wqk_ref[:, 0:D_MLSTM]) * (HEAD_DIM ** -0.5)).astype(BF16)
    k_ref[...] = _dot(hb, wqk_ref[:, D_MLSTM:2 * D_MLSTM]).astype(BF16)
    vt_ref[...] = _dot_nt(wvot_ref[0:D_MLSTM, :], hb).astype(BF16)
    sot_ref[...] = jax.nn.sigmoid(_dot_nt(wvot_ref[D_MLSTM:2 * D_MLSTM, :], hb))

    g = _dot(hb, wg_ref[...]) + bg_ref[...]
    gi = g[:, 0:LANES]
    gf = g[:, LANES:2 * LANES]
    logf = jnp.minimum(gf, 0.0) - jnp.log1p(jnp.exp(-jnp.abs(gf)))
    lane = lax.broadcasted_iota(jnp.int32, gi.shape, 1)
    cpos = lax.broadcasted_iota(jnp.int32, gi.shape, 0) % CHUNK
    fwd = lane < N_HEADS
    b = jnp.where(fwd, _chunk_scan(logf, cpos, jnp.add, 0.0, False),
                  _chunk_scan(logf, cpos, jnp.add, 0.0, True))
    r = gi - b
    cm = jnp.where(fwd, _chunk_scan(r, cpos, jnp.maximum, -jnp.inf, False),
                   _chunk_scan(r, cpos, jnp.maximum, -jnp.inf, True))
    packed = jnp.where(lane < N_CHAINS, cm,
                       jnp.where(lane < 2 * N_CHAINS, pltpu.roll(r, N_CHAINS, axis=1),
                                 pltpu.roll(b, 2 * N_CHAINS, axis=1)))
    gcol_ref[...] = packed
    for c in range(tt // CHUNK):
        grow_ref[c] = packed[c * CHUNK:(c + 1) * CHUNK].T[0:3 * N_CHAINS]


def _in_proj_call(x2d, mod3, mod_row, row_len, tt, w):
    n_tok = x2d.shape[0]
    n_tiles = n_tok // tt
    const = lambda i: (0, 0)
    tok = lambda i: (i, 0)
    tok_t = lambda i: (0, i)
    weights = [w["norm1"], w["w_sc"], w["w_qk"], w["w_vot"], w["w_g"], w["b_g"], w["conv_sc_w"], w["conv_sc_b"]]
    out_shapes = (
        jax.ShapeDtypeStruct((n_tok, D_CONV), BF16),
        jax.ShapeDtypeStruct((n_tok, D_MLSTM), BF16),
        jax.ShapeDtypeStruct((n_tok, D_MLSTM), BF16),
        jax.ShapeDtypeStruct((D_MLSTM, n_tok), BF16),
        jax.ShapeDtypeStruct((D_MLSTM, n_tok), F32),
        jax.ShapeDtypeStruct((n_tok, LANES), F32),
        jax.ShapeDtypeStruct((n_tok // CHUNK, 3 * N_CHAINS, CHUNK), F32),
    )
    return pl.pallas_call(
        functools.partial(_in_proj_kernel, row_len),
        grid=(n_tiles,),
        in_specs=[
            pl.BlockSpec((tt, D_MODEL), tok),
            pl.BlockSpec((None, 6, D_MODEL), lambda i: (mod_row(i), 0, 0)),
        ] + [pl.BlockSpec(a.shape, const) for a in weights],
        out_specs=[
            pl.BlockSpec((tt, D_CONV), tok),
            pl.BlockSpec((tt, D_MLSTM), tok),
            pl.BlockSpec((tt, D_MLSTM), tok),
            pl.BlockSpec((D_MLSTM, tt), tok_t),
            pl.BlockSpec((D_MLSTM, tt), tok_t),
            pl.BlockSpec((tt, LANES), tok),
            pl.BlockSpec((tt // CHUNK, 3 * N_CHAINS, CHUNK), lambda i: (i, 0, 0)),
        ],
        out_shape=out_shapes,
        compiler_params=_cparams(1),
        name="in_proj",
    )(x2d, mod3, *weights)


def _mlstm_kernel(cg, has_init, want_state, *refs):
    refs = list(refs)
    fwd_refs, bwd_refs = refs[0:5], refs[5:10]
    refs = refs[10:]
    if has_init:
        c0_ref, n0_ref, m0_ref = refs[:3]
        refs = refs[3:]
    htf_ref, htb_ref = refs[:2]
    refs = refs[2:]
    if want_state:
        cout_ref, nout_ref, mout_ref = refs[:3]
        refs = refs[3:]
    e_ref, m_ref = refs

    g = pl.program_id(1)

    @pl.when(g == 0)
    def _():
        if has_init:
            for d in range(2):
                for hd in range(N_HEADS):
                    j = d * N_HEADS + hd
                    e_ref[j, 0:HEAD_DIM] = c0_ref[d, hd]
                    e_ref[j, HEAD_DIM:] = jnp.broadcast_to(n0_ref[d, hd], (BF16_ROWS, HEAD_DIM))
                    m_ref[j:j + 1, :] = m0_ref[d, hd]
        else:
            e_ref[...] = jnp.zeros(e_ref.shape, F32)
            m_ref[...] = jnp.zeros(m_ref.shape, F32)

    t_idx = lax.broadcasted_iota(jnp.int32, (CHUNK, CHUNK), 0)
    s_idx = lax.broadcasted_iota(jnp.int32, (CHUNK, CHUNK), 1)
    ones_rows = jnp.ones((BF16_ROWS, CHUNK), BF16)

    def chain_step(c, d, hd):
        q_ref, k_ref, vt_ref, row_ref, col_ref = fwd_refs if d == 0 else bwd_refs
        ht_ref = htf_ref if d == 0 else htb_ref
        j = d * N_HEADS + hd
        toks = slice(c * CHUNK, (c + 1) * CHUNK)
        feat = slice(hd * HEAD_DIM, (hd + 1) * HEAD_DIM)
        last = CHUNK - 1 if d == 0 else 0
        mask = (s_idx <= t_idx) if d == 0 else (s_idx >= t_idx)

        qc = q_ref[toks, feat]
        kc = k_ref[toks, feat]
        vtc = vt_ref[feat, toks]
        cm_row = row_ref[c, j:j + 1, :]
        r_row = row_ref[c, N_CHAINS + j:N_CHAINS + j + 1, :]
        b_row = row_ref[c, 2 * N_CHAINS + j:2 * N_CHAINS + j + 1, :]
        cm_col = col_ref[toks, j:j + 1]
        m_row = m_ref[j:j + 1, :]

        big_m_col = jnp.maximum(m_row[:, j:j + 1], cm_col)
        big_m_row = jnp.maximum(m_row, cm_row)
        m_last = big_m_row[:, last:last + 1]

        w = jnp.exp(jnp.where(mask, r_row - big_m_col, -jnp.inf))
        s = (_dot_nt(qc, kc) * w).astype(BF16)
        num_t = _dot_nt(jnp.concatenate([vtc, ones_rows], axis=0), s)
        e = e_ref[j]
        inter_t = _dot_nt(e.astype(BF16), qc)
        nd = num_t + jnp.exp(m_row - big_m_row) * inter_t
        inv = 1.0 / jnp.maximum(jnp.abs(nd[HEAD_DIM:HEAD_DIM + 1]), jnp.exp(-(b_row + big_m_row)))
        ht_ref[feat, toks] = nd[0:HEAD_DIM] * inv

        wg = jnp.exp(r_row - m_last)
        lhs = jnp.concatenate([(vtc.astype(F32) * wg).astype(BF16),
                               jnp.broadcast_to(wg, (BF16_ROWS, CHUNK)).astype(BF16)], axis=0)
        e_ref[j] = jnp.exp(m_row - m_last) * e + _dot(lhs, kc)
        m_ref[j:j + 1, :] = jnp.broadcast_to(b_row[:, last:last + 1] + m_last, (1, LANES))

    for c in range(cg):
        for d in range(2):
            for hd in range(N_HEADS):
                chain_step(c if d == 0 else cg - 1 - c, d, hd)

    if want_state:
        @pl.when(g == pl.num_programs(1) - 1)
        def _():
            for d in range(2):
                for hd in range(N_HEADS):
                    j = d * N_HEADS + hd
                    cout_ref[d, hd] = e_ref[j, 0:HEAD_DIM]
                    nout_ref[d, hd] = e_ref[j, HEAD_DIM:HEAD_DIM + 1]
                    mout_ref[d, hd] = m_ref[j:j + 1, :]


def _mlstm_call(bsz, t, cg, q, k, vt, gcol, grow, init, want_state):
    n_tok = bsz * t
    tg = cg * CHUNK
    n_groups = t // tg
    fwd = lambda b, g: b * n_groups + g
    bwd = lambda b, g: b * n_groups + (n_groups - 1 - g)

    def stream_specs(pos):
        return [
            pl.BlockSpec((tg, D_MLSTM), lambda b, g: (pos(b, g), 0)),
            pl.BlockSpec((tg, D_MLSTM), lambda b, g: (pos(b, g), 0)),
            pl.BlockSpec((D_MLSTM, tg), lambda b, g: (0, pos(b, g))),
            pl.BlockSpec((cg, 3 * N_CHAINS, CHUNK), lambda b, g: (pos(b, g), 0, 0)),
            pl.BlockSpec((tg, LANES), lambda b, g: (pos(b, g), 0)),
        ]

    in_specs = stream_specs(fwd) + stream_specs(bwd)
    args = [q, k, vt, grow, gcol] * 2
    if init is not None:
        c0, n0, m0 = init
        in_specs += [
            pl.BlockSpec((None, 2, N_HEADS, HEAD_DIM, HEAD_DIM), lambda b, g: (b, 0, 0, 0, 0)),
            pl.BlockSpec((None, 2, N_HEADS, 1, HEAD_DIM), lambda b, g: (b, 0, 0, 0, 0)),
            pl.BlockSpec((None, 2, N_HEADS, 1, LANES), lambda b, g: (b, 0, 0, 0, 0)),
        ]
        args += [c0, n0, m0]
    out_specs = [
        pl.BlockSpec((D_MLSTM, tg), lambda b, g: (0, fwd(b, g))),
        pl.BlockSpec((D_MLSTM, tg), lambda b, g: (0, bwd(b, g))),
    ]
    out_shape = [jax.ShapeDtypeStruct((D_MLSTM, n_tok), F32)] * 2
    if want_state:
        out_specs += [
            pl.BlockSpec((None, None, 2, N_HEADS, HEAD_DIM, HEAD_DIM), lambda b, g: (b, 0, 0, 0, 0, 0)),
            pl.BlockSpec((None, None, 2, N_HEADS, 1, HEAD_DIM), lambda b, g: (b, 0, 0, 0, 0, 0)),
            pl.BlockSpec((None, None, 2, N_HEADS, 1, LANES), lambda b, g: (b, 0, 0, 0, 0, 0)),
        ]
        out_shape += [
            jax.ShapeDtypeStruct((bsz, 1, 2, N_HEADS, HEAD_DIM, HEAD_DIM), F32),
            jax.ShapeDtypeStruct((bsz, 1, 2, N_HEADS, 1, HEAD_DIM), F32),
            jax.ShapeDtypeStruct((bsz, 1, 2, N_HEADS, 1, LANES), F32),
        ]
    return pl.pallas_call(
        functools.partial(_mlstm_kernel, cg, init is not None, want_state),
        grid=(bsz, n_groups),
        in_specs=in_specs,
        out_specs=out_specs,
        out_shape=out_shape,
        scratch_shapes=[
            pltpu.VMEM((N_CHAINS, E_ROWS, HEAD_DIM), F32),
            pltpu.VMEM((N_CHAINS, LANES), F32),
        ],
        compiler_params=_cparams(2),
        name="mlstm",
    )(*args)


def _out_proj_kernel(x_ref, ysc_ref, htf_ref, htb_ref, sot_ref, mh_ref, mod_ref, n2_ref, wa_ref, wb_ref,
                     x1_ref, h2_ref):
    mod = mod_ref[...]
    gate1, shift2, scale2 = mod[2:3], mod[3:4], mod[4:5]
    hs = htf_ref[...] + htb_ref[...]
    heads = []
    for hd in range(N_HEADS):
        blk = hs[hd * HEAD_DIM:(hd + 1) * HEAD_DIM]
        heads.append(blk * lax.rsqrt(jnp.mean(blk * blk, axis=0, keepdims=True) + EPS))
    hm_t = ((jnp.concatenate(heads, axis=0) * mh_ref[...]) * sot_ref[...]).astype(BF16)
    mix = _dot(ysc_ref[...], wa_ref[...]) + _dot_tn(hm_t, wb_ref[...])
    x1 = x_ref[...] + gate1 * mix
    x1_ref[...] = x1
    h2_ref[...] = ((_rms_scale(x1) * n2_ref[...]) * (1.0 + scale2) + shift2).astype(BF16)


def _out_proj_call(x2d, ysc, htf, htb, sot, mod3, mod_row, tt, w):
    n_tok = x2d.shape[0]
    const = lambda i: (0, 0)
    tok = lambda i: (i, 0)
    tok_t = lambda i: (0, i)
    return pl.pallas_call(
        _out_proj_kernel,
        grid=(n_tok // tt,),
        in_specs=[
            pl.BlockSpec((tt, D_MODEL), tok),
            pl.BlockSpec((tt, D_CONV), tok),
            pl.BlockSpec((D_MLSTM, tt), tok_t),
            pl.BlockSpec((D_MLSTM, tt), tok_t),
            pl.BlockSpec((D_MLSTM, tt), tok_t),
            pl.BlockSpec((D_MLSTM, 1), const),
            pl.BlockSpec((None, 6, D_MODEL), lambda i: (mod_row(i), 0, 0)),
            pl.BlockSpec((1, D_MODEL), const),
            pl.BlockSpec(w["w_out_a"].shape, const),
            pl.BlockSpec(w["w_out_b"].shape, const),
        ],
        out_specs=[pl.BlockSpec((tt, D_MODEL), tok), pl.BlockSpec((tt, D_MODEL), tok)],
        out_shape=[jax.ShapeDtypeStruct((n_tok, D_MODEL), F32),
                   jax.ShapeDtypeStruct((n_tok, D_MODEL), BF16)],
        compiler_params=_cparams(1),
        name="out_proj",
    )(x2d, ysc, htf, htb, sot, w["mh_norm_col"], mod3, w["norm2"], w["w_out_a"], w["w_out_b"])


def _ffn_kernel(halo, seq_len, tiles_per_image, *refs):
    if halo:
        (h2_ref, prev_ref, next_ref, x1_ref, mod_ref, fn_ref, wa_ref, wg_ref, cwa_ref, cwg_ref,
         cba_ref, cbg_ref, wd_ref, y_ref, acc_ref, h2e_ref) = refs
    else:
        (h2_ref, x1_ref, mod_ref, fn_ref, wa_ref, wg_ref, cwa_ref, cwg_ref,
         cba_ref, cbg_ref, wd_ref, y_ref, acc_ref) = refs
    tt = x1_ref.shape[0]
    i = pl.program_id(0)
    j = pl.program_id(1)

    @pl.when(j == 0)
    def _():
        acc_ref[...] = jnp.zeros(acc_ref.shape, F32)
        if halo:
            h2e_ref[halo:halo + tt] = h2_ref[...]
            h2e_ref[0:halo] = prev_ref[...]
            h2e_ref[halo + tt:] = next_ref[...]

            @pl.when(i % tiles_per_image == 0)
            def _():
                h2e_ref[0:halo] = jnp.zeros((halo, D_MODEL), BF16)

            @pl.when(i % tiles_per_image == tiles_per_image - 1)
            def _():
                h2e_ref[halo + tt:] = jnp.zeros((halo, D_MODEL), BF16)

    hin = h2e_ref[...] if halo else h2_ref[...]
    a = _dot(hin, wa_ref[...])
    g = _dot(hin, wg_ref[...])
    cwa, cwg = cwa_ref[...], cwg_ref[...]
    if halo:
        ac = cwa[0:1] * a[0:tt] + cwa[1:2] * a[halo:halo + tt] + cwa[2:3] * a[2 * halo:] + cba_ref[...]
        gc = cwg[0:1] * g[0:tt] + cwg[1:2] * g[halo:halo + tt] + cwg[2:3] * g[2 * halo:] + cbg_ref[...]
    else:
        pos = lax.broadcasted_iota(jnp.int32, a.shape, 0) % seq_len
        first, last = pos == 0, pos == seq_len - 1

        def conv(z, cw, cb):
            prev = jnp.where(first, 0.0, pltpu.roll(z, 1, axis=0))
            nxt = jnp.where(last, 0.0, pltpu.roll(z, tt - 1, axis=0))
            return cw[0:1] * prev + cw[1:2] * z + cw[2:3] * nxt + cb

        ac = conv(a, cwa, cba_ref[...])
        gc = conv(g, cwg, cbg_ref[...])
    act = (gc * jax.nn.sigmoid(gc) * ac).astype(BF16)
    acc_ref[...] += _dot(act, wd_ref[...])

    @pl.when(j == N_FF_BLK - 1)
    def _():
        gate2 = mod_ref[5:6]
        x2 = x1_ref[...] + gate2 * acc_ref[...]
        y_ref[...] = _rms_scale(x2) * fn_ref[...]


def _ffn_call(h2, x1, mod3, mod_row, tt, halo, seq_len, w):
    n_tok = x1.shape[0]
    n_tiles = n_tok // tt
    tok = lambda i, j: (i, 0)
    in_specs = [pl.BlockSpec((tt, D_MODEL), tok)]
    args = [h2]
    scratch = [pltpu.VMEM((tt, D_MODEL), F32)]
    tiles_per_image = 1
    if halo:
        tiles_per_image = seq_len // tt
        per = tt // halo
        n_halo_blk = n_tok // halo
        in_specs += [
            pl.BlockSpec((halo, D_MODEL), lambda i, j: (jnp.maximum(i * per - 1, 0), 0)),
            pl.BlockSpec((halo, D_MODEL), lambda i, j: (jnp.minimum((i + 1) * per, n_halo_blk - 1), 0)),
        ]
        args += [h2, h2]
        scratch.append(pltpu.VMEM((tt + 2 * halo, D_MODEL), BF16))
    in_specs += [
        pl.BlockSpec((tt, D_MODEL), tok),
        pl.BlockSpec((None, 6, D_MODEL), lambda i, j: (mod_row(i), 0, 0)),
        pl.BlockSpec((1, D_MODEL), lambda i, j: (0, 0)),
        pl.BlockSpec((D_MODEL, FF_BLK), lambda i, j: (0, j)),
        pl.BlockSpec((D_MODEL, FF_BLK), lambda i, j: (0, N_FF_BLK + j)),
        pl.BlockSpec((3, FF_BLK), lambda i, j: (0, j)),
        pl.BlockSpec((3, FF_BLK), lambda i, j: (0, N_FF_BLK + j)),
        pl.BlockSpec((1, FF_BLK), lambda i, j: (0, j)),
        pl.BlockSpec((1, FF_BLK), lambda i, j: (0, N_FF_BLK + j)),
        pl.BlockSpec((FF_BLK, D_MODEL), lambda i, j: (j, 0)),
    ]
    args += [x1, mod3, w["final_norm"], w["w_up"], w["w_up"], w["conv_ffn_w"], w["conv_ffn_w"],
             w["conv_ffn_b"], w["conv_ffn_b"], w["w_down"]]
    return pl.pallas_call(
        functools.partial(_ffn_kernel, halo, seq_len, tiles_per_image),
        grid=(n_tiles, N_FF_BLK),
        in_specs=in_specs,
        out_specs=pl.BlockSpec((tt, D_MODEL), tok),
        out_shape=jax.ShapeDtypeStruct((n_tok, D_MODEL), F32),
        scratch_shapes=scratch,
        compiler_params=_cparams(2),
        name="ffn",
    )(*args)


def _trunk(x, mod3, mod_row_of_batch, row_len, ffn_halo, init, want_state, tt_in, tt_ffn, cg, w):
    bsz, t, _ = x.shape
    x2d = x.reshape(bsz * t, D_MODEL)

    def mod_row(tile_tokens):
        return lambda i: mod_row_of_batch((i * tile_tokens) // t)

    ysc, q, k, vt, sot, gcol, grow = _in_proj_call(x2d, mod3, mod_row(tt_in), row_len, tt_in, w)
    outs = _mlstm_call(bsz, t, cg, q, k, vt, gcol, grow, init, want_state)
    x1, h2 = _out_proj_call(x2d, ysc, outs[0], outs[1], sot, mod3, mod_row(tt_in), tt_in, w)
    y = _ffn_call(h2, x1, mod3, mod_row(tt_ffn), tt_ffn, ffn_halo, t, w)
    return y.reshape(bsz, t, D_MODEL), outs[2:]


def kernel(x_prompt, x_sample, state_C, state_n, state_m, c, c_ctx, w_mod, b_mod, norm1, w_in, b_gate,
           conv_sc_w, conv_sc_b, mh_norm, w_out, norm2, w_up, conv_ffn_w, conv_ffn_b, w_down, final_norm):
    n_lat = c.shape[0]
    n_ctx = x_prompt.shape[0]
    c8 = jnp.concatenate([c, c_ctx[None], jnp.zeros((8 - n_lat - 1, D_MODEL), F32)], axis=0)
    l = 0
    mod3 = _mod_call(c8, w_mod[l], b_mod[l][None]).reshape(8, 6, D_MODEL)

    wi = w_in[l]
    q0 = 3 * D_CONV
    g0 = q0 + 4 * D_MLSTM
    zpad = jnp.zeros((D_MODEL, LANES - N_CHAINS), F32)
    cols = lambda a: wi[:, g0 + a * N_HEADS:g0 + (a + 1) * N_HEADS]
    w_g = jnp.concatenate([cols(0), cols(2), zpad, cols(1), cols(3), zpad], axis=1).astype(BF16)
    bpad = jnp.zeros((LANES - N_CHAINS,), F32)
    bg = b_gate[l].astype(F32)
    b_g = jnp.concatenate([bg[0], bg[2], bpad, bg[1], bg[3], bpad])[None]
    w = dict(
        norm1=norm1[l][None], norm2=norm2[l][None], final_norm=final_norm[None],
        w_sc=wi[:, 0:q0].astype(BF16),
        w_qk=wi[:, q0:q0 + 2 * D_MLSTM].astype(BF16),
        w_vot=wi[:, q0 + 2 * D_MLSTM:g0].T.astype(BF16),
        w_g=w_g, b_g=b_g,
        conv_sc_w=conv_sc_w[l], conv_sc_b=conv_sc_b[l][None],
        mh_norm_col=mh_norm[l][:, None],
        w_out_a=w_out[l][0:D_CONV].astype(BF16), w_out_b=w_out[l][D_CONV:].astype(BF16),
        w_up=w_up[l].astype(BF16), conv_ffn_w=conv_ffn_w[l], conv_ffn_b=conv_ffn_b[l][None],
        w_down=w_down[l].astype(BF16),
    )

    seq = x_prompt.shape[1]
    y_prompt, (new_c, new_n, new_m) = _trunk(
        x_prompt, mod3, lambda b: n_lat, seq, 0, None, True, 512, 1024, 2, w)

    init = (state_C[:, l], state_n[:, l][:, :, :, None, :],
            jnp.broadcast_to(state_m[:, l][:, :, :, None, None], state_m[:, l].shape + (1, LANES)))
    y_sample, _ = _trunk(x_sample, mod3, lambda b: b, GRID_W, GRID_W, init, False, 512, 1024, 2, w)

    new_n = new_n.reshape(n_ctx, 1, 2, N_HEADS, HEAD_DIM)
    new_m = new_m[..., 0, 0]
    return y_prompt, y_sample, new_c, new_n, new_m
```

```python
import functools

import jax
import jax.numpy as jnp
from jax import lax
from jax.experimental import pallas as pl
from jax.experimental.pallas import tpu as pltpu

F32 = jnp.float32
BF16 = jnp.bfloat16

D_MODEL = 1024
GRID_W = 64
D_CONV = 512
D_MLSTM = 512
N_HEADS = 4
HEAD_DIM = 128
D_FF = 2816
CHUNK = 128
EPS = 1e-6

LANES = 128
SUBLANES = 8
BF16_ROWS = 16
FF_BLK = 256
N_FF_BLK = D_FF // FF_BLK
CONV_ROWS = 128
VMEM_LIMIT = 56 * 1024 * 1024
N_CHAINS = 2 * N_HEADS
E_ROWS = HEAD_DIM + BF16_ROWS


def _cparams(n_axes):
    return pltpu.CompilerParams(
        dimension_semantics=("arbitrary",) * n_axes, vmem_limit_bytes=VMEM_LIMIT)


def _rms_scale(x):
    return x * lax.rsqrt(jnp.mean(x * x, axis=-1, keepdims=True) + EPS)


def _dot(a, b):
    return jnp.dot(a, b, preferred_element_type=F32)


def _dot_nt(a, b):
    return lax.dot_general(a, b, (((1,), (1,)), ((), ())), preferred_element_type=F32)


def _dot_tn(a, b):
    return lax.dot_general(a, b, (((0,), (0,)), ((), ())), preferred_element_type=F32)


def _mod_kernel(c_ref, w_ref, b_ref, o_ref):
    c = c_ref[...]
    s = c * jax.nn.sigmoid(c)
    o_ref[...] = _dot(s.astype(BF16), w_ref[...].astype(BF16)) + b_ref[...]


def _mod_call(c8, w_mod, b_mod):
    n_out = w_mod.shape[1]
    blk = 1024
    return pl.pallas_call(
        _mod_kernel,
        grid=(n_out // blk,),
        in_specs=[
            pl.BlockSpec((8, D_MODEL), lambda i: (0, 0)),
            pl.BlockSpec((D_MODEL, blk), lambda i: (0, i)),
            pl.BlockSpec((1, blk), lambda i: (0, i)),
        ],
        out_specs=pl.BlockSpec((8, blk), lambda i: (0, i)),
        out_shape=jax.ShapeDtypeStruct((8, n_out), F32),
        compiler_params=_cparams(1),
        name="mod",
    )(c8, w_mod, b_mod)


def _chunk_scan(x, pos, op, fill, reverse):
    n = x.shape[0]
    k = 1
    while k < CHUNK:
        if reverse:
            shifted = pltpu.roll(x, n - k, axis=0)
            ok = pos < CHUNK - k
        else:
            shifted = pltpu.roll(x, k, axis=0)
            ok = pos >= k
        x = op(x, jnp.where(ok, shifted, fill))
        k *= 2
    return x


def _in_proj_kernel(row_len, x_ref, mod_ref, n1_ref, wsc_ref, wqk_ref, wvot_ref, wg_ref, bg_ref, cw_ref, cb_ref,
                    ysc_ref, q_ref, k_ref, vt_ref, sot_ref, gcol_ref, grow_ref):
    tt = x_ref.shape[0]
    x = x_ref[...]
    mod = mod_ref[...]
    shift1, scale1 = mod[0:1], mod[1:2]
    h = (_rms_scale(x) * n1_ref[...]) * (1.0 + scale1) + shift1
    hb = h.astype(BF16)

    zb = _dot(hb, wsc_ref[:, 0:D_CONV])
    zc = _dot(hb, wsc_ref[:, D_CONV:2 * D_CONV])
    zx = _dot(hb, wsc_ref[:, 2 * D_CONV:3 * D_CONV])
    u = zc * zx
    pos = lax.broadcasted_iota(jnp.int32, u.shape, 0) % row_len
    prev = jnp.where(pos == 0, 0.0, pltpu.roll(u, 1, axis=0))
    nxt = jnp.where(pos == row_len - 1, 0.0, pltpu.roll(u, tt - 1, axis=0))
    cw = cw_ref[...]
    conv = cw[0:1] * prev + cw[1:2] * u + cw[2:3] * nxt + cb_ref[...]
    ysc_ref[...] = (zb * conv).astype(BF16)

    q_ref[...] = (_dot(hb, wqk_ref[:, 0:D_MLSTM]) * (HEAD_DIM ** -0.5)).astype(BF16)
    k_ref[...] = _dot(hb, wqk_ref[:, D_MLSTM:2 * D_MLSTM]).astype(BF16)
    vt_ref[...] = _dot_nt(wvot_ref[0:D_MLSTM, :], hb).astype(BF16)
    sot_ref[...] = jax.nn.sigmoid(_dot_nt(wvot_ref[D_MLSTM:2 * D_MLSTM, :], hb))

    g = _dot(hb, wg_ref[...]) + bg_ref[...]
    gi = g[:, 0:LANES]
    gf = g[:, LANES:2 * LANES]
    logf = jnp.minimum(gf, 0.0) - jnp.log1p(jnp.exp(-jnp.abs(gf)))
    lane = lax.broadcasted_iota(jnp.int32, gi.shape, 1)
    cpos = lax.broadcasted_iota(jnp.int32, gi.shape, 0) % CHUNK
    fwd = lane < N_HEADS
    b = jnp.where(fwd, _chunk_scan(logf, cpos, jnp.add, 0.0, False),
                  _chunk_scan(logf, cpos, jnp.add, 0.0, True))
    r = gi - b
    cm = jnp.where(fwd, _chunk_scan(r, cpos, jnp.maximum, -jnp.inf, False),
                   _chunk_scan(r, cpos, jnp.maximum, -jnp.inf, True))
    packed = jnp.where(lane < N_CHAINS, cm,
                       jnp.where(lane < 2 * N_CHAINS, pltpu.roll(r, N_CHAINS, axis=1),
                                 pltpu.roll(b, 2 * N_CHAINS, axis=1)))
    gcol_ref[...] = packed
    for c in range(tt // CHUNK):
        grow_ref[c] = packed[c * CHUNK:(c + 1) * CHUNK].T[0:3 * N_CHAINS]


def _in_proj_call(x2d, mod3, mod_row, row_len, tt, w):
    n_tok = x2d.shape[0]
    n_tiles = n_tok // tt
    const = lambda i: (0, 0)
    tok = lambda i: (i, 0)
    tok_t = lambda i: (0, i)
    weights = [w["norm1"], w["w_sc"], w["w_qk"], w["w_vot"], w["w_g"], w["b_g"], w["conv_sc_w"], w["conv_sc_b"]]
    out_shapes = (
        jax.ShapeDtypeStruct((n_tok, D_CONV), BF16),
        jax.ShapeDtypeStruct((n_tok, D_MLSTM), BF16),
        jax.ShapeDtypeStruct((n_tok, D_MLSTM), BF16),
        jax.ShapeDtypeStruct((D_MLSTM, n_tok), BF16),
        jax.ShapeDtypeStruct((D_MLSTM, n_tok), F32),
        jax.ShapeDtypeStruct((n_tok, LANES), F32),
        jax.ShapeDtypeStruct((n_tok // CHUNK, 3 * N_CHAINS, CHUNK), F32),
    )
    return pl.pallas_call(
        functools.partial(_in_proj_kernel, row_len),
        grid=(n_tiles,),
        in_specs=[
            pl.BlockSpec((tt, D_MODEL), tok),
            pl.BlockSpec((None, 6, D_MODEL), lambda i: (mod_row(i), 0, 0)),
        ] + [pl.BlockSpec(a.shape, const) for a in weights],
        out_specs=[
            pl.BlockSpec((tt, D_CONV), tok),
            pl.BlockSpec((tt, D_MLSTM), tok),
            pl.BlockSpec((tt, D_MLSTM), tok),
            pl.BlockSpec((D_MLSTM, tt), tok_t),
            pl.BlockSpec((D_MLSTM, tt), tok_t),
            pl.BlockSpec((tt, LANES), tok),
            pl.BlockSpec((tt // CHUNK, 3 * N_CHAINS, CHUNK), lambda i: (i, 0, 0)),
        ],
        out_shape=out_shapes,
        compiler_params=_cparams(1),
        name="in_proj",
    )(x2d, mod3, *weights)


def _mlstm_kernel(cg, has_init, want_state, *refs):
    refs = list(refs)
    fwd_refs, bwd_refs = refs[0:5], refs[5:10]
    refs = refs[10:]
    if has_init:
        c0_ref, n0_ref, m0_ref = refs[:3]
        refs = refs[3:]
    htf_ref, htb_ref = refs[:2]
    refs = refs[2:]
    if want_state:
        cout_ref, nout_ref, mout_ref = refs[:3]
        refs = refs[3:]
    e_ref, m_ref = refs

    g = pl.program_id(1)

    @pl.when(g == 0)
    def _():
        if has_init:
            for d in range(2):
                for hd in range(N_HEADS):
                    j = d * N_HEADS + hd
                    e_ref[j, 0:HEAD_DIM] = c0_ref[d, hd]
                    e_ref[j, HEAD_DIM:] = jnp.broadcast_to(n0_ref[d, hd], (BF16_ROWS, HEAD_DIM))
                    m_ref[j:j + 1, :] = m0_ref[d, hd]
        else:
            e_ref[...] = jnp.zeros(e_ref.shape, F32)
            m_ref[...] = jnp.zeros(m_ref.shape, F32)

    t_idx = lax.broadcasted_iota(jnp.int32, (CHUNK, CHUNK), 0)
    s_idx = lax.broadcasted_iota(jnp.int32, (CHUNK, CHUNK), 1)
    ones_rows = jnp.ones((BF16_ROWS, CHUNK), BF16)

    def chain_step(c, d, hd):
        q_ref, k_ref, vt_ref, row_ref, col_ref = fwd_refs if d == 0 else bwd_refs
        ht_ref = htf_ref if d == 0 else htb_ref
        j = d * N_HEADS + hd
        toks = slice(c * CHUNK, (c + 1) * CHUNK)
        feat = slice(hd * HEAD_DIM, (hd + 1) * HEAD_DIM)
        last = CHUNK - 1 if d == 0 else 0
        mask = (s_idx <= t_idx) if d == 0 else (s_idx >= t_idx)

        qc = q_ref[toks, feat]
        kc = k_ref[toks, feat]
        vtc = vt_ref[feat, toks]
        cm_row = row_ref[c, j:j + 1, :]
        r_row = row_ref[c, N_CHAINS + j:N_CHAINS + j + 1, :]
        b_row = row_ref[c, 2 * N_CHAINS + j:2 * N_CHAINS + j + 1, :]
        cm_col = col_ref[toks, j:j + 1]
        m_row = m_ref[j:j + 1, :]

        big_m_col = jnp.maximum(m_row[:, j:j + 1], cm_col)
        big_m_row = jnp.maximum(m_row, cm_row)
        m_last = big_m_row[:, last:last + 1]

        w = jnp.exp(jnp.where(mask, r_row - big_m_col, -jnp.inf))
        s = (_dot_nt(qc, kc) * w).astype(BF16)
        num_t = _dot_nt(jnp.concatenate([vtc, ones_rows], axis=0), s)
        e = e_ref[j]
        inter_t = _dot_nt(e.astype(BF16), qc)
        nd = num_t + jnp.exp(m_row - big_m_row) * inter_t
        inv = 1.0 / jnp.maximum(jnp.abs(nd[HEAD_DIM:HEAD_DIM + 1]), jnp.exp(-(b_row + big_m_row)))
        ht_ref[feat, toks] = nd[0:HEAD_DIM] * inv

        wg = jnp.exp(r_row - m_last)
        lhs = jnp.concatenate([(vtc.astype(F32) * wg).astype(BF16),
                               jnp.broadcast_to(wg, (BF16_ROWS, CHUNK)).astype(BF16)], axis=0)
        e_ref[j] = jnp.exp(m_row - m_last) * e + _dot(lhs, kc)
        m_ref[j:j + 1, :] = jnp.broadcast_to(b_row[:, last:last + 1] + m_last, (1, LANES))

    for c in range(cg):
        for d in range(2):
            for hd in range(N_HEADS):
                chain_step(c if d == 0 else cg - 1 - c, d, hd)

    if want_state:
        @pl.when(g == pl.num_programs(1) - 1)
        def _():
            for d in range(2):
                for hd in range(N_HEADS):
                    j = d * N_HEADS + hd
                    cout_ref[d, hd] = e_ref[j, 0:HEAD_DIM]
                    nout_ref[d, hd] = e_ref[j, HEAD_DIM:HEAD_DIM + 1]
                    mout_ref[d, hd] = m_ref[j:j + 1, :]


def _mlstm_call(bsz, t, cg, q, k, vt, gcol, grow, init, want_state):
    n_tok = bsz * t
    tg = cg * CHUNK
    n_groups = t // tg
    fwd = lambda b, g: b * n_groups + g
    bwd = lambda b, g: b * n_groups + (n_groups - 1 - g)

    def stream_specs(pos):
        return [
            pl.BlockSpec((tg, D_MLSTM), lambda b, g: (pos(b, g), 0)),
            pl.BlockSpec((tg, D_MLSTM), lambda b, g: (pos(b, g), 0)),
            pl.BlockSpec((D_MLSTM, tg), lambda b, g: (0, pos(b, g))),
            pl.BlockSpec((cg, 3 * N_CHAINS, CHUNK), lambda b, g: (pos(b, g), 0, 0)),
            pl.BlockSpec((tg, LANES), lambda b, g: (pos(b, g), 0)),
        ]

    in_specs = stream_specs(fwd) + stream_specs(bwd)
    args = [q, k, vt, grow, gcol] * 2
    if init is not None:
        c0, n0, m0 = init
        in_specs += [
            pl.BlockSpec((None, 2, N_HEADS, HEAD_DIM, HEAD_DIM), lambda b, g: (b, 0, 0, 0, 0)),
            pl.BlockSpec((None, 2, N_HEADS, 1, HEAD_DIM), lambda b, g: (b, 0, 0, 0, 0)),
            pl.BlockSpec((None, 2, N_HEADS, 1, LANES), lambda b, g: (b, 0, 0, 0, 0)),
        ]
        args += [c0, n0, m0]
    out_specs = [
        pl.BlockSpec((D_MLSTM, tg), lambda b, g: (0, fwd(b, g))),
        pl.BlockSpec((D_MLSTM, tg), lambda b, g: (0, bwd(b, g))),
    ]
    out_shape = [jax.ShapeDtypeStruct((D_MLSTM, n_tok), F32)] * 2
    if want_state:
        out_specs += [
            pl.BlockSpec((None, None, 2, N_HEADS, HEAD_DIM, HEAD_DIM), lambda b, g: (b, 0, 0, 0, 0, 0)),
            pl.BlockSpec((None, None, 2, N_HEADS, 1, HEAD_DIM), lambda b, g: (b, 0, 0, 0, 0, 0)),
            pl.BlockSpec((None, None, 2, N_HEADS, 1, LANES), lambda b, g: (b, 0, 0, 0, 0, 0)),
        ]
        out_shape += [
            jax.ShapeDtypeStruct((bsz, 1, 2, N_HEADS, HEAD_DIM, HEAD_DIM), F32),
            jax.ShapeDtypeStruct((bsz, 1, 2, N_HEADS, 1, HEAD_DIM), F32),
            jax.ShapeDtypeStruct((bsz, 1, 2, N_HEADS, 1, LANES), F32),
        ]
    return pl.pallas_call(
        functools.partial(_mlstm_kernel, cg, init is not None, want_state),
        grid=(bsz, n_groups),
        in_specs=in_specs,
        out_specs=out_specs,
        out_shape=out_shape,
        scratch_shapes=[
            pltpu.VMEM((N_CHAINS, E_ROWS, HEAD_DIM), F32),
            pltpu.VMEM((N_CHAINS, LANES), F32),
        ],
        compiler_params=_cparams(2),
        name="mlstm",
    )(*args)


def _out_proj_kernel(x_ref, ysc_ref, htf_ref, htb_ref, sot_ref, mh_ref, mod_ref, n2_ref, wa_ref, wb_ref,
                     x1_ref, h2_ref):
    mod = mod_ref[...]
    gate1, shift2, scale2 = mod[2:3], mod[3:4], mod[4:5]
    hs = htf_ref[...] + htb_ref[...]
    heads = []
    for hd in range(N_HEADS):
        blk = hs[hd * HEAD_DIM:(hd + 1) * HEAD_DIM]
        heads.append(blk * lax.rsqrt(jnp.mean(blk * blk, axis=0, keepdims=True) + EPS))
    hm_t = ((jnp.concatenate(heads, axis=0) * mh_ref[...]) * sot_ref[...]).astype(BF16)
    mix = _dot(ysc_ref[...], wa_ref[...]) + _dot_tn(hm_t, wb_ref[...])
    x1 = x_ref[...] + gate1 * mix
    x1_ref[...] = x1
    h2_ref[...] = ((_rms_scale(x1) * n2_ref[...]) * (1.0 + scale2) + shift2).astype(BF16)


def _out_proj_call(x2d, ysc, htf, htb, sot, mod3, mod_row, tt, w):
    n_tok = x2d.shape[0]
    const = lambda i: (0, 0)
    tok = lambda i: (i, 0)
    tok_t = lambda i: (0, i)
    return pl.pallas_call(
        _out_proj_kernel,
        grid=(n_tok // tt,),
        in_specs=[
            pl.BlockSpec((tt, D_MODEL), tok),
            pl.BlockSpec((tt, D_CONV), tok),
            pl.BlockSpec((D_MLSTM, tt), tok_t),
            pl.BlockSpec((D_MLSTM, tt), tok_t),
            pl.BlockSpec((D_MLSTM, tt), tok_t),
            pl.BlockSpec((D_MLSTM, 1), const),
            pl.BlockSpec((None, 6, D_MODEL), lambda i: (mod_row(i), 0, 0)),
            pl.BlockSpec((1, D_MODEL), const),
            pl.BlockSpec(w["w_out_a"].shape, const),
            pl.BlockSpec(w["w_out_b"].shape, const),
        ],
        out_specs=[pl.BlockSpec((tt, D_MODEL), tok), pl.BlockSpec((tt, D_MODEL), tok)],
        out_shape=[jax.ShapeDtypeStruct((n_tok, D_MODEL), F32),
                   jax.ShapeDtypeStruct((n_tok, D_MODEL), BF16)],
        compiler_params=_cparams(1),
        name="out_proj",
    )(x2d, ysc, htf, htb, sot, w["mh_norm_col"], mod3, w["norm2"], w["w_out_a"], w["w_out_b"])


def _ffn_kernel(halo, seq_len, tiles_per_image, *refs):
    if halo:
        h2_ref, prev_ref, next_ref = refs[:3]
        refs = refs[3:]
    else:
        h2_ref = refs[0]
        refs = refs[1:]
    (x1_ref, mod_ref, fn_ref, wa_ref, wg_ref, cw_ref, cb_ref, wd_ref, y_ref,
     act_ref, a0_ref, a1_ref, g0_ref, g1_ref, h2e_ref) = refs
    slots = ((a0_ref, g0_ref), (a1_ref, g1_ref))
    tt = x1_ref.shape[0]
    i = pl.program_id(0)
    j = pl.program_id(1)
    last_blk = N_FF_BLK - 1

    pad = halo if halo else SUBLANES
    shift = halo if halo else 1

    def up_proj(slot):
        a_ref, g_ref = slots[slot]
        hin = h2e_ref[...]
        rows = slice(0, tt + 2 * halo) if halo else slice(pad, pad + tt)
        a_ref[rows] = _dot(hin, wa_ref[...])
        g_ref[rows] = _dot(hin, wg_ref[...])

    def conv_act(blk, slot):
        a_ref, g_ref = slots[slot]
        off_a = pl.multiple_of(blk * FF_BLK, FF_BLK)
        off_g = pl.multiple_of(blk * FF_BLK + D_FF, FF_BLK)
        nr = CONV_ROWS
        row = lax.broadcasted_iota(jnp.int32, (nr, FF_BLK), 0)

        def conv(z_ref, off, r0):
            cw = cw_ref[:, pl.ds(off, FF_BLK)]
            cb = cb_ref[:, pl.ds(off, FF_BLK)]
            prev = z_ref[pad - shift + r0:pad - shift + r0 + nr]
            nxt = z_ref[pad + shift + r0:pad + shift + r0 + nr]
            if not halo:
                if r0 % seq_len == 0:
                    prev = jnp.where(row == 0, 0.0, prev)
                if (r0 + nr) % seq_len == 0:
                    nxt = jnp.where(row == nr - 1, 0.0, nxt)
            return cw[0:1] * prev + cw[1:2] * z_ref[pad + r0:pad + r0 + nr] + cw[2:3] * nxt + cb

        for r0 in range(0, tt, nr):
            ac = conv(a_ref, off_a, r0)
            gc = conv(g_ref, off_g, r0)
            act_ref[r0:r0 + nr, pl.ds(off_a, FF_BLK)] = (gc * jax.nn.sigmoid(gc) * ac).astype(BF16)

    @pl.when(j == 0)
    def _():
        h2e_ref[halo:halo + tt] = h2_ref[...]
        if halo:
            h2e_ref[0:halo] = prev_ref[...]
            h2e_ref[halo + tt:] = next_ref[...]

            @pl.when(i % tiles_per_image == 0)
            def _():
                h2e_ref[0:halo] = jnp.zeros((halo, D_MODEL), BF16)

            @pl.when(i % tiles_per_image == tiles_per_image - 1)
            def _():
                h2e_ref[halo + tt:] = jnp.zeros((halo, D_MODEL), BF16)
        else:
            for z_ref in (a0_ref, a1_ref, g0_ref, g1_ref):
                z_ref[0:pad] = jnp.zeros((pad, FF_BLK), F32)
                z_ref[pad + tt:] = jnp.zeros((pad, FF_BLK), F32)
        up_proj(0)

    @pl.when(j % 2 == 1)
    def _():
        conv_act(j - 1, 0)
        up_proj(1)

    @pl.when(jnp.logical_and(j % 2 == 0, j > 0))
    def _():
        conv_act(j - 1, 1)
        up_proj(0)

    @pl.when(j == last_blk)
    def _():
        k0 = last_blk * FF_BLK
        part = _dot(act_ref[:, 0:k0], wd_ref[0:k0, :])
        conv_act(last_blk, last_blk % 2)
        ffn = part + _dot(act_ref[:, k0:], wd_ref[k0:, :])
        x2 = x1_ref[...] + mod_ref[5:6] * ffn
        y_ref[...] = _rms_scale(x2) * fn_ref[...]


def _ffn_call(h2, x1, mod3, mod_row, tt, halo, seq_len, w):
    n_tok = x1.shape[0]
    n_tiles = n_tok // tt
    tok = lambda i, j: (i, 0)
    const = lambda i, j: (0, 0)
    in_specs = [pl.BlockSpec((tt, D_MODEL), tok)]
    args = [h2]
    up_rows = tt + 2 * (halo if halo else SUBLANES)
    scratch = [pltpu.VMEM((tt, D_FF), BF16)] + [pltpu.VMEM((up_rows, FF_BLK), F32)] * 4
    tiles_per_image = 1
    if halo:
        tiles_per_image = seq_len // tt
        per = tt // halo
        n_halo_blk = n_tok // halo
        in_specs += [
            pl.BlockSpec((halo, D_MODEL), lambda i, j: (jnp.maximum(i * per - 1, 0), 0)),
            pl.BlockSpec((halo, D_MODEL), lambda i, j: (jnp.minimum((i + 1) * per, n_halo_blk - 1), 0)),
        ]
        args += [h2, h2]
    scratch.append(pltpu.VMEM((tt + 2 * halo, D_MODEL), BF16))
    in_specs += [
        pl.BlockSpec((tt, D_MODEL), tok),
        pl.BlockSpec((None, 6, D_MODEL), lambda i, j: (mod_row(i), 0, 0)),
        pl.BlockSpec((1, D_MODEL), const),
        pl.BlockSpec((D_MODEL, FF_BLK), lambda i, j: (0, j)),
        pl.BlockSpec((D_MODEL, FF_BLK), lambda i, j: (0, N_FF_BLK + j)),
        pl.BlockSpec((3, 2 * D_FF), const),
        pl.BlockSpec((1, 2 * D_FF), const),
        pl.BlockSpec((D_FF, D_MODEL), const),
    ]
    args += [x1, mod3, w["final_norm"], w["w_up"], w["w_up"], w["conv_ffn_w"], w["conv_ffn_b"], w["w_down"]]
    return pl.pallas_call(
        functools.partial(_ffn_kernel, halo, seq_len, tiles_per_image),
        grid=(n_tiles, N_FF_BLK),
        in_specs=in_specs,
        out_specs=pl.BlockSpec((tt, D_MODEL), tok),
        out_shape=jax.ShapeDtypeStruct((n_tok, D_MODEL), F32),
        scratch_shapes=scratch,
        compiler_params=_cparams(2),
        name="ffn",
    )(*args)


def _trunk(x, mod3, mod_row_of_batch, row_len, ffn_halo, init, want_state, tt_in, tt_ffn, cg, w):
    bsz, t, _ = x.shape
    x2d = x.reshape(bsz * t, D_MODEL)

    def mod_row(tile_tokens):
        return lambda i: mod_row_of_batch((i * tile_tokens) // t)

    ysc, q, k, vt, sot, gcol, grow = _in_proj_call(x2d, mod3, mod_row(tt_in), row_len, tt_in, w)
    outs = _mlstm_call(bsz, t, cg, q, k, vt, gcol, grow, init, want_state)
    x1, h2 = _out_proj_call(x2d, ysc, outs[0], outs[1], sot, mod3, mod_row(tt_in), tt_in, w)
    y = _ffn_call(h2, x1, mod3, mod_row(tt_ffn), tt_ffn, ffn_halo, t, w)
    return y.reshape(bsz, t, D_MODEL), outs[2:]


def kernel(x_prompt, x_sample, state_C, state_n, state_m, c, c_ctx, w_mod, b_mod, norm1, w_in, b_gate,
           conv_sc_w, conv_sc_b, mh_norm, w_out, norm2, w_up, conv_ffn_w, conv_ffn_b, w_down, final_norm):
    n_lat = c.shape[0]
    n_ctx = x_prompt.shape[0]
    c8 = jnp.concatenate([c, c_ctx[None], jnp.zeros((8 - n_lat - 1, D_MODEL), F32)], axis=0)
    l = 0
    mod3 = _mod_call(c8, w_mod[l], b_mod[l][None]).reshape(8, 6, D_MODEL)

    wi = w_in[l]
    q0 = 3 * D_CONV
    g0 = q0 + 4 * D_MLSTM
    zpad = jnp.zeros((D_MODEL, LANES - N_CHAINS), F32)
    cols = lambda a: wi[:, g0 + a * N_HEADS:g0 + (a + 1) * N_HEADS]
    w_g = jnp.concatenate([cols(0), cols(2), zpad, cols(1), cols(3), zpad], axis=1).astype(BF16)
    bpad = jnp.zeros((LANES - N_CHAINS,), F32)
    bg = b_gate[l].astype(F32)
    b_g = jnp.concatenate([bg[0], bg[2], bpad, bg[1], bg[3], bpad])[None]
    w = dict(
        norm1=norm1[l][None], norm2=norm2[l][None], final_norm=final_norm[None],
        w_sc=wi[:, 0:q0].astype(BF16),
        w_qk=wi[:, q0:q0 + 2 * D_MLSTM].astype(BF16),
        w_vot=wi[:, q0 + 2 * D_MLSTM:g0].T.astype(BF16),
        w_g=w_g, b_g=b_g,
        conv_sc_w=conv_sc_w[l], conv_sc_b=conv_sc_b[l][None],
        mh_norm_col=mh_norm[l][:, None],
        w_out_a=w_out[l][0:D_CONV].astype(BF16), w_out_b=w_out[l][D_CONV:].astype(BF16),
        w_up=w_up[l].astype(BF16), conv_ffn_w=conv_ffn_w[l], conv_ffn_b=conv_ffn_b[l][None],
        w_down=w_down[l].astype(BF16),
    )

    seq = x_prompt.shape[1]
    y_prompt, (new_c, new_n, new_m) = _trunk(
        x_prompt, mod3, lambda b: n_lat, seq, 0, None, True, 512, 1024, 2, w)

    init = (state_C[:, l], state_n[:, l][:, :, :, None, :],
            jnp.broadcast_to(state_m[:, l][:, :, :, None, None], state_m[:, l].shape + (1, LANES)))
    y_sample, _ = _trunk(x_sample, mod3, lambda b: b, GRID_W, GRID_W, init, False, 512, 1024, 2, w)

    new_n = new_n.reshape(n_ctx, 1, 2, N_HEADS, HEAD_DIM)
    new_m = new_m[..., 0, 0]
    return y_prompt, y_sample, new_c, new_n, new_m
```

```python
import functools

import jax
import jax.numpy as jnp
from jax import lax
from jax.experimental import pallas as pl
from jax.experimental.pallas import tpu as pltpu

F32 = jnp.float32
BF16 = jnp.bfloat16

D_MODEL = 1024
GRID_W = 64
D_CONV = 512
D_MLSTM = 512
N_HEADS = 4
HEAD_DIM = 128
D_FF = 2816
CHUNK = 128
EPS = 1e-6

LANES = 128
SUBLANES = 8
BF16_ROWS = 16
FF_BLK = 256
N_FF_BLK = D_FF // FF_BLK
CONV_ROWS = 128
VMEM_LIMIT = 56 * 1024 * 1024
N_CHAINS = 2 * N_HEADS
E_ROWS = HEAD_DIM + BF16_ROWS


def _cparams(n_axes):
    return pltpu.CompilerParams(
        dimension_semantics=("arbitrary",) * n_axes, vmem_limit_bytes=VMEM_LIMIT)


def _rms_scale(x):
    return x * lax.rsqrt(jnp.mean(x * x, axis=-1, keepdims=True) + EPS)


def _dot(a, b):
    return jnp.dot(a, b, preferred_element_type=F32)


def _dot_nt(a, b):
    return lax.dot_general(a, b, (((1,), (1,)), ((), ())), preferred_element_type=F32)


def _dot_tn(a, b):
    return lax.dot_general(a, b, (((0,), (0,)), ((), ())), preferred_element_type=F32)


def _mod_kernel(c_ref, w_ref, b_ref, o_ref):
    c = c_ref[...]
    s = c * jax.nn.sigmoid(c)
    o_ref[...] = _dot(s.astype(BF16), w_ref[...].astype(BF16)) + b_ref[...]


def _mod_call(c8, w_mod, b_mod):
    n_out = w_mod.shape[1]
    blk = 1024
    return pl.pallas_call(
        _mod_kernel,
        grid=(n_out // blk,),
        in_specs=[
            pl.BlockSpec((8, D_MODEL), lambda i: (0, 0)),
            pl.BlockSpec((D_MODEL, blk), lambda i: (0, i)),
            pl.BlockSpec((1, blk), lambda i: (0, i)),
        ],
        out_specs=pl.BlockSpec((8, blk), lambda i: (0, i)),
        out_shape=jax.ShapeDtypeStruct((8, n_out), F32),
        compiler_params=_cparams(1),
        name="mod",
    )(c8, w_mod, b_mod)


def _chunk_scan(x, pos, op, fill, reverse):
    n = x.shape[0]
    k = 1
    while k < CHUNK:
        if reverse:
            shifted = pltpu.roll(x, n - k, axis=0)
            ok = pos < CHUNK - k
        else:
            shifted = pltpu.roll(x, k, axis=0)
            ok = pos >= k
        x = op(x, jnp.where(ok, shifted, fill))
        k *= 2
    return x


def _in_proj_kernel(row_len, x_ref, mod_ref, n1_ref, wsc_ref, wqk_ref, wvot_ref, wg_ref, bg_ref, cw_ref, cb_ref,
                    ysc_ref, q_ref, k_ref, vt_ref, sot_ref, gcol_ref, grow_ref):
    tt = x_ref.shape[0]
    x = x_ref[...]
    mod = mod_ref[...]
    shift1, scale1 = mod[0:1], mod[1:2]
    h = (_rms_scale(x) * n1_ref[...]) * (1.0 + scale1) + shift1
    hb = h.astype(BF16)

    zb = _dot(hb, wsc_ref[:, 0:D_CONV])
    zc = _dot(hb, wsc_ref[:, D_CONV:2 * D_CONV])
    zx = _dot(hb, wsc_ref[:, 2 * D_CONV:3 * D_CONV])
    u = zc * zx
    pos = lax.broadcasted_iota(jnp.int32, u.shape, 0) % row_len
    prev = jnp.where(pos == 0, 0.0, pltpu.roll(u, 1, axis=0))
    nxt = jnp.where(pos == row_len - 1, 0.0, pltpu.roll(u, tt - 1, axis=0))
    cw = cw_ref[...]
    conv = cw[0:1] * prev + cw[1:2] * u + cw[2:3] * nxt + cb_ref[...]
    ysc_ref[...] = (zb * conv).astype(BF16)

    q_ref[...] = (_dot(hb, wqk_ref[:, 0:D_MLSTM]) * (HEAD_DIM ** -0.5)).astype(BF16)
    k_ref[...] = _dot(hb, wqk_ref[:, D_MLSTM:2 * D_MLSTM]).astype(BF16)
    vt_ref[...] = _dot_nt(wvot_ref[0:D_MLSTM, :], hb).astype(BF16)
    sot_ref[...] = jax.nn.sigmoid(_dot_nt(wvot_ref[D_MLSTM:2 * D_MLSTM, :], hb))

    g = _dot(hb, wg_ref[...]) + bg_ref[...]
    gi = g[:, 0:LANES]
    gf = g[:, LANES:2 * LANES]
    logf = jnp.minimum(gf, 0.0) - jnp.log1p(jnp.exp(-jnp.abs(gf)))
    lane = lax.broadcasted_iota(jnp.int32, gi.shape, 1)
    cpos = lax.broadcasted_iota(jnp.int32, gi.shape, 0) % CHUNK
    fwd = lane < N_HEADS
    b = jnp.where(fwd, _chunk_scan(logf, cpos, jnp.add, 0.0, False),
                  _chunk_scan(logf, cpos, jnp.add, 0.0, True))
    r = gi - b
    cm = jnp.where(fwd, _chunk_scan(r, cpos, jnp.maximum, -jnp.inf, False),
                   _chunk_scan(r, cpos, jnp.maximum, -jnp.inf, True))
    packed = jnp.where(lane < N_CHAINS, cm,
                       jnp.where(lane < 2 * N_CHAINS, pltpu.roll(r, N_CHAINS, axis=1),
                                 pltpu.roll(b, 2 * N_CHAINS, axis=1)))
    gcol_ref[...] = packed
    for c in range(tt // CHUNK):
        grow_ref[c] = packed[c * CHUNK:(c + 1) * CHUNK].T[0:3 * N_CHAINS]


def _in_proj_call(x2d, mod3, mod_row, row_len, tt, w):
    n_tok = x2d.shape[0]
    n_tiles = n_tok // tt
    const = lambda i: (0, 0)
    tok = lambda i: (i, 0)
    tok_t = lambda i: (0, i)
    weights = [w["norm1"], w["w_sc"], w["w_qk"], w["w_vot"], w["w_g"], w["b_g"], w["conv_sc_w"], w["conv_sc_b"]]
    out_shapes = (
        jax.ShapeDtypeStruct((n_tok, D_CONV), BF16),
        jax.ShapeDtypeStruct((n_tok, D_MLSTM), BF16),
        jax.ShapeDtypeStruct((n_tok, D_MLSTM), BF16),
        jax.ShapeDtypeStruct((D_MLSTM, n_tok), BF16),
        jax.ShapeDtypeStruct((D_MLSTM, n_tok), F32),
        jax.ShapeDtypeStruct((n_tok, LANES), F32),
        jax.ShapeDtypeStruct((n_tok // CHUNK, 3 * N_CHAINS, CHUNK), F32),
    )
    return pl.pallas_call(
        functools.partial(_in_proj_kernel, row_len),
        grid=(n_tiles,),
        in_specs=[
            pl.BlockSpec((tt, D_MODEL), tok),
            pl.BlockSpec((None, 6, D_MODEL), lambda i: (mod_row(i), 0, 0)),
        ] + [pl.BlockSpec(a.shape, const) for a in weights],
        out_specs=[
            pl.BlockSpec((tt, D_CONV), tok),
            pl.BlockSpec((tt, D_MLSTM), tok),
            pl.BlockSpec((tt, D_MLSTM), tok),
            pl.BlockSpec((D_MLSTM, tt), tok_t),
            pl.BlockSpec((D_MLSTM, tt), tok_t),
            pl.BlockSpec((tt, LANES), tok),
            pl.BlockSpec((tt // CHUNK, 3 * N_CHAINS, CHUNK), lambda i: (i, 0, 0)),
        ],
        out_shape=out_shapes,
        compiler_params=_cparams(1),
        name="in_proj",
    )(x2d, mod3, *weights)


def _mlstm_kernel(cg, has_init, want_state, *refs):
    refs = list(refs)
    fwd_refs, bwd_refs = refs[0:5], refs[5:10]
    refs = refs[10:]
    if has_init:
        c0_ref, n0_ref, m0_ref = refs[:3]
        refs = refs[3:]
    htf_ref, htb_ref = refs[:2]
    refs = refs[2:]
    if want_state:
        cout_ref, nout_ref, mout_ref = refs[:3]
        refs = refs[3:]
    e_ref, m_ref = refs

    g = pl.program_id(1)

    @pl.when(g == 0)
    def _():
        if has_init:
            for d in range(2):
                for hd in range(N_HEADS):
                    j = d * N_HEADS + hd
                    e_ref[j, 0:HEAD_DIM] = c0_ref[d, hd]
                    e_ref[j, HEAD_DIM:] = jnp.broadcast_to(n0_ref[d, hd], (BF16_ROWS, HEAD_DIM))
                    m_ref[j:j + 1, :] = m0_ref[d, hd]
        else:
            e_ref[...] = jnp.zeros(e_ref.shape, F32)
            m_ref[...] = jnp.zeros(m_ref.shape, F32)

    t_idx = lax.broadcasted_iota(jnp.int32, (CHUNK, CHUNK), 0)
    s_idx = lax.broadcasted_iota(jnp.int32, (CHUNK, CHUNK), 1)
    ones_rows = jnp.ones((BF16_ROWS, CHUNK), BF16)

    def chain_step(c, d, hd):
        q_ref, k_ref, vt_ref, row_ref, col_ref = fwd_refs if d == 0 else bwd_refs
        ht_ref = htf_ref if d == 0 else htb_ref
        j = d * N_HEADS + hd
        toks = slice(c * CHUNK, (c + 1) * CHUNK)
        feat = slice(hd * HEAD_DIM, (hd + 1) * HEAD_DIM)
        last = CHUNK - 1 if d == 0 else 0
        mask = (s_idx <= t_idx) if d == 0 else (s_idx >= t_idx)

        qc = q_ref[toks, feat]
        kc = k_ref[toks, feat]
        vtc = vt_ref[feat, toks]
        cm_row = row_ref[c, j:j + 1, :]
        r_row = row_ref[c, N_CHAINS + j:N_CHAINS + j + 1, :]
        b_row = row_ref[c, 2 * N_CHAINS + j:2 * N_CHAINS + j + 1, :]
        cm_col = col_ref[toks, j:j + 1]
        m_row = m_ref[j:j + 1, :]

        big_m_col = jnp.maximum(m_row[:, j:j + 1], cm_col)
        big_m_row = jnp.maximum(m_row, cm_row)
        m_last = big_m_row[:, last:last + 1]

        w = jnp.exp(jnp.where(mask, r_row - big_m_col, -jnp.inf))
        s = (_dot_nt(qc, kc) * w).astype(BF16)
        num_t = _dot_nt(jnp.concatenate([vtc, ones_rows], axis=0), s)
        e = e_ref[j]
        inter_t = _dot_nt(e.astype(BF16), qc)
        nd = num_t + jnp.exp(m_row - big_m_row) * inter_t
        inv = 1.0 / jnp.maximum(jnp.abs(nd[HEAD_DIM:HEAD_DIM + 1]), jnp.exp(-(b_row + big_m_row)))
        ht_ref[feat, toks] = nd[0:HEAD_DIM] * inv

        wg = jnp.exp(r_row - m_last)
        lhs = jnp.concatenate([(vtc.astype(F32) * wg).astype(BF16),
                               jnp.broadcast_to(wg, (BF16_ROWS, CHUNK)).astype(BF16)], axis=0)
        e_ref[j] = jnp.exp(m_row - m_last) * e + _dot(lhs, kc)
        m_ref[j:j + 1, :] = jnp.broadcast_to(b_row[:, last:last + 1] + m_last, (1, LANES))

    for c in range(cg):
        for d in range(2):
            for hd in range(N_HEADS):
                chain_step(c if d == 0 else cg - 1 - c, d, hd)

    if want_state:
        @pl.when(g == pl.num_programs(1) - 1)
        def _():
            for d in range(2):
                for hd in range(N_HEADS):
                    j = d * N_HEADS + hd
                    cout_ref[d, hd] = e_ref[j, 0:HEAD_DIM]
                    nout_ref[d, hd] = e_ref[j, HEAD_DIM:HEAD_DIM + 1]
                    mout_ref[d, hd] = m_ref[j:j + 1, :]


def _mlstm_call(bsz, t, cg, q, k, vt, gcol, grow, init, want_state):
    n_tok = bsz * t
    tg = cg * CHUNK
    n_groups = t // tg
    fwd = lambda b, g: b * n_groups + g
    bwd = lambda b, g: b * n_groups + (n_groups - 1 - g)

    def stream_specs(pos):
        return [
            pl.BlockSpec((tg, D_MLSTM), lambda b, g: (pos(b, g), 0)),
            pl.BlockSpec((tg, D_MLSTM), lambda b, g: (pos(b, g), 0)),
            pl.BlockSpec((D_MLSTM, tg), lambda b, g: (0, pos(b, g))),
            pl.BlockSpec((cg, 3 * N_CHAINS, CHUNK), lambda b, g: (pos(b, g), 0, 0)),
            pl.BlockSpec((tg, LANES), lambda b, g: (pos(b, g), 0)),
        ]

    in_specs = stream_specs(fwd) + stream_specs(bwd)
    args = [q, k, vt, grow, gcol] * 2
    if init is not None:
        c0, n0, m0 = init
        in_specs += [
            pl.BlockSpec((None, 2, N_HEADS, HEAD_DIM, HEAD_DIM), lambda b, g: (b, 0, 0, 0, 0)),
            pl.BlockSpec((None, 2, N_HEADS, 1, HEAD_DIM), lambda b, g: (b, 0, 0, 0, 0)),
            pl.BlockSpec((None, 2, N_HEADS, 1, LANES), lambda b, g: (b, 0, 0, 0, 0)),
        ]
        args += [c0, n0, m0]
    out_specs = [
        pl.BlockSpec((D_MLSTM, tg), lambda b, g: (0, fwd(b, g))),
        pl.BlockSpec((D_MLSTM, tg), lambda b, g: (0, bwd(b, g))),
    ]
    out_shape = [jax.ShapeDtypeStruct((D_MLSTM, n_tok), F32)] * 2
    if want_state:
        out_specs += [
            pl.BlockSpec((None, None, 2, N_HEADS, HEAD_DIM, HEAD_DIM), lambda b, g: (b, 0, 0, 0, 0, 0)),
            pl.BlockSpec((None, None, 2, N_HEADS, 1, HEAD_DIM), lambda b, g: (b, 0, 0, 0, 0, 0)),
            pl.BlockSpec((None, None, 2, N_HEADS, 1, LANES), lambda b, g: (b, 0, 0, 0, 0, 0)),
        ]
        out_shape += [
            jax.ShapeDtypeStruct((bsz, 1, 2, N_HEADS, HEAD_DIM, HEAD_DIM), F32),
            jax.ShapeDtypeStruct((bsz, 1, 2, N_HEADS, 1, HEAD_DIM), F32),
            jax.ShapeDtypeStruct((bsz, 1, 2, N_HEADS, 1, LANES), F32),
        ]
    return pl.pallas_call(
        functools.partial(_mlstm_kernel, cg, init is not None, want_state),
        grid=(bsz, n_groups),
        in_specs=in_specs,
        out_specs=out_specs,
        out_shape=out_shape,
        scratch_shapes=[
            pltpu.VMEM((N_CHAINS, E_ROWS, HEAD_DIM), F32),
            pltpu.VMEM((N_CHAINS, LANES), F32),
        ],
        compiler_params=_cparams(2),
        name="mlstm",
    )(*args)


def _out_proj_kernel(x_ref, ysc_ref, htf_ref, htb_ref, sot_ref, mh_ref, mod_ref, n2_ref, wa_ref, wb_ref,
                     x1_ref, h2_ref):
    mod = mod_ref[...]
    gate1, shift2, scale2 = mod[2:3], mod[3:4], mod[4:5]
    hs = htf_ref[...] + htb_ref[...]
    heads = []
    for hd in range(N_HEADS):
        blk = hs[hd * HEAD_DIM:(hd + 1) * HEAD_DIM]
        heads.append(blk * lax.rsqrt(jnp.mean(blk * blk, axis=0, keepdims=True) + EPS))
    hm_t = ((jnp.concatenate(heads, axis=0) * mh_ref[...]) * sot_ref[...]).astype(BF16)
    mix = _dot(ysc_ref[...], wa_ref[...]) + _dot_tn(hm_t, wb_ref[...])
    x1 = x_ref[...] + gate1 * mix
    x1_ref[...] = x1
    h2_ref[...] = ((_rms_scale(x1) * n2_ref[...]) * (1.0 + scale2) + shift2).astype(BF16)


def _out_proj_call(x2d, ysc, htf, htb, sot, mod3, mod_row, tt, w):
    n_tok = x2d.shape[0]
    const = lambda i: (0, 0)
    tok = lambda i: (i, 0)
    tok_t = lambda i: (0, i)
    return pl.pallas_call(
        _out_proj_kernel,
        grid=(n_tok // tt,),
        in_specs=[
            pl.BlockSpec((tt, D_MODEL), tok),
            pl.BlockSpec((tt, D_CONV), tok),
            pl.BlockSpec((D_MLSTM, tt), tok_t),
            pl.BlockSpec((D_MLSTM, tt), tok_t),
            pl.BlockSpec((D_MLSTM, tt), tok_t),
            pl.BlockSpec((D_MLSTM, 1), const),
            pl.BlockSpec((None, 6, D_MODEL), lambda i: (mod_row(i), 0, 0)),
            pl.BlockSpec((1, D_MODEL), const),
            pl.BlockSpec(w["w_out_a"].shape, const),
            pl.BlockSpec(w["w_out_b"].shape, const),
        ],
        out_specs=[pl.BlockSpec((tt, D_MODEL), tok), pl.BlockSpec((tt, D_MODEL), tok)],
        out_shape=[jax.ShapeDtypeStruct((n_tok, D_MODEL), F32),
                   jax.ShapeDtypeStruct((n_tok, D_MODEL), BF16)],
        compiler_params=_cparams(1),
        name="out_proj",
    )(x2d, ysc, htf, htb, sot, w["mh_norm_col"], mod3, w["norm2"], w["w_out_a"], w["w_out_b"])


def _ffn_kernel(halo, seq_len, tiles_per_image, *refs):
    if halo:
        h2_ref, prev_ref, next_ref = refs[:3]
        refs = refs[3:]
    else:
        h2_ref = refs[0]
        refs = refs[1:]
    (x1_ref, mod_ref, fn_ref, wa_ref, wg_ref, cw_ref, cb_ref, wd_ref, y_ref,
     act_ref, a0_ref, a1_ref, g0_ref, g1_ref, h2e_ref) = refs
    slots = ((a0_ref, g0_ref), (a1_ref, g1_ref))
    tt = x1_ref.shape[0]
    i = pl.program_id(0)
    j = pl.program_id(1)
    last_blk = N_FF_BLK - 1

    pad = halo if halo else SUBLANES
    shift = halo if halo else 1

    def up_proj(slot):
        a_ref, g_ref = slots[slot]
        hin = h2e_ref[...]
        rows = slice(0, tt + 2 * halo) if halo else slice(pad, pad + tt)
        a_ref[rows] = _dot(hin, wa_ref[...])
        g_ref[rows] = _dot(hin, wg_ref[...])

    def conv_act(blk, slot):
        a_ref, g_ref = slots[slot]
        off_a = pl.multiple_of(blk * FF_BLK, FF_BLK)
        off_g = pl.multiple_of(blk * FF_BLK + D_FF, FF_BLK)
        nr = CONV_ROWS
        row = lax.broadcasted_iota(jnp.int32, (nr, FF_BLK), 0)

        def conv(z_ref, off, r0):
            cw = cw_ref[:, pl.ds(off, FF_BLK)]
            cb = cb_ref[:, pl.ds(off, FF_BLK)]
            prev = z_ref[pad - shift + r0:pad - shift + r0 + nr]
            nxt = z_ref[pad + shift + r0:pad + shift + r0 + nr]
            if not halo:
                if r0 % seq_len == 0:
                    prev = jnp.where(row == 0, 0.0, prev)
                if (r0 + nr) % seq_len == 0:
                    nxt = jnp.where(row == nr - 1, 0.0, nxt)
            return cw[0:1] * prev + cw[1:2] * z_ref[pad + r0:pad + r0 + nr] + cw[2:3] * nxt + cb

        for r0 in range(0, tt, nr):
            ac = conv(a_ref, off_a, r0)
            gc = conv(g_ref, off_g, r0)
            act_ref[r0:r0 + nr, pl.ds(off_a, FF_BLK)] = (gc * jax.nn.sigmoid(gc) * ac).astype(BF16)

    @pl.when(j == 0)
    def _():
        h2e_ref[halo:halo + tt] = h2_ref[...]
        if halo:
            h2e_ref[0:halo] = prev_ref[...]
            h2e_ref[halo + tt:] = next_ref[...]

            @pl.when(i % tiles_per_image == 0)
            def _():
                h2e_ref[0:halo] = jnp.zeros((halo, D_MODEL), BF16)

            @pl.when(i % tiles_per_image == tiles_per_image - 1)
            def _():
                h2e_ref[halo + tt:] = jnp.zeros((halo, D_MODEL), BF16)
        else:
            for z_ref in (a0_ref, a1_ref, g0_ref, g1_ref):
                z_ref[0:pad] = jnp.zeros((pad, FF_BLK), F32)
                z_ref[pad + tt:] = jnp.zeros((pad, FF_BLK), F32)
        up_proj(0)

    @pl.when(j % 2 == 1)
    def _():
        conv_act(j - 1, 0)
        up_proj(1)

    @pl.when(jnp.logical_and(j % 2 == 0, j > 0))
    def _():
        conv_act(j - 1, 1)
        up_proj(0)

    @pl.when(j == last_blk)
    def _():
        k0 = last_blk * FF_BLK
        part = _dot(act_ref[:, 0:k0], wd_ref[0:k0, :])
        conv_act(last_blk, last_blk % 2)
        ffn = part + _dot(act_ref[:, k0:], wd_ref[k0:, :])
        x2 = x1_ref[...] + mod_ref[5:6] * ffn
        y_ref[...] = _rms_scale(x2) * fn_ref[...]


def _ffn_call(h2, x1, mod3, mod_row, tt, halo, seq_len, w):
    n_tok = x1.shape[0]
    n_tiles = n_tok // tt
    tok = lambda i, j: (i, 0)
    const = lambda i, j: (0, 0)
    in_specs = [pl.BlockSpec((tt, D_MODEL), tok)]
    args = [h2]
    up_rows = tt + 2 * (halo if halo else SUBLANES)
    scratch = [pltpu.VMEM((tt, D_FF), BF16)] + [pltpu.VMEM((up_rows, FF_BLK), F32)] * 4
    tiles_per_image = 1
    if halo:
        tiles_per_image = seq_len // tt
        per = tt // halo
        n_halo_blk = n_tok // halo
        in_specs += [
            pl.BlockSpec((halo, D_MODEL), lambda i, j: (jnp.maximum(i * per - 1, 0), 0)),
            pl.BlockSpec((halo, D_MODEL), lambda i, j: (jnp.minimum((i + 1) * per, n_halo_blk - 1), 0)),
        ]
        args += [h2, h2]
    scratch.append(pltpu.VMEM((tt + 2 * halo, D_MODEL), BF16))
    in_specs += [
        pl.BlockSpec((tt, D_MODEL), tok),
        pl.BlockSpec((None, 6, D_MODEL), lambda i, j: (mod_row(i), 0, 0)),
        pl.BlockSpec((1, D_MODEL), const),
        pl.BlockSpec((None, D_MODEL, FF_BLK), lambda i, j: (j, 0, 0)),
        pl.BlockSpec((None, D_MODEL, FF_BLK), lambda i, j: (N_FF_BLK + j, 0, 0)),
        pl.BlockSpec((3, 2 * D_FF), const),
        pl.BlockSpec((1, 2 * D_FF), const),
        pl.BlockSpec((D_FF, D_MODEL), const),
    ]
    args += [x1, mod3, w["final_norm"], w["w_up"], w["w_up"], w["conv_ffn_w"], w["conv_ffn_b"], w["w_down"]]
    return pl.pallas_call(
        functools.partial(_ffn_kernel, halo, seq_len, tiles_per_image),
        grid=(n_tiles, N_FF_BLK),
        in_specs=in_specs,
        out_specs=pl.BlockSpec((tt, D_MODEL), tok),
        out_shape=jax.ShapeDtypeStruct((n_tok, D_MODEL), F32),
        scratch_shapes=scratch,
        compiler_params=_cparams(2),
        name="ffn",
    )(*args)


def _trunk(x, mod3, mod_row_of_batch, row_len, ffn_halo, init, want_state, tt_in, tt_ffn, cg, w):
    bsz, t, _ = x.shape
    x2d = x.reshape(bsz * t, D_MODEL)

    def mod_row(tile_tokens):
        return lambda i: mod_row_of_batch((i * tile_tokens) // t)

    ysc, q, k, vt, sot, gcol, grow = _in_proj_call(x2d, mod3, mod_row(tt_in), row_len, tt_in, w)
    outs = _mlstm_call(bsz, t, cg, q, k, vt, gcol, grow, init, want_state)
    x1, h2 = _out_proj_call(x2d, ysc, outs[0], outs[1], sot, mod3, mod_row(tt_in), tt_in, w)
    y = _ffn_call(h2, x1, mod3, mod_row(tt_ffn), tt_ffn, ffn_halo, t, w)
    return y.reshape(bsz, t, D_MODEL), outs[2:]


def kernel(x_prompt, x_sample, state_C, state_n, state_m, c, c_ctx, w_mod, b_mod, norm1, w_in, b_gate,
           conv_sc_w, conv_sc_b, mh_norm, w_out, norm2, w_up, conv_ffn_w, conv_ffn_b, w_down, final_norm):
    n_lat = c.shape[0]
    n_ctx = x_prompt.shape[0]
    c8 = jnp.concatenate([c, c_ctx[None], jnp.zeros((8 - n_lat - 1, D_MODEL), F32)], axis=0)
    l = 0
    mod3 = _mod_call(c8, w_mod[l], b_mod[l][None]).reshape(8, 6, D_MODEL)

    wi = w_in[l]
    q0 = 3 * D_CONV
    g0 = q0 + 4 * D_MLSTM
    zpad = jnp.zeros((D_MODEL, LANES - N_CHAINS), F32)
    cols = lambda a: wi[:, g0 + a * N_HEADS:g0 + (a + 1) * N_HEADS]
    w_g = jnp.concatenate([cols(0), cols(2), zpad, cols(1), cols(3), zpad], axis=1).astype(BF16)
    bpad = jnp.zeros((LANES - N_CHAINS,), F32)
    bg = b_gate[l].astype(F32)
    b_g = jnp.concatenate([bg[0], bg[2], bpad, bg[1], bg[3], bpad])[None]
    w = dict(
        norm1=norm1[l][None], norm2=norm2[l][None], final_norm=final_norm[None],
        w_sc=wi[:, 0:q0].astype(BF16),
        w_qk=wi[:, q0:q0 + 2 * D_MLSTM].astype(BF16),
        w_vot=wi[:, q0 + 2 * D_MLSTM:g0].T.astype(BF16),
        w_g=w_g, b_g=b_g,
        conv_sc_w=conv_sc_w[l], conv_sc_b=conv_sc_b[l][None],
        mh_norm_col=mh_norm[l][:, None],
        w_out_a=w_out[l][0:D_CONV].astype(BF16), w_out_b=w_out[l][D_CONV:].astype(BF16),
        w_up=w_up[l].astype(BF16).reshape(D_MODEL, 2 * N_FF_BLK, FF_BLK).transpose(1, 0, 2), conv_ffn_w=conv_ffn_w[l], conv_ffn_b=conv_ffn_b[l][None],
        w_down=w_down[l].astype(BF16),
    )

    seq = x_prompt.shape[1]
    y_prompt, (new_c, new_n, new_m) = _trunk(
        x_prompt, mod3, lambda b: n_lat, seq, 0, None, True, 512, 1024, 2, w)

    init = (state_C[:, l], state_n[:, l][:, :, :, None, :],
            jnp.broadcast_to(state_m[:, l][:, :, :, None, None], state_m[:, l].shape + (1, LANES)))
    y_sample, _ = _trunk(x_sample, mod3, lambda b: b, GRID_W, GRID_W, init, False, 512, 1024, 2, w)

    new_n = new_n.reshape(n_ctx, 1, 2, N_HEADS, HEAD_DIM)
    new_m = new_m[..., 0, 0]
    return y_prompt, y_sample, new_c, new_n, new_m
```

```python
import functools

import jax
import jax.numpy as jnp
from jax import lax
from jax.experimental import pallas as pl
from jax.experimental.pallas import tpu as pltpu

F32 = jnp.float32
BF16 = jnp.bfloat16

D_MODEL = 1024
GRID_W = 64
D_CONV = 512
D_MLSTM = 512
N_HEADS = 4
HEAD_DIM = 128
D_FF = 2816
CHUNK = 128
EPS = 1e-6

LANES = 128
SUBLANES = 8
BF16_ROWS = 16
FF_BLK = 256
N_FF_BLK = D_FF // FF_BLK
CONV_ROWS = 128
VMEM_LIMIT = 56 * 1024 * 1024
N_CHAINS = 2 * N_HEADS
E_ROWS = HEAD_DIM + BF16_ROWS


def _cparams(n_axes):
    return pltpu.CompilerParams(
        dimension_semantics=("arbitrary",) * n_axes, vmem_limit_bytes=VMEM_LIMIT)


def _rms_scale(x):
    return x * lax.rsqrt(jnp.mean(x * x, axis=-1, keepdims=True) + EPS)


def _dot(a, b):
    return jnp.dot(a, b, preferred_element_type=F32)


def _dot_nt(a, b):
    return lax.dot_general(a, b, (((1,), (1,)), ((), ())), preferred_element_type=F32)


def _dot_tn(a, b):
    return lax.dot_general(a, b, (((0,), (0,)), ((), ())), preferred_element_type=F32)


def _mod_kernel(c_ref, w_ref, b_ref, o_ref):
    c = c_ref[...]
    s = c * jax.nn.sigmoid(c)
    o_ref[...] = _dot(s.astype(BF16), w_ref[...].astype(BF16)) + b_ref[...]


def _mod_call(c8, w_mod, b_mod):
    n_out = w_mod.shape[1]
    blk = 1024
    return pl.pallas_call(
        _mod_kernel,
        grid=(n_out // blk,),
        in_specs=[
            pl.BlockSpec((8, D_MODEL), lambda i: (0, 0)),
            pl.BlockSpec((D_MODEL, blk), lambda i: (0, i)),
            pl.BlockSpec((1, blk), lambda i: (0, i)),
        ],
        out_specs=pl.BlockSpec((8, blk), lambda i: (0, i)),
        out_shape=jax.ShapeDtypeStruct((8, n_out), F32),
        compiler_params=_cparams(1),
        name="mod",
    )(c8, w_mod, b_mod)


def _chunk_scan(x, pos, op, fill, reverse):
    n = x.shape[0]
    k = 1
    while k < CHUNK:
        if reverse:
            shifted = pltpu.roll(x, n - k, axis=0)
            ok = pos < CHUNK - k
        else:
            shifted = pltpu.roll(x, k, axis=0)
            ok = pos >= k
        x = op(x, jnp.where(ok, shifted, fill))
        k *= 2
    return x


def _in_proj_kernel(row_len, x_ref, mod_ref, n1_ref, wsc_ref, wqk_ref, wvot_ref, wg_ref, bg_ref, cw_ref, cb_ref,
                    ysc_ref, q_ref, k_ref, vt_ref, sot_ref, gcol_ref, grow_ref):
    tt = x_ref.shape[0]
    x = x_ref[...]
    mod = mod_ref[...]
    shift1, scale1 = mod[0:1], mod[1:2]
    h = (_rms_scale(x) * n1_ref[...]) * (1.0 + scale1) + shift1
    hb = h.astype(BF16)

    zb = _dot(hb, wsc_ref[:, 0:D_CONV])
    zc = _dot(hb, wsc_ref[:, D_CONV:2 * D_CONV])
    zx = _dot(hb, wsc_ref[:, 2 * D_CONV:3 * D_CONV])
    u = zc * zx
    pos = lax.broadcasted_iota(jnp.int32, u.shape, 0) % row_len
    prev = jnp.where(pos == 0, 0.0, pltpu.roll(u, 1, axis=0))
    nxt = jnp.where(pos == row_len - 1, 0.0, pltpu.roll(u, tt - 1, axis=0))
    cw = cw_ref[...]
    conv = cw[0:1] * prev + cw[1:2] * u + cw[2:3] * nxt + cb_ref[...]
    ysc_ref[...] = (zb * conv).astype(BF16)

    q_ref[...] = (_dot(hb, wqk_ref[:, 0:D_MLSTM]) * (HEAD_DIM ** -0.5)).astype(BF16)
    k_ref[...] = _dot(hb, wqk_ref[:, D_MLSTM:2 * D_MLSTM]).astype(BF16)
    vt_ref[...] = _dot_nt(wvot_ref[0:D_MLSTM, :], hb).astype(BF16)
    sot_ref[...] = jax.nn.sigmoid(_dot_nt(wvot_ref[D_MLSTM:2 * D_MLSTM, :], hb))

    g = _dot(hb, wg_ref[...]) + bg_ref[...]
    gi = g[:, 0:LANES]
    gf = g[:, LANES:2 * LANES]
    logf = jnp.minimum(gf, 0.0) - jnp.log1p(jnp.exp(-jnp.abs(gf)))
    lane = lax.broadcasted_iota(jnp.int32, gi.shape, 1)
    cpos = lax.broadcasted_iota(jnp.int32, gi.shape, 0) % CHUNK
    fwd = lane < N_HEADS
    b = jnp.where(fwd, _chunk_scan(logf, cpos, jnp.add, 0.0, False),
                  _chunk_scan(logf, cpos, jnp.add, 0.0, True))
    r = gi - b
    cm = jnp.where(fwd, _chunk_scan(r, cpos, jnp.maximum, -jnp.inf, False),
                   _chunk_scan(r, cpos, jnp.maximum, -jnp.inf, True))
    packed = jnp.where(lane < N_CHAINS, cm,
                       jnp.where(lane < 2 * N_CHAINS, pltpu.roll(r, N_CHAINS, axis=1),
                                 pltpu.roll(b, 2 * N_CHAINS, axis=1)))
    gcol_ref[...] = packed
    for c in range(tt // CHUNK):
        grow_ref[c] = packed[c * CHUNK:(c + 1) * CHUNK].T[0:3 * N_CHAINS]


def _in_proj_call(x2d, mod3, mod_row, row_len, tt, w):
    n_tok = x2d.shape[0]
    n_tiles = n_tok // tt
    const = lambda i: (0, 0)
    tok = lambda i: (i, 0)
    tok_t = lambda i: (0, i)
    weights = [w["norm1"], w["w_sc"], w["w_qk"], w["w_vot"], w["w_g"], w["b_g"], w["conv_sc_w"], w["conv_sc_b"]]
    out_shapes = (
        jax.ShapeDtypeStruct((n_tok, D_CONV), BF16),
        jax.ShapeDtypeStruct((n_tok, D_MLSTM), BF16),
        jax.ShapeDtypeStruct((n_tok, D_MLSTM), BF16),
        jax.ShapeDtypeStruct((D_MLSTM, n_tok), BF16),
        jax.ShapeDtypeStruct((D_MLSTM, n_tok), F32),
        jax.ShapeDtypeStruct((n_tok, LANES), F32),
        jax.ShapeDtypeStruct((n_tok // CHUNK, 3 * N_CHAINS, CHUNK), F32),
    )
    return pl.pallas_call(
        functools.partial(_in_proj_kernel, row_len),
        grid=(n_tiles,),
        in_specs=[
            pl.BlockSpec((tt, D_MODEL), tok),
            pl.BlockSpec((None, 6, D_MODEL), lambda i: (mod_row(i), 0, 0)),
        ] + [pl.BlockSpec(a.shape, const) for a in weights],
        out_specs=[
            pl.BlockSpec((tt, D_CONV), tok),
            pl.BlockSpec((tt, D_MLSTM), tok),
            pl.BlockSpec((tt, D_MLSTM), tok),
            pl.BlockSpec((D_MLSTM, tt), tok_t),
            pl.BlockSpec((D_MLSTM, tt), tok_t),
            pl.BlockSpec((tt, LANES), tok),
            pl.BlockSpec((tt // CHUNK, 3 * N_CHAINS, CHUNK), lambda i: (i, 0, 0)),
        ],
        out_shape=out_shapes,
        compiler_params=_cparams(1),
        name="in_proj",
    )(x2d, mod3, *weights)


def _mlstm_kernel(cg, has_init, want_state, *refs):
    refs = list(refs)
    fwd_refs, bwd_refs = refs[0:5], refs[5:10]
    refs = refs[10:]
    if has_init:
        c0_ref, n0_ref, m0_ref = refs[:3]
        refs = refs[3:]
    htf_ref, htb_ref = refs[:2]
    refs = refs[2:]
    if want_state:
        cout_ref, nout_ref, mout_ref = refs[:3]
        refs = refs[3:]
    e_ref, m_ref = refs

    g = pl.program_id(1)

    @pl.when(g == 0)
    def _():
        if has_init:
            for d in range(2):
                for hd in range(N_HEADS):
                    j = d * N_HEADS + hd
                    e_ref[j, 0:HEAD_DIM] = c0_ref[d, hd]
                    e_ref[j, HEAD_DIM:] = jnp.broadcast_to(n0_ref[d, hd], (BF16_ROWS, HEAD_DIM))
                    m_ref[j:j + 1, :] = m0_ref[d, hd]
        else:
            e_ref[...] = jnp.zeros(e_ref.shape, F32)
            m_ref[...] = jnp.zeros(m_ref.shape, F32)

    t_idx = lax.broadcasted_iota(jnp.int32, (CHUNK, CHUNK), 0)
    s_idx = lax.broadcasted_iota(jnp.int32, (CHUNK, CHUNK), 1)
    ones_rows = jnp.ones((BF16_ROWS, CHUNK), BF16)

    def chain_step(c, d, hd):
        q_ref, k_ref, vt_ref, row_ref, col_ref = fwd_refs if d == 0 else bwd_refs
        ht_ref = htf_ref if d == 0 else htb_ref
        j = d * N_HEADS + hd
        toks = slice(c * CHUNK, (c + 1) * CHUNK)
        feat = slice(hd * HEAD_DIM, (hd + 1) * HEAD_DIM)
        last = CHUNK - 1 if d == 0 else 0
        mask = (s_idx <= t_idx) if d == 0 else (s_idx >= t_idx)

        qc = q_ref[toks, feat]
        kc = k_ref[toks, feat]
        vtc = vt_ref[feat, toks]
        cm_row = row_ref[c, j:j + 1, :]
        r_row = row_ref[c, N_CHAINS + j:N_CHAINS + j + 1, :]
        b_row = row_ref[c, 2 * N_CHAINS + j:2 * N_CHAINS + j + 1, :]
        cm_col = col_ref[toks, j:j + 1]
        m_row = m_ref[j:j + 1, :]

        big_m_col = jnp.maximum(m_row[:, j:j + 1], cm_col)
        big_m_row = jnp.maximum(m_row, cm_row)
        m_last = big_m_row[:, last:last + 1]

        w = jnp.exp(jnp.where(mask, r_row - big_m_col, -jnp.inf))
        s = (_dot_nt(qc, kc) * w).astype(BF16)
        num_t = _dot_nt(jnp.concatenate([vtc, ones_rows], axis=0), s)
        e = e_ref[j]
        inter_t = _dot_nt(e.astype(BF16), qc)
        nd = num_t + jnp.exp(m_row - big_m_row) * inter_t
        inv = 1.0 / jnp.maximum(jnp.abs(nd[HEAD_DIM:HEAD_DIM + 1]), jnp.exp(-(b_row + big_m_row)))
        ht_ref[feat, toks] = nd[0:HEAD_DIM] * inv

        wg = jnp.exp(r_row - m_last)
        lhs = jnp.concatenate([(vtc.astype(F32) * wg).astype(BF16),
                               jnp.broadcast_to(wg, (BF16_ROWS, CHUNK)).astype(BF16)], axis=0)
        e_ref[j] = jnp.exp(m_row - m_last) * e + _dot(lhs, kc)
        m_ref[j:j + 1, :] = jnp.broadcast_to(b_row[:, last:last + 1] + m_last, (1, LANES))

    for c in range(cg):
        for d in range(2):
            for hd in range(N_HEADS):
                chain_step(c if d == 0 else cg - 1 - c, d, hd)

    if want_state:
        @pl.when(g == pl.num_programs(1) - 1)
        def _():
            for d in range(2):
                for hd in range(N_HEADS):
                    j = d * N_HEADS + hd
                    cout_ref[d, hd] = e_ref[j, 0:HEAD_DIM]
                    nout_ref[d, hd] = e_ref[j, HEAD_DIM:HEAD_DIM + 1]
                    mout_ref[d, hd] = m_ref[j:j + 1, :]


def _mlstm_call(bsz, t, cg, q, k, vt, gcol, grow, init, want_state):
    n_tok = bsz * t
    tg = cg * CHUNK
    n_groups = t // tg
    fwd = lambda b, g: b * n_groups + g
    bwd = lambda b, g: b * n_groups + (n_groups - 1 - g)

    def stream_specs(pos):
        return [
            pl.BlockSpec((tg, D_MLSTM), lambda b, g: (pos(b, g), 0)),
            pl.BlockSpec((tg, D_MLSTM), lambda b, g: (pos(b, g), 0)),
            pl.BlockSpec((D_MLSTM, tg), lambda b, g: (0, pos(b, g))),
            pl.BlockSpec((cg, 3 * N_CHAINS, CHUNK), lambda b, g: (pos(b, g), 0, 0)),
            pl.BlockSpec((tg, LANES), lambda b, g: (pos(b, g), 0)),
        ]

    in_specs = stream_specs(fwd) + stream_specs(bwd)
    args = [q, k, vt, grow, gcol] * 2
    if init is not None:
        c0, n0, m0 = init
        in_specs += [
            pl.BlockSpec((None, 2, N_HEADS, HEAD_DIM, HEAD_DIM), lambda b, g: (b, 0, 0, 0, 0)),
            pl.BlockSpec((None, 2, N_HEADS, 1, HEAD_DIM), lambda b, g: (b, 0, 0, 0, 0)),
            pl.BlockSpec((None, 2, N_HEADS, 1, LANES), lambda b, g: (b, 0, 0, 0, 0)),
        ]
        args += [c0, n0, m0]
    out_specs = [
        pl.BlockSpec((D_MLSTM, tg), lambda b, g: (0, fwd(b, g))),
        pl.BlockSpec((D_MLSTM, tg), lambda b, g: (0, bwd(b, g))),
    ]
    out_shape = [jax.ShapeDtypeStruct((D_MLSTM, n_tok), F32)] * 2
    if want_state:
        out_specs += [
            pl.BlockSpec((None, None, 2, N_HEADS, HEAD_DIM, HEAD_DIM), lambda b, g: (b, 0, 0, 0, 0, 0)),
            pl.BlockSpec((None, None, 2, N_HEADS, 1, HEAD_DIM), lambda b, g: (b, 0, 0, 0, 0, 0)),
            pl.BlockSpec((None, None, 2, N_HEADS, 1, LANES), lambda b, g: (b, 0, 0, 0, 0, 0)),
        ]
        out_shape += [
            jax.ShapeDtypeStruct((bsz, 1, 2, N_HEADS, HEAD_DIM, HEAD_DIM), F32),
            jax.ShapeDtypeStruct((bsz, 1, 2, N_HEADS, 1, HEAD_DIM), F32),
            jax.ShapeDtypeStruct((bsz, 1, 2, N_HEADS, 1, LANES), F32),
        ]
    return pl.pallas_call(
        functools.partial(_mlstm_kernel, cg, init is not None, want_state),
        grid=(bsz, n_groups),
        in_specs=in_specs,
        out_specs=out_specs,
        out_shape=out_shape,
        scratch_shapes=[
            pltpu.VMEM((N_CHAINS, E_ROWS, HEAD_DIM), F32),
            pltpu.VMEM((N_CHAINS, LANES), F32),
        ],
        compiler_params=_cparams(2),
        name="mlstm",
    )(*args)


def _out_proj_kernel(x_ref, ysc_ref, htf_ref, htb_ref, sot_ref, mh_ref, mod_ref, n2_ref, wa_ref, wb_ref,
                     x1_ref, h2_ref):
    mod = mod_ref[...]
    gate1, shift2, scale2 = mod[2:3], mod[3:4], mod[4:5]
    hs = htf_ref[...] + htb_ref[...]
    heads = []
    for hd in range(N_HEADS):
        blk = hs[hd * HEAD_DIM:(hd + 1) * HEAD_DIM]
        heads.append(blk * lax.rsqrt(jnp.mean(blk * blk, axis=0, keepdims=True) + EPS))
    hm_t = ((jnp.concatenate(heads, axis=0) * mh_ref[...]) * sot_ref[...]).astype(BF16)
    mix = _dot(ysc_ref[...], wa_ref[...]) + _dot_tn(hm_t, wb_ref[...])
    x1 = x_ref[...] + gate1 * mix
    x1_ref[...] = x1
    h2_ref[...] = ((_rms_scale(x1) * n2_ref[...]) * (1.0 + scale2) + shift2).astype(BF16)


def _out_proj_call(x2d, ysc, htf, htb, sot, mod3, mod_row, tt, w):
    n_tok = x2d.shape[0]
    const = lambda i: (0, 0)
    tok = lambda i: (i, 0)
    tok_t = lambda i: (0, i)
    return pl.pallas_call(
        _out_proj_kernel,
        grid=(n_tok // tt,),
        in_specs=[
            pl.BlockSpec((tt, D_MODEL), tok),
            pl.BlockSpec((tt, D_CONV), tok),
            pl.BlockSpec((D_MLSTM, tt), tok_t),
            pl.BlockSpec((D_MLSTM, tt), tok_t),
            pl.BlockSpec((D_MLSTM, tt), tok_t),
            pl.BlockSpec((D_MLSTM, 1), const),
            pl.BlockSpec((None, 6, D_MODEL), lambda i: (mod_row(i), 0, 0)),
            pl.BlockSpec((1, D_MODEL), const),
            pl.BlockSpec(w["w_out_a"].shape, const),
            pl.BlockSpec(w["w_out_b"].shape, const),
        ],
        out_specs=[pl.BlockSpec((tt, D_MODEL), tok), pl.BlockSpec((tt, D_MODEL), tok)],
        out_shape=[jax.ShapeDtypeStruct((n_tok, D_MODEL), F32),
                   jax.ShapeDtypeStruct((n_tok, D_MODEL), BF16)],
        compiler_params=_cparams(1),
        name="out_proj",
    )(x2d, ysc, htf, htb, sot, w["mh_norm_col"], mod3, w["norm2"], w["w_out_a"], w["w_out_b"])


def _ffn_kernel(halo, seq_len, tiles_per_image, *refs):
    if halo:
        h2_ref, prev_ref, next_ref = refs[:3]
        refs = refs[3:]
    else:
        h2_ref = refs[0]
        refs = refs[1:]
    (x1_ref, mod_ref, fn_ref, wu_ref, cw_ref, cb_ref, wd_ref, y_ref,
     act_ref, a0_ref, a1_ref, g0_ref, g1_ref, h2e_ref) = refs
    slots = ((a0_ref, g0_ref), (a1_ref, g1_ref))
    tt = x1_ref.shape[0]
    i = pl.program_id(0)
    last_blk = N_FF_BLK - 1
    pad = halo if halo else SUBLANES
    shift = halo if halo else 1
    nr = CONV_ROWS
    row = lax.broadcasted_iota(jnp.int32, (nr, FF_BLK), 0)

    def up_proj(blk, slot):
        a_ref, g_ref = slots[slot]
        off_a = pl.multiple_of(blk * FF_BLK, FF_BLK)
        off_g = pl.multiple_of(blk * FF_BLK + D_FF, FF_BLK)
        hin = h2e_ref[...]
        rows = slice(0, tt + 2 * halo) if halo else slice(pad, pad + tt)
        a_ref[rows] = _dot(hin, wu_ref[:, pl.ds(off_a, FF_BLK)])
        g_ref[rows] = _dot(hin, wu_ref[:, pl.ds(off_g, FF_BLK)])

    def conv_act(blk, slot):
        a_ref, g_ref = slots[slot]
        off_a = pl.multiple_of(blk * FF_BLK, FF_BLK)
        off_g = pl.multiple_of(blk * FF_BLK + D_FF, FF_BLK)

        def conv(z_ref, off, r0):
            cw = cw_ref[:, pl.ds(off, FF_BLK)]
            cb = cb_ref[:, pl.ds(off, FF_BLK)]
            prev = z_ref[pad - shift + r0:pad - shift + r0 + nr]
            nxt = z_ref[pad + shift + r0:pad + shift + r0 + nr]
            if not halo:
                if r0 % seq_len == 0:
                    prev = jnp.where(row == 0, 0.0, prev)
                if (r0 + nr) % seq_len == 0:
                    nxt = jnp.where(row == nr - 1, 0.0, nxt)
            return cw[0:1] * prev + cw[1:2] * z_ref[pad + r0:pad + r0 + nr] + cw[2:3] * nxt + cb

        for r0 in range(0, tt, nr):
            ac = conv(a_ref, off_a, r0)
            gc = conv(g_ref, off_g, r0)
            act_ref[r0:r0 + nr, pl.ds(off_a, FF_BLK)] = (gc * jax.nn.sigmoid(gc) * ac).astype(BF16)

    h2e_ref[halo:halo + tt] = h2_ref[...]
    if halo:
        h2e_ref[0:halo] = prev_ref[...]
        h2e_ref[halo + tt:] = next_ref[...]

        @pl.when(i % tiles_per_image == 0)
        def _():
            h2e_ref[0:halo] = jnp.zeros((halo, D_MODEL), BF16)

        @pl.when(i % tiles_per_image == tiles_per_image - 1)
        def _():
            h2e_ref[halo + tt:] = jnp.zeros((halo, D_MODEL), BF16)
    else:
        for z_ref in (a0_ref, a1_ref, g0_ref, g1_ref):
            z_ref[0:pad] = jnp.zeros((pad, FF_BLK), F32)
            z_ref[pad + tt:] = jnp.zeros((pad, FF_BLK), F32)

    up_proj(0, 0)

    def block_pair(p, carry):
        blk = 2 * p + 1
        conv_act(blk - 1, 0)
        up_proj(blk, 1)
        conv_act(blk, 1)
        up_proj(blk + 1, 0)
        return carry

    lax.fori_loop(0, last_blk // 2, block_pair, 0)

    k0 = last_blk * FF_BLK
    part = _dot(act_ref[:, 0:k0], wd_ref[0:k0, :])
    conv_act(last_blk, 0)
    ffn = part + _dot(act_ref[:, k0:], wd_ref[k0:, :])
    x2 = x1_ref[...] + mod_ref[5:6] * ffn
    y_ref[...] = _rms_scale(x2) * fn_ref[...]


def _ffn_call(h2, x1, mod3, mod_row, tt, halo, seq_len, w):
    n_tok = x1.shape[0]
    n_tiles = n_tok // tt
    tok = lambda i: (i, 0)
    const = lambda i: (0, 0)
    resident = dict(pipeline_mode=pl.Buffered(1))
    in_specs = [pl.BlockSpec((tt, D_MODEL), tok)]
    args = [h2]
    up_rows = tt + 2 * (halo if halo else SUBLANES)
    scratch = [pltpu.VMEM((tt, D_FF), BF16)] + [pltpu.VMEM((up_rows, FF_BLK), F32)] * 4
    tiles_per_image = 1
    if halo:
        tiles_per_image = seq_len // tt
        per = tt // halo
        n_halo_blk = n_tok // halo
        in_specs += [
            pl.BlockSpec((halo, D_MODEL), lambda i: (jnp.maximum(i * per - 1, 0), 0)),
            pl.BlockSpec((halo, D_MODEL), lambda i: (jnp.minimum((i + 1) * per, n_halo_blk - 1), 0)),
        ]
        args += [h2, h2]
    scratch.append(pltpu.VMEM((tt + 2 * halo, D_MODEL), BF16))
    in_specs += [
        pl.BlockSpec((tt, D_MODEL), tok),
        pl.BlockSpec((None, 6, D_MODEL), lambda i: (mod_row(i), 0, 0)),
        pl.BlockSpec((1, D_MODEL), const),
        pl.BlockSpec((D_MODEL, 2 * D_FF), const, **resident),
        pl.BlockSpec((3, 2 * D_FF), const),
        pl.BlockSpec((1, 2 * D_FF), const),
        pl.BlockSpec((D_FF, D_MODEL), const, **resident),
    ]
    args += [x1, mod3, w["final_norm"], w["w_up"], w["conv_ffn_w"], w["conv_ffn_b"], w["w_down"]]
    return pl.pallas_call(
        functools.partial(_ffn_kernel, halo, seq_len, tiles_per_image),
        grid=(n_tiles,),
        in_specs=in_specs,
        out_specs=pl.BlockSpec((tt, D_MODEL), tok),
        out_shape=jax.ShapeDtypeStruct((n_tok, D_MODEL), F32),
        scratch_shapes=scratch,
        compiler_params=_cparams(1),
        name="ffn",
    )(*args)


def _trunk(x, mod3, mod_row_of_batch, row_len, ffn_halo, init, want_state, tt_in, tt_ffn, cg, w):
    bsz, t, _ = x.shape
    x2d = x.reshape(bsz * t, D_MODEL)

    def mod_row(tile_tokens):
        return lambda i: mod_row_of_batch((i * tile_tokens) // t)

    ysc, q, k, vt, sot, gcol, grow = _in_proj_call(x2d, mod3, mod_row(tt_in), row_len, tt_in, w)
    outs = _mlstm_call(bsz, t, cg, q, k, vt, gcol, grow, init, want_state)
    x1, h2 = _out_proj_call(x2d, ysc, outs[0], outs[1], sot, mod3, mod_row(tt_in), tt_in, w)
    y = _ffn_call(h2, x1, mod3, mod_row(tt_ffn), tt_ffn, ffn_halo, t, w)
    return y.reshape(bsz, t, D_MODEL), outs[2:]


def kernel(x_prompt, x_sample, state_C, state_n, state_m, c, c_ctx, w_mod, b_mod, norm1, w_in, b_gate,
           conv_sc_w, conv_sc_b, mh_norm, w_out, norm2, w_up, conv_ffn_w, conv_ffn_b, w_down, final_norm):
    n_lat = c.shape[0]
    n_ctx = x_prompt.shape[0]
    c8 = jnp.concatenate([c, c_ctx[None], jnp.zeros((8 - n_lat - 1, D_MODEL), F32)], axis=0)
    l = 0
    mod3 = _mod_call(c8, w_mod[l], b_mod[l][None]).reshape(8, 6, D_MODEL)

    wi = w_in[l]
    q0 = 3 * D_CONV
    g0 = q0 + 4 * D_MLSTM
    zpad = jnp.zeros((D_MODEL, LANES - N_CHAINS), F32)
    cols = lambda a: wi[:, g0 + a * N_HEADS:g0 + (a + 1) * N_HEADS]
    w_g = jnp.concatenate([cols(0), cols(2), zpad, cols(1), cols(3), zpad], axis=1).astype(BF16)
    bpad = jnp.zeros((LANES - N_CHAINS,), F32)
    bg = b_gate[l].astype(F32)
    b_g = jnp.concatenate([bg[0], bg[2], bpad, bg[1], bg[3], bpad])[None]
    w = dict(
        norm1=norm1[l][None], norm2=norm2[l][None], final_norm=final_norm[None],
        w_sc=wi[:, 0:q0].astype(BF16),
        w_qk=wi[:, q0:q0 + 2 * D_MLSTM].astype(BF16),
        w_vot=wi[:, q0 + 2 * D_MLSTM:g0].T.astype(BF16),
        w_g=w_g, b_g=b_g,
        conv_sc_w=conv_sc_w[l], conv_sc_b=conv_sc_b[l][None],
        mh_norm_col=mh_norm[l][:, None],
        w_out_a=w_out[l][0:D_CONV].astype(BF16), w_out_b=w_out[l][D_CONV:].astype(BF16),
        w_up=w_up[l].astype(BF16), conv_ffn_w=conv_ffn_w[l], conv_ffn_b=conv_ffn_b[l][None],
        w_down=w_down[l].astype(BF16),
    )

    seq = x_prompt.shape[1]
    y_prompt, (new_c, new_n, new_m) = _trunk(
        x_prompt, mod3, lambda b: n_lat, seq, 0, None, True, 512, 1024, 2, w)

    init = (state_C[:, l], state_n[:, l][:, :, :, None, :],
            jnp.broadcast_to(state_m[:, l][:, :, :, None, None], state_m[:, l].shape + (1, LANES)))
    y_sample, _ = _trunk(x_sample, mod3, lambda b: b, GRID_W, GRID_W, init, False, 512, 1024, 2, w)

    new_n = new_n.reshape(n_ctx, 1, 2, N_HEADS, HEAD_DIM)
    new_m = new_m[..., 0, 0]
    return y_prompt, y_sample, new_c, new_n, new_m
```

```python
import functools

import jax
import jax.numpy as jnp
from jax import lax
from jax.experimental import pallas as pl
from jax.experimental.pallas import tpu as pltpu

F32 = jnp.float32
BF16 = jnp.bfloat16

D_MODEL = 1024
GRID_W = 64
D_CONV = 512
D_MLSTM = 512
N_HEADS = 4
HEAD_DIM = 128
D_FF = 2816
CHUNK = 256
EPS = 1e-6

LANES = 128
SUBLANES = 8
BF16_ROWS = 16
FF_BLK = 256
N_FF_BLK = D_FF // FF_BLK
CONV_ROWS = 128
VMEM_LIMIT = 56 * 1024 * 1024
N_CHAINS = 2 * N_HEADS
E_ROWS = HEAD_DIM + BF16_ROWS


def _cparams(n_axes):
    return pltpu.CompilerParams(
        dimension_semantics=("arbitrary",) * n_axes, vmem_limit_bytes=VMEM_LIMIT)


def _rms_scale(x):
    return x * lax.rsqrt(jnp.mean(x * x, axis=-1, keepdims=True) + EPS)


def _dot(a, b):
    return jnp.dot(a, b, preferred_element_type=F32)


def _dot_nt(a, b):
    return lax.dot_general(a, b, (((1,), (1,)), ((), ())), preferred_element_type=F32)


def _dot_tn(a, b):
    return lax.dot_general(a, b, (((0,), (0,)), ((), ())), preferred_element_type=F32)


def _mod_kernel(c_ref, w_ref, b_ref, o_ref):
    c = c_ref[...]
    s = c * jax.nn.sigmoid(c)
    o_ref[...] = _dot(s.astype(BF16), w_ref[...].astype(BF16)) + b_ref[...]


def _mod_call(c8, w_mod, b_mod):
    n_out = w_mod.shape[1]
    blk = 1024
    return pl.pallas_call(
        _mod_kernel,
        grid=(n_out // blk,),
        in_specs=[
            pl.BlockSpec((8, D_MODEL), lambda i: (0, 0)),
            pl.BlockSpec((D_MODEL, blk), lambda i: (0, i)),
            pl.BlockSpec((1, blk), lambda i: (0, i)),
        ],
        out_specs=pl.BlockSpec((8, blk), lambda i: (0, i)),
        out_shape=jax.ShapeDtypeStruct((8, n_out), F32),
        compiler_params=_cparams(1),
        name="mod",
    )(c8, w_mod, b_mod)


def _chunk_scan(x, pos, op, fill, reverse):
    n = x.shape[0]
    k = 1
    while k < CHUNK:
        if reverse:
            shifted = pltpu.roll(x, n - k, axis=0)
            ok = pos < CHUNK - k
        else:
            shifted = pltpu.roll(x, k, axis=0)
            ok = pos >= k
        x = op(x, jnp.where(ok, shifted, fill))
        k *= 2
    return x


def _in_proj_kernel(row_len, x_ref, mod_ref, n1_ref, wsc_ref, wk_ref, wqvot_ref, wg_ref, bg_ref, cw_ref, cb_ref,
                    ysc_ref, qt_ref, k_ref, vt_ref, sot_ref, gcol_ref, grow_ref):
    tt = x_ref.shape[0]
    x = x_ref[...]
    mod = mod_ref[...]
    shift1, scale1 = mod[0:1], mod[1:2]
    h = (_rms_scale(x) * n1_ref[...]) * (1.0 + scale1) + shift1
    hb = h.astype(BF16)

    zb = _dot(hb, wsc_ref[:, 0:D_CONV])
    zc = _dot(hb, wsc_ref[:, D_CONV:2 * D_CONV])
    zx = _dot(hb, wsc_ref[:, 2 * D_CONV:3 * D_CONV])
    u = zc * zx
    pos = lax.broadcasted_iota(jnp.int32, u.shape, 0) % row_len
    prev = jnp.where(pos == 0, 0.0, pltpu.roll(u, 1, axis=0))
    nxt = jnp.where(pos == row_len - 1, 0.0, pltpu.roll(u, tt - 1, axis=0))
    cw = cw_ref[...]
    conv = cw[0:1] * prev + cw[1:2] * u + cw[2:3] * nxt + cb_ref[...]
    ysc_ref[...] = (zb * conv).astype(BF16)

    k_ref[...] = _dot(hb, wk_ref[...]).astype(BF16)
    qt_ref[...] = (_dot_nt(wqvot_ref[0:D_MLSTM, :], hb) * (HEAD_DIM ** -0.5)).astype(BF16)
    vt_ref[...] = _dot_nt(wqvot_ref[D_MLSTM:2 * D_MLSTM, :], hb).astype(BF16)
    sot_ref[...] = jax.nn.sigmoid(_dot_nt(wqvot_ref[2 * D_MLSTM:3 * D_MLSTM, :], hb))

    g = _dot(hb, wg_ref[...]) + bg_ref[...]
    gi = g[:, 0:LANES]
    gf = g[:, LANES:2 * LANES]
    logf = jnp.minimum(gf, 0.0) - jnp.log1p(jnp.exp(-jnp.abs(gf)))
    lane = lax.broadcasted_iota(jnp.int32, gi.shape, 1)
    cpos = lax.broadcasted_iota(jnp.int32, gi.shape, 0) % CHUNK
    fwd = lane < N_HEADS
    b = jnp.where(fwd, _chunk_scan(logf, cpos, jnp.add, 0.0, False),
                  _chunk_scan(logf, cpos, jnp.add, 0.0, True))
    r = gi - b
    cm = jnp.where(fwd, _chunk_scan(r, cpos, jnp.maximum, -jnp.inf, False),
                   _chunk_scan(r, cpos, jnp.maximum, -jnp.inf, True))
    packed = jnp.where(lane < N_CHAINS, cm,
                       jnp.where(lane < 2 * N_CHAINS, pltpu.roll(r, N_CHAINS, axis=1),
                                 pltpu.roll(b, 2 * N_CHAINS, axis=1)))
    gcol_ref[...] = packed
    for c in range(tt // CHUNK):
        grow_ref[c] = packed[c * CHUNK:(c + 1) * CHUNK].T[0:3 * N_CHAINS]


def _in_proj_call(x2d, mod3, mod_row, row_len, tt, w):
    n_tok = x2d.shape[0]
    n_tiles = n_tok // tt
    const = lambda i: (0, 0)
    tok = lambda i: (i, 0)
    tok_t = lambda i: (0, i)
    weights = [w["norm1"], w["w_sc"], w["w_k"], w["w_qvot"], w["w_g"], w["b_g"], w["conv_sc_w"], w["conv_sc_b"]]
    out_shapes = (
        jax.ShapeDtypeStruct((n_tok, D_CONV), BF16),
        jax.ShapeDtypeStruct((D_MLSTM, n_tok), BF16),
        jax.ShapeDtypeStruct((n_tok, D_MLSTM), BF16),
        jax.ShapeDtypeStruct((D_MLSTM, n_tok), BF16),
        jax.ShapeDtypeStruct((D_MLSTM, n_tok), F32),
        jax.ShapeDtypeStruct((n_tok, LANES), F32),
        jax.ShapeDtypeStruct((n_tok // CHUNK, 3 * N_CHAINS, CHUNK), F32),
    )
    return pl.pallas_call(
        functools.partial(_in_proj_kernel, row_len),
        grid=(n_tiles,),
        in_specs=[
            pl.BlockSpec((tt, D_MODEL), tok),
            pl.BlockSpec((None, 6, D_MODEL), lambda i: (mod_row(i), 0, 0)),
        ] + [pl.BlockSpec(a.shape, const) for a in weights],
        out_specs=[
            pl.BlockSpec((tt, D_CONV), tok),
            pl.BlockSpec((D_MLSTM, tt), tok_t),
            pl.BlockSpec((tt, D_MLSTM), tok),
            pl.BlockSpec((D_MLSTM, tt), tok_t),
            pl.BlockSpec((D_MLSTM, tt), tok_t),
            pl.BlockSpec((tt, LANES), tok),
            pl.BlockSpec((tt // CHUNK, 3 * N_CHAINS, CHUNK), lambda i: (i, 0, 0)),
        ],
        out_shape=out_shapes,
        compiler_params=_cparams(1),
        name="in_proj",
    )(x2d, mod3, *weights)


def _mlstm_kernel(cg, has_init, want_state, *refs):
    refs = list(refs)
    fwd_refs, bwd_refs = refs[0:5], refs[5:10]
    refs = refs[10:]
    if has_init:
        c0_ref, n0_ref, m0_ref = refs[:3]
        refs = refs[3:]
    htf_ref, htb_ref = refs[:2]
    refs = refs[2:]
    if want_state:
        cout_ref, nout_ref, mout_ref = refs[:3]
        refs = refs[3:]
    e_ref, m_ref = refs

    g = pl.program_id(1)

    @pl.when(g == 0)
    def _():
        if has_init:
            for d in range(2):
                for hd in range(N_HEADS):
                    j = d * N_HEADS + hd
                    e_ref[j, 0:HEAD_DIM] = c0_ref[d, hd]
                    e_ref[j, HEAD_DIM:] = jnp.broadcast_to(n0_ref[d, hd], (BF16_ROWS, HEAD_DIM))
                    m_ref[j] = jnp.broadcast_to(m0_ref[d, hd], (SUBLANES, LANES))
        else:
            e_ref[...] = jnp.zeros(e_ref.shape, F32)
            m_ref[...] = jnp.zeros(m_ref.shape, F32)

    s_idx = lax.broadcasted_iota(jnp.int32, (CHUNK, CHUNK), 0)
    t_idx = lax.broadcasted_iota(jnp.int32, (CHUNK, CHUNK), 1)
    ones_rows = jnp.ones((BF16_ROWS, CHUNK), BF16)

    def chain_step(c, d, hd):
        qt_ref, k_ref, vt_ref, row_ref, col_ref = fwd_refs if d == 0 else bwd_refs
        ht_ref = htf_ref if d == 0 else htb_ref
        j = d * N_HEADS + hd
        toks = slice(c * CHUNK, (c + 1) * CHUNK)
        feat = slice(hd * HEAD_DIM, (hd + 1) * HEAD_DIM)
        last = CHUNK - 1 if d == 0 else 0
        mask = (s_idx <= t_idx) if d == 0 else (s_idx >= t_idx)

        qtc = qt_ref[feat, toks]
        kc = k_ref[toks, feat]
        vtc = vt_ref[feat, toks]
        cm_row = row_ref[c, j:j + 1, :]
        r_row = row_ref[c, N_CHAINS + j:N_CHAINS + j + 1, :]
        b_row = row_ref[c, 2 * N_CHAINS + j:2 * N_CHAINS + j + 1, :]
        r_col = col_ref[toks, N_CHAINS + j:N_CHAINS + j + 1]
        m_row = m_ref[j, 0:1, :]
        m_t = m_row[:, 0:1]

        big_m_row = jnp.maximum(m_t, cm_row)
        m_last = big_m_row[:, last:last + 1]

        w_t = jnp.exp(jnp.where(mask, r_col - big_m_row, -jnp.inf))
        e = e_ref[j]
        kq = _dot(jnp.concatenate([kc, e.astype(BF16)], axis=0), qtc)
        s_t = (kq[0:CHUNK] * w_t).astype(BF16)
        num_t = _dot(jnp.concatenate([vtc, ones_rows], axis=0), s_t)
        nd = num_t + jnp.exp(m_t - big_m_row) * kq[CHUNK:]
        inv = 1.0 / jnp.maximum(jnp.abs(nd[HEAD_DIM:HEAD_DIM + 1]), jnp.exp(-(b_row + big_m_row)))
        ht_ref[feat, toks] = nd[0:HEAD_DIM] * inv

        wg = jnp.exp(r_row - m_last)
        lhs = jnp.concatenate([(vtc.astype(F32) * wg).astype(BF16),
                               jnp.broadcast_to(wg, (BF16_ROWS, CHUNK)).astype(BF16)], axis=0)
        e_ref[j] = jnp.exp(m_row - m_last) * e + _dot(lhs, kc)
        m_ref[j] = jnp.broadcast_to(b_row[:, last:last + 1] + m_last, (SUBLANES, LANES))

    for c in range(cg):
        for d in range(2):
            for hd in range(N_HEADS):
                chain_step(c if d == 0 else cg - 1 - c, d, hd)

    if want_state:
        @pl.when(g == pl.num_programs(1) - 1)
        def _():
            for d in range(2):
                for hd in range(N_HEADS):
                    j = d * N_HEADS + hd
                    cout_ref[d, hd] = e_ref[j, 0:HEAD_DIM]
                    nout_ref[d, hd] = e_ref[j, HEAD_DIM:HEAD_DIM + 1]
                    mout_ref[d, hd] = m_ref[j, 0:1, :]


def _mlstm_call(bsz, t, cg, qt, k, vt, gcol, grow, init, want_state):
    n_tok = bsz * t
    tg = cg * CHUNK
    n_groups = t // tg
    fwd = lambda b, g: b * n_groups + g
    bwd = lambda b, g: b * n_groups + (n_groups - 1 - g)

    def stream_specs(pos):
        return [
            pl.BlockSpec((D_MLSTM, tg), lambda b, g: (0, pos(b, g))),
            pl.BlockSpec((tg, D_MLSTM), lambda b, g: (pos(b, g), 0)),
            pl.BlockSpec((D_MLSTM, tg), lambda b, g: (0, pos(b, g))),
            pl.BlockSpec((cg, 3 * N_CHAINS, CHUNK), lambda b, g: (pos(b, g), 0, 0)),
            pl.BlockSpec((tg, LANES), lambda b, g: (pos(b, g), 0)),
        ]

    in_specs = stream_specs(fwd) + stream_specs(bwd)
    args = [qt, k, vt, grow, gcol] * 2
    if init is not None:
        c0, n0, m0 = init
        in_specs += [
            pl.BlockSpec((None, 2, N_HEADS, HEAD_DIM, HEAD_DIM), lambda b, g: (b, 0, 0, 0, 0)),
            pl.BlockSpec((None, 2, N_HEADS, 1, HEAD_DIM), lambda b, g: (b, 0, 0, 0, 0)),
            pl.BlockSpec((None, 2, N_HEADS, 1, LANES), lambda b, g: (b, 0, 0, 0, 0)),
        ]
        args += [c0, n0, m0]
    out_specs = [
        pl.BlockSpec((D_MLSTM, tg), lambda b, g: (0, fwd(b, g))),
        pl.BlockSpec((D_MLSTM, tg), lambda b, g: (0, bwd(b, g))),
    ]
    out_shape = [jax.ShapeDtypeStruct((D_MLSTM, n_tok), F32)] * 2
    if want_state:
        out_specs += [
            pl.BlockSpec((None, None, 2, N_HEADS, HEAD_DIM, HEAD_DIM), lambda b, g: (b, 0, 0, 0, 0, 0)),
            pl.BlockSpec((None, None, 2, N_HEADS, 1, HEAD_DIM), lambda b, g: (b, 0, 0, 0, 0, 0)),
            pl.BlockSpec((None, None, 2, N_HEADS, 1, LANES), lambda b, g: (b, 0, 0, 0, 0, 0)),
        ]
        out_shape += [
            jax.ShapeDtypeStruct((bsz, 1, 2, N_HEADS, HEAD_DIM, HEAD_DIM), F32),
            jax.ShapeDtypeStruct((bsz, 1, 2, N_HEADS, 1, HEAD_DIM), F32),
            jax.ShapeDtypeStruct((bsz, 1, 2, N_HEADS, 1, LANES), F32),
        ]
    return pl.pallas_call(
        functools.partial(_mlstm_kernel, cg, init is not None, want_state),
        grid=(bsz, n_groups),
        in_specs=in_specs,
        out_specs=out_specs,
        out_shape=out_shape,
        scratch_shapes=[
            pltpu.VMEM((N_CHAINS, E_ROWS, HEAD_DIM), F32),
            pltpu.VMEM((N_CHAINS, SUBLANES, LANES), F32),
        ],
        compiler_params=_cparams(2),
        name="mlstm",
    )(*args)


def _out_proj_kernel(x_ref, ysc_ref, htf_ref, htb_ref, sot_ref, mh_ref, mod_ref, n2_ref, wa_ref, wb_ref,
                     x1_ref, h2_ref):
    mod = mod_ref[...]
    gate1, shift2, scale2 = mod[2:3], mod[3:4], mod[4:5]
    hs = htf_ref[...] + htb_ref[...]
    heads = []
    for hd in range(N_HEADS):
        blk = hs[hd * HEAD_DIM:(hd + 1) * HEAD_DIM]
        heads.append(blk * lax.rsqrt(jnp.mean(blk * blk, axis=0, keepdims=True) + EPS))
    hm_t = ((jnp.concatenate(heads, axis=0) * mh_ref[...]) * sot_ref[...]).astype(BF16)
    mix = _dot(ysc_ref[...], wa_ref[...]) + _dot_tn(hm_t, wb_ref[...])
    x1 = x_ref[...] + gate1 * mix
    x1_ref[...] = x1
    h2_ref[...] = ((_rms_scale(x1) * n2_ref[...]) * (1.0 + scale2) + shift2).astype(BF16)


def _out_proj_call(x2d, ysc, htf, htb, sot, mod3, mod_row, tt, w):
    n_tok = x2d.shape[0]
    const = lambda i: (0, 0)
    tok = lambda i: (i, 0)
    tok_t = lambda i: (0, i)
    return pl.pallas_call(
        _out_proj_kernel,
        grid=(n_tok // tt,),
        in_specs=[
            pl.BlockSpec((tt, D_MODEL), tok),
            pl.BlockSpec((tt, D_CONV), tok),
            pl.BlockSpec((D_MLSTM, tt), tok_t),
            pl.BlockSpec((D_MLSTM, tt), tok_t),
            pl.BlockSpec((D_MLSTM, tt), tok_t),
            pl.BlockSpec((D_MLSTM, 1), const),
            pl.BlockSpec((None, 6, D_MODEL), lambda i: (mod_row(i), 0, 0)),
            pl.BlockSpec((1, D_MODEL), const),
            pl.BlockSpec(w["w_out_a"].shape, const),
            pl.BlockSpec(w["w_out_b"].shape, const),
        ],
        out_specs=[pl.BlockSpec((tt, D_MODEL), tok), pl.BlockSpec((tt, D_MODEL), tok)],
        out_shape=[jax.ShapeDtypeStruct((n_tok, D_MODEL), F32),
                   jax.ShapeDtypeStruct((n_tok, D_MODEL), BF16)],
        compiler_params=_cparams(1),
        name="out_proj",
    )(x2d, ysc, htf, htb, sot, w["mh_norm_col"], mod3, w["norm2"], w["w_out_a"], w["w_out_b"])


def _ffn_kernel(halo, seq_len, tiles_per_image, *refs):
    if halo:
        h2_ref, prev_ref, next_ref = refs[:3]
        refs = refs[3:]
    else:
        h2_ref = refs[0]
        refs = refs[1:]
    (x1_ref, mod_ref, fn_ref, wu_ref, cw_ref, cb_ref, wd_ref, y_ref,
     act_ref, a0_ref, a1_ref, g0_ref, g1_ref, h2e_ref) = refs
    slots = ((a0_ref, g0_ref), (a1_ref, g1_ref))
    tt = x1_ref.shape[0]
    i = pl.program_id(0)
    last_blk = N_FF_BLK - 1
    pad = halo if halo else SUBLANES
    shift = halo if halo else 1
    nr = CONV_ROWS
    row = lax.broadcasted_iota(jnp.int32, (nr, FF_BLK), 0)

    def up_proj(blk, slot):
        a_ref, g_ref = slots[slot]
        off_a = pl.multiple_of(blk * FF_BLK, FF_BLK)
        off_g = pl.multiple_of(blk * FF_BLK + D_FF, FF_BLK)
        hin = h2e_ref[...]
        rows = slice(0, tt + 2 * halo) if halo else slice(pad, pad + tt)
        a_ref[rows] = _dot(hin, wu_ref[:, pl.ds(off_a, FF_BLK)])
        g_ref[rows] = _dot(hin, wu_ref[:, pl.ds(off_g, FF_BLK)])

    def conv_act(blk, slot):
        a_ref, g_ref = slots[slot]
        off_a = pl.multiple_of(blk * FF_BLK, FF_BLK)
        off_g = pl.multiple_of(blk * FF_BLK + D_FF, FF_BLK)

        def conv(z_ref, off, r0):
            cw = cw_ref[:, pl.ds(off, FF_BLK)]
            cb = cb_ref[:, pl.ds(off, FF_BLK)]
            prev = z_ref[pad - shift + r0:pad - shift + r0 + nr]
            nxt = z_ref[pad + shift + r0:pad + shift + r0 + nr]
            if not halo:
                if r0 % seq_len == 0:
                    prev = jnp.where(row == 0, 0.0, prev)
                if (r0 + nr) % seq_len == 0:
                    nxt = jnp.where(row == nr - 1, 0.0, nxt)
            return cw[0:1] * prev + cw[1:2] * z_ref[pad + r0:pad + r0 + nr] + cw[2:3] * nxt + cb

        for r0 in range(0, tt, nr):
            ac = conv(a_ref, off_a, r0)
            gc = conv(g_ref, off_g, r0)
            act_ref[r0:r0 + nr, pl.ds(off_a, FF_BLK)] = (gc * jax.nn.sigmoid(gc) * ac).astype(BF16)

    h2e_ref[halo:halo + tt] = h2_ref[...]
    if halo:
        h2e_ref[0:halo] = prev_ref[...]
        h2e_ref[halo + tt:] = next_ref[...]

        @pl.when(i % tiles_per_image == 0)
        def _():
            h2e_ref[0:halo] = jnp.zeros((halo, D_MODEL), BF16)

        @pl.when(i % tiles_per_image == tiles_per_image - 1)
        def _():
            h2e_ref[halo + tt:] = jnp.zeros((halo, D_MODEL), BF16)
    else:
        for z_ref in (a0_ref, a1_ref, g0_ref, g1_ref):
            z_ref[0:pad] = jnp.zeros((pad, FF_BLK), F32)
            z_ref[pad + tt:] = jnp.zeros((pad, FF_BLK), F32)

    up_proj(0, 0)

    def block_pair(p, carry):
        blk = 2 * p + 1
        conv_act(blk - 1, 0)
        up_proj(blk, 1)
        conv_act(blk, 1)
        up_proj(blk + 1, 0)
        return carry

    lax.fori_loop(0, last_blk // 2, block_pair, 0)

    k0 = last_blk * FF_BLK
    part = _dot(act_ref[:, 0:k0], wd_ref[0:k0, :])
    conv_act(last_blk, 0)
    ffn = part + _dot(act_ref[:, k0:], wd_ref[k0:, :])
    x2 = x1_ref[...] + mod_ref[5:6] * ffn
    y_ref[...] = _rms_scale(x2) * fn_ref[...]


def _ffn_call(h2, x1, mod3, mod_row, tt, halo, seq_len, w):
    n_tok = x1.shape[0]
    n_tiles = n_tok // tt
    tok = lambda i: (i, 0)
    const = lambda i: (0, 0)
    resident = dict(pipeline_mode=pl.Buffered(1))
    in_specs = [pl.BlockSpec((tt, D_MODEL), tok)]
    args = [h2]
    up_rows = tt + 2 * (halo if halo else SUBLANES)
    scratch = [pltpu.VMEM((tt, D_FF), BF16)] + [pltpu.VMEM((up_rows, FF_BLK), F32)] * 4
    tiles_per_image = 1
    if halo:
        tiles_per_image = seq_len // tt
        per = tt // halo
        n_halo_blk = n_tok // halo
        in_specs += [
            pl.BlockSpec((halo, D_MODEL), lambda i: (jnp.maximum(i * per - 1, 0), 0)),
            pl.BlockSpec((halo, D_MODEL), lambda i: (jnp.minimum((i + 1) * per, n_halo_blk - 1), 0)),
        ]
        args += [h2, h2]
    scratch.append(pltpu.VMEM((tt + 2 * halo, D_MODEL), BF16))
    in_specs += [
        pl.BlockSpec((tt, D_MODEL), tok),
        pl.BlockSpec((None, 6, D_MODEL), lambda i: (mod_row(i), 0, 0)),
        pl.BlockSpec((1, D_MODEL), const),
        pl.BlockSpec((D_MODEL, 2 * D_FF), const, **resident),
        pl.BlockSpec((3, 2 * D_FF), const),
        pl.BlockSpec((1, 2 * D_FF), const),
        pl.BlockSpec((D_FF, D_MODEL), const, **resident),
    ]
    args += [x1, mod3, w["final_norm"], w["w_up"], w["conv_ffn_w"], w["conv_ffn_b"], w["w_down"]]
    return pl.pallas_call(
        functools.partial(_ffn_kernel, halo, seq_len, tiles_per_image),
        grid=(n_tiles,),
        in_specs=in_specs,
        out_specs=pl.BlockSpec((tt, D_MODEL), tok),
        out_shape=jax.ShapeDtypeStruct((n_tok, D_MODEL), F32),
        scratch_shapes=scratch,
        compiler_params=_cparams(1),
        name="ffn",
    )(*args)


def _trunk(x, mod3, mod_row_of_batch, row_len, ffn_halo, init, want_state, tt_in, tt_ffn, cg, w):
    bsz, t, _ = x.shape
    x2d = x.reshape(bsz * t, D_MODEL)

    def mod_row(tile_tokens):
        return lambda i: mod_row_of_batch((i * tile_tokens) // t)

    ysc, qt, k, vt, sot, gcol, grow = _in_proj_call(x2d, mod3, mod_row(tt_in), row_len, tt_in, w)
    outs = _mlstm_call(bsz, t, cg, qt, k, vt, gcol, grow, init, want_state)
    x1, h2 = _out_proj_call(x2d, ysc, outs[0], outs[1], sot, mod3, mod_row(tt_in), tt_in, w)
    y = _ffn_call(h2, x1, mod3, mod_row(tt_ffn), tt_ffn, ffn_halo, t, w)
    return y.reshape(bsz, t, D_MODEL), outs[2:]


def kernel(x_prompt, x_sample, state_C, state_n, state_m, c, c_ctx, w_mod, b_mod, norm1, w_in, b_gate,
           conv_sc_w, conv_sc_b, mh_norm, w_out, norm2, w_up, conv_ffn_w, conv_ffn_b, w_down, final_norm):
    n_lat = c.shape[0]
    n_ctx = x_prompt.shape[0]
    c8 = jnp.concatenate([c, c_ctx[None], jnp.zeros((8 - n_lat - 1, D_MODEL), F32)], axis=0)
    l = 0
    mod3 = _mod_call(c8, w_mod[l], b_mod[l][None]).reshape(8, 6, D_MODEL)

    wi = w_in[l]
    q0 = 3 * D_CONV
    g0 = q0 + 4 * D_MLSTM
    zpad = jnp.zeros((D_MODEL, LANES - N_CHAINS), F32)
    cols = lambda a: wi[:, g0 + a * N_HEADS:g0 + (a + 1) * N_HEADS]
    w_g = jnp.concatenate([cols(0), cols(2), zpad, cols(1), cols(3), zpad], axis=1).astype(BF16)
    bpad = jnp.zeros((LANES - N_CHAINS,), F32)
    bg = b_gate[l].astype(F32)
    b_g = jnp.concatenate([bg[0], bg[2], bpad, bg[1], bg[3], bpad])[None]
    w = dict(
        norm1=norm1[l][None], norm2=norm2[l][None], final_norm=final_norm[None],
        w_sc=wi[:, 0:q0].astype(BF16),
        w_k=wi[:, q0 + D_MLSTM:q0 + 2 * D_MLSTM].astype(BF16),
        w_qvot=jnp.concatenate([wi[:, q0:q0 + D_MLSTM], wi[:, q0 + 2 * D_MLSTM:g0]], axis=1).T.astype(BF16),
        w_g=w_g, b_g=b_g,
        conv_sc_w=conv_sc_w[l], conv_sc_b=conv_sc_b[l][None],
        mh_norm_col=mh_norm[l][:, None],
        w_out_a=w_out[l][0:D_CONV].astype(BF16), w_out_b=w_out[l][D_CONV:].astype(BF16),
        w_up=w_up[l].astype(BF16), conv_ffn_w=conv_ffn_w[l], conv_ffn_b=conv_ffn_b[l][None],
        w_down=w_down[l].astype(BF16),
    )

    seq = x_prompt.shape[1]
    y_prompt, (new_c, new_n, new_m) = _trunk(
        x_prompt, mod3, lambda b: n_lat, seq, 0, None, True, 512, 1024, 1, w)

    init = (state_C[:, l], state_n[:, l][:, :, :, None, :],
            jnp.broadcast_to(state_m[:, l][:, :, :, None, None], state_m[:, l].shape + (1, LANES)))
    y_sample, _ = _trunk(x_sample, mod3, lambda b: b, GRID_W, GRID_W, init, False, 512, 1024, 2, w)

    new_n = new_n.reshape(n_ctx, 1, 2, N_HEADS, HEAD_DIM)
    new_m = new_m[..., 0, 0]
    return y_prompt, y_sample, new_c, new_n, new_m
```

```python
import functools

import jax
import jax.numpy as jnp
from jax import lax
from jax.experimental import pallas as pl
from jax.experimental.pallas import tpu as pltpu

F32 = jnp.float32
BF16 = jnp.bfloat16

D_MODEL = 1024
GRID_W = 64
D_CONV = 512
D_MLSTM = 512
N_HEADS = 4
HEAD_DIM = 128
D_FF = 2816
CHUNK = 256
EPS = 1e-6

LANES = 128
SUBLANES = 8
BF16_ROWS = 16
FF_BLK = 256
N_FF_BLK = D_FF // FF_BLK
CONV_ROWS = 128
VMEM_LIMIT = 56 * 1024 * 1024
N_CHAINS = 2 * N_HEADS
E_ROWS = HEAD_DIM + BF16_ROWS


def _cparams(n_axes):
    return pltpu.CompilerParams(
        dimension_semantics=("arbitrary",) * n_axes, vmem_limit_bytes=VMEM_LIMIT)


def _rms_scale(x):
    return x * lax.rsqrt(jnp.mean(x * x, axis=-1, keepdims=True) + EPS)


def _dot(a, b):
    return jnp.dot(a, b, preferred_element_type=F32)


def _dot_nt(a, b):
    return lax.dot_general(a, b, (((1,), (1,)), ((), ())), preferred_element_type=F32)


def _dot_tn(a, b):
    return lax.dot_general(a, b, (((0,), (0,)), ((), ())), preferred_element_type=F32)


def _mod_kernel(c_ref, w_ref, b_ref, o_ref):
    c = c_ref[...]
    s = c * jax.nn.sigmoid(c)
    o_ref[...] = _dot(s.astype(BF16), w_ref[...].astype(BF16)) + b_ref[...]


def _mod_call(c8, w_mod, b_mod):
    n_out = w_mod.shape[1]
    blk = 1024
    return pl.pallas_call(
        _mod_kernel,
        grid=(n_out // blk,),
        in_specs=[
            pl.BlockSpec((8, D_MODEL), lambda i: (0, 0)),
            pl.BlockSpec((D_MODEL, blk), lambda i: (0, i)),
            pl.BlockSpec((1, blk), lambda i: (0, i)),
        ],
        out_specs=pl.BlockSpec((8, blk), lambda i: (0, i)),
        out_shape=jax.ShapeDtypeStruct((8, n_out), F32),
        compiler_params=_cparams(1),
        name="mod",
    )(c8, w_mod, b_mod)


def _chunk_scan(x, pos, op, fill, reverse):
    n = x.shape[0]
    k = 1
    while k < CHUNK:
        if reverse:
            shifted = pltpu.roll(x, n - k, axis=0)
            ok = pos < CHUNK - k
        else:
            shifted = pltpu.roll(x, k, axis=0)
            ok = pos >= k
        x = op(x, jnp.where(ok, shifted, fill))
        k *= 2
    return x


def _in_proj_kernel(row_len, x_ref, mod_ref, n1_ref, wsc_ref, wk_ref, wqvot_ref, wg_ref, bg_ref, cw_ref, cb_ref,
                    ysc_ref, qt_ref, k_ref, vt_ref, sot_ref, gcol_ref, grow_ref):
    tt = x_ref.shape[0]
    x = x_ref[...]
    mod = mod_ref[...]
    shift1, scale1 = mod[0:1], mod[1:2]
    h = (_rms_scale(x) * n1_ref[...]) * (1.0 + scale1) + shift1
    hb = h.astype(BF16)

    g = _dot(hb, wg_ref[...]) + bg_ref[...]
    gi = g[:, 0:LANES]
    gf = g[:, LANES:2 * LANES]
    logf = jnp.minimum(gf, 0.0) - jnp.log1p(jnp.exp(-jnp.abs(gf)))
    lane = lax.broadcasted_iota(jnp.int32, gi.shape, 1)
    cpos = lax.broadcasted_iota(jnp.int32, gi.shape, 0) % CHUNK
    fwd = lane < N_HEADS
    b = jnp.where(fwd, _chunk_scan(logf, cpos, jnp.add, 0.0, False),
                  _chunk_scan(logf, cpos, jnp.add, 0.0, True))
    r = gi - b
    cm = jnp.where(fwd, _chunk_scan(r, cpos, jnp.maximum, -jnp.inf, False),
                   _chunk_scan(r, cpos, jnp.maximum, -jnp.inf, True))
    packed = jnp.where(lane < N_CHAINS, cm,
                       jnp.where(lane < 2 * N_CHAINS, pltpu.roll(r, N_CHAINS, axis=1),
                                 pltpu.roll(b, 2 * N_CHAINS, axis=1)))
    gcol_ref[...] = packed
    for c in range(tt // CHUNK):
        grow_ref[c] = packed[c * CHUNK:(c + 1) * CHUNK].T[0:3 * N_CHAINS]

    zb = _dot(hb, wsc_ref[:, 0:D_CONV])
    zc = _dot(hb, wsc_ref[:, D_CONV:2 * D_CONV])
    zx = _dot(hb, wsc_ref[:, 2 * D_CONV:3 * D_CONV])
    u = zc * zx
    pos = lax.broadcasted_iota(jnp.int32, u.shape, 0) % row_len
    prev = jnp.where(pos == 0, 0.0, pltpu.roll(u, 1, axis=0))
    nxt = jnp.where(pos == row_len - 1, 0.0, pltpu.roll(u, tt - 1, axis=0))
    cw = cw_ref[...]
    conv = cw[0:1] * prev + cw[1:2] * u + cw[2:3] * nxt + cb_ref[...]
    ysc_ref[...] = (zb * conv).astype(BF16)

    k_ref[...] = _dot(hb, wk_ref[...]).astype(BF16)
    qvo_t = _dot_nt(wqvot_ref[...], hb)
    qt_ref[...] = (qvo_t[0:D_MLSTM] * (HEAD_DIM ** -0.5)).astype(BF16)
    vt_ref[...] = qvo_t[D_MLSTM:2 * D_MLSTM].astype(BF16)
    sot_ref[...] = jax.nn.sigmoid(qvo_t[2 * D_MLSTM:3 * D_MLSTM])


def _in_proj_call(x2d, mod3, mod_row, row_len, tt, w):
    n_tok = x2d.shape[0]
    n_tiles = n_tok // tt
    const = lambda i: (0, 0)
    tok = lambda i: (i, 0)
    tok_t = lambda i: (0, i)
    weights = [w["norm1"], w["w_sc"], w["w_k"], w["w_qvot"], w["w_g"], w["b_g"], w["conv_sc_w"], w["conv_sc_b"]]
    out_shapes = (
        jax.ShapeDtypeStruct((n_tok, D_CONV), BF16),
        jax.ShapeDtypeStruct((D_MLSTM, n_tok), BF16),
        jax.ShapeDtypeStruct((n_tok, D_MLSTM), BF16),
        jax.ShapeDtypeStruct((D_MLSTM, n_tok), BF16),
        jax.ShapeDtypeStruct((D_MLSTM, n_tok), F32),
        jax.ShapeDtypeStruct((n_tok, LANES), F32),
        jax.ShapeDtypeStruct((n_tok // CHUNK, 3 * N_CHAINS, CHUNK), F32),
    )
    return pl.pallas_call(
        functools.partial(_in_proj_kernel, row_len),
        grid=(n_tiles,),
        in_specs=[
            pl.BlockSpec((tt, D_MODEL), tok),
            pl.BlockSpec((None, 6, D_MODEL), lambda i: (mod_row(i), 0, 0)),
        ] + [pl.BlockSpec(a.shape, const) for a in weights],
        out_specs=[
            pl.BlockSpec((tt, D_CONV), tok),
            pl.BlockSpec((D_MLSTM, tt), tok_t),
            pl.BlockSpec((tt, D_MLSTM), tok),
            pl.BlockSpec((D_MLSTM, tt), tok_t),
            pl.BlockSpec((D_MLSTM, tt), tok_t),
            pl.BlockSpec((tt, LANES), tok),
            pl.BlockSpec((tt // CHUNK, 3 * N_CHAINS, CHUNK), lambda i: (i, 0, 0)),
        ],
        out_shape=out_shapes,
        compiler_params=_cparams(1),
        name="in_proj",
    )(x2d, mod3, *weights)


def _mlstm_kernel(cg, has_init, want_state, *refs):
    refs = list(refs)
    fwd_refs, bwd_refs = refs[0:5], refs[5:10]
    refs = refs[10:]
    if has_init:
        c0_ref, n0_ref, m0_ref = refs[:3]
        refs = refs[3:]
    htf_ref, htb_ref = refs[:2]
    refs = refs[2:]
    if want_state:
        cout_ref, nout_ref, mout_ref = refs[:3]
        refs = refs[3:]
    e_ref, m_ref = refs

    g = pl.program_id(1)

    @pl.when(g == 0)
    def _():
        if has_init:
            for d in range(2):
                for hd in range(N_HEADS):
                    j = d * N_HEADS + hd
                    e_ref[j, 0:HEAD_DIM] = c0_ref[d, hd]
                    e_ref[j, HEAD_DIM:] = jnp.broadcast_to(n0_ref[d, hd], (BF16_ROWS, HEAD_DIM))
                    m_ref[j] = jnp.broadcast_to(m0_ref[d, hd], (SUBLANES, LANES))
        else:
            e_ref[...] = jnp.zeros(e_ref.shape, F32)
            m_ref[...] = jnp.zeros(m_ref.shape, F32)

    s_idx = lax.broadcasted_iota(jnp.int32, (CHUNK, CHUNK), 0)
    t_idx = lax.broadcasted_iota(jnp.int32, (CHUNK, CHUNK), 1)
    ones_rows = jnp.ones((BF16_ROWS, CHUNK), BF16)

    def chain_step(c, d, hd):
        qt_ref, k_ref, vt_ref, row_ref, col_ref = fwd_refs if d == 0 else bwd_refs
        ht_ref = htf_ref if d == 0 else htb_ref
        j = d * N_HEADS + hd
        toks = slice(c * CHUNK, (c + 1) * CHUNK)
        feat = slice(hd * HEAD_DIM, (hd + 1) * HEAD_DIM)
        last = CHUNK - 1 if d == 0 else 0
        mask = (s_idx <= t_idx) if d == 0 else (s_idx >= t_idx)

        qtc = qt_ref[feat, toks]
        kc = k_ref[toks, feat]
        vtc = vt_ref[feat, toks]
        cm_row = row_ref[c, j:j + 1, :]
        r_row = row_ref[c, N_CHAINS + j:N_CHAINS + j + 1, :]
        b_row = row_ref[c, 2 * N_CHAINS + j:2 * N_CHAINS + j + 1, :]
        r_col = col_ref[toks, N_CHAINS + j:N_CHAINS + j + 1]
        m_row = m_ref[j, 0:1, :]
        m_t = m_row[:, 0:1]

        big_m_row = jnp.maximum(m_t, cm_row)
        m_last = big_m_row[:, last:last + 1]

        w_t = jnp.exp(jnp.where(mask, r_col - big_m_row, -jnp.inf))
        e = e_ref[j]
        kq = _dot(jnp.concatenate([kc, e.astype(BF16)], axis=0), qtc)
        s_t = (kq[0:CHUNK] * w_t).astype(BF16)
        num_t = _dot(jnp.concatenate([vtc, ones_rows], axis=0), s_t)
        nd = num_t + jnp.exp(m_t - big_m_row) * kq[CHUNK:]
        inv = 1.0 / jnp.maximum(jnp.abs(nd[HEAD_DIM:HEAD_DIM + 1]), jnp.exp(-(b_row + big_m_row)))
        ht_ref[feat, toks] = nd[0:HEAD_DIM] * inv

        wg = jnp.exp(r_row - m_last)
        lhs = jnp.concatenate([(vtc.astype(F32) * wg).astype(BF16),
                               jnp.broadcast_to(wg, (BF16_ROWS, CHUNK)).astype(BF16)], axis=0)
        e_ref[j] = jnp.exp(m_row - m_last) * e + _dot(lhs, kc)
        m_ref[j] = jnp.broadcast_to(b_row[:, last:last + 1] + m_last, (SUBLANES, LANES))

    for c in range(cg):
        for d in range(2):
            for hd in range(N_HEADS):
                chain_step(c if d == 0 else cg - 1 - c, d, hd)

    if want_state:
        @pl.when(g == pl.num_programs(1) - 1)
        def _():
            for d in range(2):
                for hd in range(N_HEADS):
                    j = d * N_HEADS + hd
                    cout_ref[d, hd] = e_ref[j, 0:HEAD_DIM]
                    nout_ref[d, hd] = e_ref[j, HEAD_DIM:HEAD_DIM + 1]
                    mout_ref[d, hd] = m_ref[j, 0:1, :]


def _mlstm_call(bsz, t, cg, qt, k, vt, gcol, grow, init, want_state):
    n_tok = bsz * t
    tg = cg * CHUNK
    n_groups = t // tg
    fwd = lambda b, g: b * n_groups + g
    bwd = lambda b, g: b * n_groups + (n_groups - 1 - g)

    def stream_specs(pos):
        return [
            pl.BlockSpec((D_MLSTM, tg), lambda b, g: (0, pos(b, g))),
            pl.BlockSpec((tg, D_MLSTM), lambda b, g: (pos(b, g), 0)),
            pl.BlockSpec((D_MLSTM, tg), lambda b, g: (0, pos(b, g))),
            pl.BlockSpec((cg, 3 * N_CHAINS, CHUNK), lambda b, g: (pos(b, g), 0, 0)),
            pl.BlockSpec((tg, LANES), lambda b, g: (pos(b, g), 0)),
        ]

    in_specs = stream_specs(fwd) + stream_specs(bwd)
    args = [qt, k, vt, grow, gcol] * 2
    if init is not None:
        c0, n0, m0 = init
        in_specs += [
            pl.BlockSpec((None, 2, N_HEADS, HEAD_DIM, HEAD_DIM), lambda b, g: (b, 0, 0, 0, 0)),
            pl.BlockSpec((None, 2, N_HEADS, 1, HEAD_DIM), lambda b, g: (b, 0, 0, 0, 0)),
            pl.BlockSpec((None, 2, N_HEADS, 1, LANES), lambda b, g: (b, 0, 0, 0, 0)),
        ]
        args += [c0, n0, m0]
    out_specs = [
        pl.BlockSpec((D_MLSTM, tg), lambda b, g: (0, fwd(b, g))),
        pl.BlockSpec((D_MLSTM, tg), lambda b, g: (0, bwd(b, g))),
    ]
    out_shape = [jax.ShapeDtypeStruct((D_MLSTM, n_tok), F32)] * 2
    if want_state:
        out_specs += [
            pl.BlockSpec((None, None, 2, N_HEADS, HEAD_DIM, HEAD_DIM), lambda b, g: (b, 0, 0, 0, 0, 0)),
            pl.BlockSpec((None, None, 2, N_HEADS, 1, HEAD_DIM), lambda b, g: (b, 0, 0, 0, 0, 0)),
            pl.BlockSpec((None, None, 2, N_HEADS, 1, LANES), lambda b, g: (b, 0, 0, 0, 0, 0)),
        ]
        out_shape += [
            jax.ShapeDtypeStruct((bsz, 1, 2, N_HEADS, HEAD_DIM, HEAD_DIM), F32),
            jax.ShapeDtypeStruct((bsz, 1, 2, N_HEADS, 1, HEAD_DIM), F32),
            jax.ShapeDtypeStruct((bsz, 1, 2, N_HEADS, 1, LANES), F32),
        ]
    return pl.pallas_call(
        functools.partial(_mlstm_kernel, cg, init is not None, want_state),
        grid=(bsz, n_groups),
        in_specs=in_specs,
        out_specs=out_specs,
        out_shape=out_shape,
        scratch_shapes=[
            pltpu.VMEM((N_CHAINS, E_ROWS, HEAD_DIM), F32),
            pltpu.VMEM((N_CHAINS, SUBLANES, LANES), F32),
        ],
        compiler_params=_cparams(2),
        name="mlstm",
    )(*args)


def _out_proj_kernel(x_ref, ysc_ref, htf_ref, htb_ref, sot_ref, mh_ref, mod_ref, n2_ref, wa_ref, wb_ref,
                     x1_ref, h2_ref):
    mod = mod_ref[...]
    gate1, shift2, scale2 = mod[2:3], mod[3:4], mod[4:5]
    hs = htf_ref[...] + htb_ref[...]
    heads = []
    for hd in range(N_HEADS):
        blk = hs[hd * HEAD_DIM:(hd + 1) * HEAD_DIM]
        heads.append(blk * lax.rsqrt(jnp.mean(blk * blk, axis=0, keepdims=True) + EPS))
    hm_t = ((jnp.concatenate(heads, axis=0) * mh_ref[...]) * sot_ref[...]).astype(BF16)
    mix = _dot(ysc_ref[...], wa_ref[...]) + _dot_tn(hm_t, wb_ref[...])
    x1 = x_ref[...] + gate1 * mix
    x1_ref[...] = x1
    h2_ref[...] = ((_rms_scale(x1) * n2_ref[...]) * (1.0 + scale2) + shift2).astype(BF16)


def _out_proj_call(x2d, ysc, htf, htb, sot, mod3, mod_row, tt, w):
    n_tok = x2d.shape[0]
    const = lambda i: (0, 0)
    tok = lambda i: (i, 0)
    tok_t = lambda i: (0, i)
    return pl.pallas_call(
        _out_proj_kernel,
        grid=(n_tok // tt,),
        in_specs=[
            pl.BlockSpec((tt, D_MODEL), tok),
            pl.BlockSpec((tt, D_CONV), tok),
            pl.BlockSpec((D_MLSTM, tt), tok_t),
            pl.BlockSpec((D_MLSTM, tt), tok_t),
            pl.BlockSpec((D_MLSTM, tt), tok_t),
            pl.BlockSpec((D_MLSTM, 1), const),
            pl.BlockSpec((None, 6, D_MODEL), lambda i: (mod_row(i), 0, 0)),
            pl.BlockSpec((1, D_MODEL), const),
            pl.BlockSpec(w["w_out_a"].shape, const),
            pl.BlockSpec(w["w_out_b"].shape, const),
        ],
        out_specs=[pl.BlockSpec((tt, D_MODEL), tok), pl.BlockSpec((tt, D_MODEL), tok)],
        out_shape=[jax.ShapeDtypeStruct((n_tok, D_MODEL), F32),
                   jax.ShapeDtypeStruct((n_tok, D_MODEL), BF16)],
        compiler_params=_cparams(1),
        name="out_proj",
    )(x2d, ysc, htf, htb, sot, w["mh_norm_col"], mod3, w["norm2"], w["w_out_a"], w["w_out_b"])


def _ffn_kernel(halo, seq_len, tiles_per_image, *refs):
    if halo:
        h2_ref, prev_ref, next_ref = refs[:3]
        refs = refs[3:]
    else:
        h2_ref = refs[0]
        refs = refs[1:]
    (x1_ref, mod_ref, fn_ref, wu_ref, cw_ref, cb_ref, wd_ref, y_ref,
     act_ref, a0_ref, a1_ref, g0_ref, g1_ref, h2e_ref) = refs
    slots = ((a0_ref, g0_ref), (a1_ref, g1_ref))
    tt = x1_ref.shape[0]
    i = pl.program_id(0)
    last_blk = N_FF_BLK - 1
    pad = halo if halo else SUBLANES
    shift = halo if halo else 1
    nr = CONV_ROWS
    row = lax.broadcasted_iota(jnp.int32, (nr, FF_BLK), 0)

    def up_proj(blk, slot):
        a_ref, g_ref = slots[slot]
        off_a = pl.multiple_of(blk * FF_BLK, FF_BLK)
        off_g = pl.multiple_of(blk * FF_BLK + D_FF, FF_BLK)
        hin = h2e_ref[...]
        rows = slice(0, tt + 2 * halo) if halo else slice(pad, pad + tt)
        a_ref[rows] = _dot(hin, wu_ref[:, pl.ds(off_a, FF_BLK)])
        g_ref[rows] = _dot(hin, wu_ref[:, pl.ds(off_g, FF_BLK)])

    def conv_act(blk, slot):
        a_ref, g_ref = slots[slot]
        off_a = pl.multiple_of(blk * FF_BLK, FF_BLK)
        off_g = pl.multiple_of(blk * FF_BLK + D_FF, FF_BLK)

        def conv(z_ref, off, r0):
            cw = cw_ref[:, pl.ds(off, FF_BLK)]
            cb = cb_ref[:, pl.ds(off, FF_BLK)]
            prev = z_ref[pad - shift + r0:pad - shift + r0 + nr]
            nxt = z_ref[pad + shift + r0:pad + shift + r0 + nr]
            if not halo:
                if r0 % seq_len == 0:
                    prev = jnp.where(row == 0, 0.0, prev)
                if (r0 + nr) % seq_len == 0:
                    nxt = jnp.where(row == nr - 1, 0.0, nxt)
            return cw[0:1] * prev + cw[1:2] * z_ref[pad + r0:pad + r0 + nr] + cw[2:3] * nxt + cb

        for r0 in range(0, tt, nr):
            ac = conv(a_ref, off_a, r0)
            gc = conv(g_ref, off_g, r0)
            act_ref[r0:r0 + nr, pl.ds(off_a, FF_BLK)] = (gc * jax.nn.sigmoid(gc) * ac).astype(BF16)

    h2e_ref[halo:halo + tt] = h2_ref[...]
    if halo:
        h2e_ref[0:halo] = prev_ref[...]
        h2e_ref[halo + tt:] = next_ref[...]

        @pl.when(i % tiles_per_image == 0)
        def _():
            h2e_ref[0:halo] = jnp.zeros((halo, D_MODEL), BF16)

        @pl.when(i % tiles_per_image == tiles_per_image - 1)
        def _():
            h2e_ref[halo + tt:] = jnp.zeros((halo, D_MODEL), BF16)
    else:
        for z_ref in (a0_ref, a1_ref, g0_ref, g1_ref):
            z_ref[0:pad] = jnp.zeros((pad, FF_BLK), F32)
            z_ref[pad + tt:] = jnp.zeros((pad, FF_BLK), F32)

    up_proj(0, 0)

    def block_pair(p, carry):
        blk = 2 * p + 1
        conv_act(blk - 1, 0)
        up_proj(blk, 1)
        conv_act(blk, 1)
        up_proj(blk + 1, 0)
        return carry

    lax.fori_loop(0, last_blk // 2, block_pair, 0)

    k0 = last_blk * FF_BLK
    part = _dot(act_ref[:, 0:k0], wd_ref[0:k0, :])
    conv_act(last_blk, 0)
    ffn = part + _dot(act_ref[:, k0:], wd_ref[k0:, :])
    x2 = x1_ref[...] + mod_ref[5:6] * ffn
    y_ref[...] = _rms_scale(x2) * fn_ref[...]


def _ffn_call(h2, x1, mod3, mod_row, tt, halo, seq_len, w):
    n_tok = x1.shape[0]
    n_tiles = n_tok // tt
    tok = lambda i: (i, 0)
    const = lambda i: (0, 0)
    resident = dict(pipeline_mode=pl.Buffered(1))
    in_specs = [pl.BlockSpec((tt, D_MODEL), tok)]
    args = [h2]
    up_rows = tt + 2 * (halo if halo else SUBLANES)
    scratch = [pltpu.VMEM((tt, D_FF), BF16)] + [pltpu.VMEM((up_rows, FF_BLK), F32)] * 4
    tiles_per_image = 1
    if halo:
        tiles_per_image = seq_len // tt
        per = tt // halo
        n_halo_blk = n_tok // halo
        in_specs += [
            pl.BlockSpec((halo, D_MODEL), lambda i: (jnp.maximum(i * per - 1, 0), 0)),
            pl.BlockSpec((halo, D_MODEL), lambda i: (jnp.minimum((i + 1) * per, n_halo_blk - 1), 0)),
        ]
        args += [h2, h2]
    scratch.append(pltpu.VMEM((tt + 2 * halo, D_MODEL), BF16))
    in_specs += [
        pl.BlockSpec((tt, D_MODEL), tok),
        pl.BlockSpec((None, 6, D_MODEL), lambda i: (mod_row(i), 0, 0)),
        pl.BlockSpec((1, D_MODEL), const),
        pl.BlockSpec((D_MODEL, 2 * D_FF), const, **resident),
        pl.BlockSpec((3, 2 * D_FF), const),
        pl.BlockSpec((1, 2 * D_FF), const),
        pl.BlockSpec((D_FF, D_MODEL), const, **resident),
    ]
    args += [x1, mod3, w["final_norm"], w["w_up"], w["conv_ffn_w"], w["conv_ffn_b"], w["w_down"]]
    return pl.pallas_call(
        functools.partial(_ffn_kernel, halo, seq_len, tiles_per_image),
        grid=(n_tiles,),
        in_specs=in_specs,
        out_specs=pl.BlockSpec((tt, D_MODEL), tok),
        out_shape=jax.ShapeDtypeStruct((n_tok, D_MODEL), F32),
        scratch_shapes=scratch,
        compiler_params=_cparams(1),
        name="ffn",
    )(*args)


def _trunk(x, mod3, mod_row_of_batch, row_len, ffn_halo, init, want_state, tt_in, tt_ffn, cg, w):
    bsz, t, _ = x.shape
    x2d = x.reshape(bsz * t, D_MODEL)

    def mod_row(tile_tokens):
        return lambda i: mod_row_of_batch((i * tile_tokens) // t)

    ysc, qt, k, vt, sot, gcol, grow = _in_proj_call(x2d, mod3, mod_row(tt_in), row_len, tt_in, w)
    outs = _mlstm_call(bsz, t, cg, qt, k, vt, gcol, grow, init, want_state)
    x1, h2 = _out_proj_call(x2d, ysc, outs[0], outs[1], sot, mod3, mod_row(tt_in), tt_in, w)
    y = _ffn_call(h2, x1, mod3, mod_row(tt_ffn), tt_ffn, ffn_halo, t, w)
    return y.reshape(bsz, t, D_MODEL), outs[2:]


def kernel(x_prompt, x_sample, state_C, state_n, state_m, c, c_ctx, w_mod, b_mod, norm1, w_in, b_gate,
           conv_sc_w, conv_sc_b, mh_norm, w_out, norm2, w_up, conv_ffn_w, conv_ffn_b, w_down, final_norm):
    n_lat = c.shape[0]
    n_ctx = x_prompt.shape[0]
    c8 = jnp.concatenate([c, c_ctx[None], jnp.zeros((8 - n_lat - 1, D_MODEL), F32)], axis=0)
    l = 0
    mod3 = _mod_call(c8, w_mod[l], b_mod[l][None]).reshape(8, 6, D_MODEL)

    wi = w_in[l]
    q0 = 3 * D_CONV
    g0 = q0 + 4 * D_MLSTM
    zpad = jnp.zeros((D_MODEL, LANES - N_CHAINS), F32)
    cols = lambda a: wi[:, g0 + a * N_HEADS:g0 + (a + 1) * N_HEADS]
    w_g = jnp.concatenate([cols(0), cols(2), zpad, cols(1), cols(3), zpad], axis=1).astype(BF16)
    bpad = jnp.zeros((LANES - N_CHAINS,), F32)
    bg = b_gate[l].astype(F32)
    b_g = jnp.concatenate([bg[0], bg[2], bpad, bg[1], bg[3], bpad])[None]
    w = dict(
        norm1=norm1[l][None], norm2=norm2[l][None], final_norm=final_norm[None],
        w_sc=wi[:, 0:q0].astype(BF16),
        w_k=wi[:, q0 + D_MLSTM:q0 + 2 * D_MLSTM].astype(BF16),
        w_qvot=jnp.concatenate([wi[:, q0:q0 + D_MLSTM], wi[:, q0 + 2 * D_MLSTM:g0]], axis=1).T.astype(BF16),
        w_g=w_g, b_g=b_g,
        conv_sc_w=conv_sc_w[l], conv_sc_b=conv_sc_b[l][None],
        mh_norm_col=mh_norm[l][:, None],
        w_out_a=w_out[l][0:D_CONV].astype(BF16), w_out_b=w_out[l][D_CONV:].astype(BF16),
        w_up=w_up[l].astype(BF16), conv_ffn_w=conv_ffn_w[l], conv_ffn_b=conv_ffn_b[l][None],
        w_down=w_down[l].astype(BF16),
    )

    seq = x_prompt.shape[1]
    y_prompt, (new_c, new_n, new_m) = _trunk(
        x_prompt, mod3, lambda b: n_lat, seq, 0, None, True, 512, 1024, 1, w)

    init = (state_C[:, l], state_n[:, l][:, :, :, None, :],
            jnp.broadcast_to(state_m[:, l][:, :, :, None, None], state_m[:, l].shape + (1, LANES)))
    y_sample, _ = _trunk(x_sample, mod3, lambda b: b, GRID_W, GRID_W, init, False, 512, 1024, 2, w)

    new_n = new_n.reshape(n_ctx, 1, 2, N_HEADS, HEAD_DIM)
    new_m = new_m[..., 0, 0]
    return y_prompt, y_sample, new_c, new_n, new_m
```

```python
import functools

import jax
import jax.numpy as jnp
from jax import lax
from jax.experimental import pallas as pl
from jax.experimental.pallas import tpu as pltpu

F32 = jnp.float32
BF16 = jnp.bfloat16

D_MODEL = 1024
GRID_W = 64
D_CONV = 512
D_MLSTM = 512
N_HEADS = 4
HEAD_DIM = 128
D_FF = 2816
CHUNK = 256
EPS = 1e-6

LANES = 128
SUBLANES = 8
BF16_ROWS = 16
FF_BLK = 256
N_FF_BLK = D_FF // FF_BLK
CONV_ROWS = 128
VMEM_LIMIT = 56 * 1024 * 1024
N_CHAINS = 2 * N_HEADS
E_ROWS = HEAD_DIM + BF16_ROWS
SKEW = 3


def _cparams(n_axes):
    return pltpu.CompilerParams(
        dimension_semantics=("arbitrary",) * n_axes, vmem_limit_bytes=VMEM_LIMIT)


def _rms_scale(x):
    return x * lax.rsqrt(jnp.mean(x * x, axis=-1, keepdims=True) + EPS)


def _dot(a, b):
    return jnp.dot(a, b, preferred_element_type=F32)


def _dot_nt(a, b):
    return lax.dot_general(a, b, (((1,), (1,)), ((), ())), preferred_element_type=F32)


def _dot_tn(a, b):
    return lax.dot_general(a, b, (((0,), (0,)), ((), ())), preferred_element_type=F32)


def _mod_kernel(c_ref, w_ref, b_ref, o_ref):
    c = c_ref[...]
    s = c * jax.nn.sigmoid(c)
    o_ref[...] = _dot(s.astype(BF16), w_ref[...].astype(BF16)) + b_ref[...]


def _mod_call(c8, w_mod, b_mod):
    n_out = w_mod.shape[1]
    blk = 1024
    return pl.pallas_call(
        _mod_kernel,
        grid=(n_out // blk,),
        in_specs=[
            pl.BlockSpec((8, D_MODEL), lambda i: (0, 0)),
            pl.BlockSpec((D_MODEL, blk), lambda i: (0, i)),
            pl.BlockSpec((1, blk), lambda i: (0, i)),
        ],
        out_specs=pl.BlockSpec((8, blk), lambda i: (0, i)),
        out_shape=jax.ShapeDtypeStruct((8, n_out), F32),
        compiler_params=_cparams(1),
        name="mod",
    )(c8, w_mod, b_mod)


def _chunk_scan(x, pos, op, fill, reverse):
    n = x.shape[0]
    k = 1
    while k < CHUNK:
        if reverse:
            shifted = pltpu.roll(x, n - k, axis=0)
            ok = pos < CHUNK - k
        else:
            shifted = pltpu.roll(x, k, axis=0)
            ok = pos >= k
        x = op(x, jnp.where(ok, shifted, fill))
        k *= 2
    return x


def _in_proj_kernel(row_len, x_ref, mod_ref, n1_ref, wsc_ref, wk_ref, wqvot_ref, wg_ref, bg_ref, cw_ref, cb_ref,
                    ysc_ref, qt_ref, k_ref, vt_ref, sot_ref, gcol_ref, grow_ref):
    tt = x_ref.shape[0]
    x = x_ref[...]
    mod = mod_ref[...]
    shift1, scale1 = mod[0:1], mod[1:2]
    h = (_rms_scale(x) * n1_ref[...]) * (1.0 + scale1) + shift1
    hb = h.astype(BF16)

    g = _dot(hb, wg_ref[...]) + bg_ref[...]
    gi = g[:, 0:LANES]
    gf = g[:, LANES:2 * LANES]
    logf = jnp.minimum(gf, 0.0) - jnp.log1p(jnp.exp(-jnp.abs(gf)))
    lane = lax.broadcasted_iota(jnp.int32, gi.shape, 1)
    cpos = lax.broadcasted_iota(jnp.int32, gi.shape, 0) % CHUNK
    fwd = lane < N_HEADS
    b = jnp.where(fwd, _chunk_scan(logf, cpos, jnp.add, 0.0, False),
                  _chunk_scan(logf, cpos, jnp.add, 0.0, True))
    r = gi - b
    cm = jnp.where(fwd, _chunk_scan(r, cpos, jnp.maximum, -jnp.inf, False),
                   _chunk_scan(r, cpos, jnp.maximum, -jnp.inf, True))
    packed = jnp.where(lane < N_CHAINS, cm,
                       jnp.where(lane < 2 * N_CHAINS, pltpu.roll(r, N_CHAINS, axis=1),
                                 pltpu.roll(b, 2 * N_CHAINS, axis=1)))
    gcol_ref[...] = packed
    for c in range(tt // CHUNK):
        grow_ref[c] = packed[c * CHUNK:(c + 1) * CHUNK].T[0:3 * N_CHAINS]

    zb = _dot(hb, wsc_ref[:, 0:D_CONV])
    zc = _dot(hb, wsc_ref[:, D_CONV:2 * D_CONV])
    zx = _dot(hb, wsc_ref[:, 2 * D_CONV:3 * D_CONV])
    u = zc * zx
    pos = lax.broadcasted_iota(jnp.int32, u.shape, 0) % row_len
    prev = jnp.where(pos == 0, 0.0, pltpu.roll(u, 1, axis=0))
    nxt = jnp.where(pos == row_len - 1, 0.0, pltpu.roll(u, tt - 1, axis=0))
    cw = cw_ref[...]
    conv = cw[0:1] * prev + cw[1:2] * u + cw[2:3] * nxt + cb_ref[...]
    ysc_ref[...] = (zb * conv).astype(BF16)

    k_ref[...] = _dot(hb, wk_ref[...]).astype(BF16)
    qvo_t = _dot_nt(wqvot_ref[...], hb)
    qt_ref[...] = (qvo_t[0:D_MLSTM] * (HEAD_DIM ** -0.5)).astype(BF16)
    vt_ref[...] = qvo_t[D_MLSTM:2 * D_MLSTM].astype(BF16)
    sot_ref[...] = jax.nn.sigmoid(qvo_t[2 * D_MLSTM:3 * D_MLSTM])


def _in_proj_call(x2d, mod3, mod_row, row_len, tt, w):
    n_tok = x2d.shape[0]
    n_tiles = n_tok // tt
    const = lambda i: (0, 0)
    tok = lambda i: (i, 0)
    tok_t = lambda i: (0, i)
    weights = [w["norm1"], w["w_sc"], w["w_k"], w["w_qvot"], w["w_g"], w["b_g"], w["conv_sc_w"], w["conv_sc_b"]]
    out_shapes = (
        jax.ShapeDtypeStruct((n_tok, D_CONV), BF16),
        jax.ShapeDtypeStruct((D_MLSTM, n_tok), BF16),
        jax.ShapeDtypeStruct((n_tok, D_MLSTM), BF16),
        jax.ShapeDtypeStruct((D_MLSTM, n_tok), BF16),
        jax.ShapeDtypeStruct((D_MLSTM, n_tok), F32),
        jax.ShapeDtypeStruct((n_tok, LANES), F32),
        jax.ShapeDtypeStruct((n_tok // CHUNK, 3 * N_CHAINS, CHUNK), F32),
    )
    return pl.pallas_call(
        functools.partial(_in_proj_kernel, row_len),
        grid=(n_tiles,),
        in_specs=[
            pl.BlockSpec((tt, D_MODEL), tok),
            pl.BlockSpec((None, 6, D_MODEL), lambda i: (mod_row(i), 0, 0)),
        ] + [pl.BlockSpec(a.shape, const) for a in weights],
        out_specs=[
            pl.BlockSpec((tt, D_CONV), tok),
            pl.BlockSpec((D_MLSTM, tt), tok_t),
            pl.BlockSpec((tt, D_MLSTM), tok),
            pl.BlockSpec((D_MLSTM, tt), tok_t),
            pl.BlockSpec((D_MLSTM, tt), tok_t),
            pl.BlockSpec((tt, LANES), tok),
            pl.BlockSpec((tt // CHUNK, 3 * N_CHAINS, CHUNK), lambda i: (i, 0, 0)),
        ],
        out_shape=out_shapes,
        compiler_params=_cparams(1),
        name="in_proj",
    )(x2d, mod3, *weights)


def _mlstm_kernel(cg, has_init, want_state, *refs):
    refs = list(refs)
    fwd_refs, bwd_refs = refs[0:5], refs[5:10]
    refs = refs[10:]
    if has_init:
        c0_ref, n0_ref, m0_ref = refs[:3]
        refs = refs[3:]
    htf_ref, htb_ref = refs[:2]
    refs = refs[2:]
    if want_state:
        cout_ref, nout_ref, mout_ref = refs[:3]
        refs = refs[3:]
    e_ref, m_ref = refs

    g = pl.program_id(1)

    @pl.when(g == 0)
    def _():
        if has_init:
            for d in range(2):
                for hd in range(N_HEADS):
                    j = d * N_HEADS + hd
                    e_ref[j, 0:HEAD_DIM] = c0_ref[d, hd]
                    e_ref[j, HEAD_DIM:] = jnp.broadcast_to(n0_ref[d, hd], (BF16_ROWS, HEAD_DIM))
                    m_ref[j] = jnp.broadcast_to(m0_ref[d, hd], (SUBLANES, LANES))
        else:
            e_ref[...] = jnp.zeros(e_ref.shape, F32)
            m_ref[...] = jnp.zeros(m_ref.shape, F32)

    s_idx = lax.broadcasted_iota(jnp.int32, (CHUNK, CHUNK), 0)
    t_idx = lax.broadcasted_iota(jnp.int32, (CHUNK, CHUNK), 1)
    ones_rows = jnp.ones((BF16_ROWS, CHUNK), BF16)

    def stage_scores(c, d, hd):
        qt_ref, k_ref, vt_ref, row_ref, col_ref = fwd_refs if d == 0 else bwd_refs
        j = d * N_HEADS + hd
        toks = slice(c * CHUNK, (c + 1) * CHUNK)
        feat = slice(hd * HEAD_DIM, (hd + 1) * HEAD_DIM)
        last = CHUNK - 1 if d == 0 else 0

        qtc = qt_ref[feat, toks]
        kc = k_ref[toks, feat]
        vtc = vt_ref[feat, toks]
        cm_row = row_ref[c, j:j + 1, :]
        r_row = row_ref[c, N_CHAINS + j:N_CHAINS + j + 1, :]
        b_row = row_ref[c, 2 * N_CHAINS + j:2 * N_CHAINS + j + 1, :]
        m_row = m_ref[j, 0:1, :]
        m_t = m_row[:, 0:1]
        big_m_row = jnp.maximum(m_t, cm_row)
        m_last = big_m_row[:, last:last + 1]

        e = e_ref[j]
        kq = _dot(jnp.concatenate([kc, e.astype(BF16)], axis=0), qtc)

        wg = jnp.exp(r_row - m_last)
        lhs = jnp.concatenate([(vtc.astype(F32) * wg).astype(BF16),
                               jnp.broadcast_to(wg, (BF16_ROWS, CHUNK)).astype(BF16)], axis=0)
        e_ref[j] = jnp.exp(m_row - m_last) * e + _dot(lhs, kc)
        m_ref[j] = jnp.broadcast_to(b_row[:, last:last + 1] + m_last, (SUBLANES, LANES))
        return kq, vtc, big_m_row, m_t, b_row

    def stage_output(c, d, hd, kq, vtc, big_m_row, m_t, b_row):
        col_ref = (fwd_refs if d == 0 else bwd_refs)[4]
        ht_ref = htf_ref if d == 0 else htb_ref
        j = d * N_HEADS + hd
        toks = slice(c * CHUNK, (c + 1) * CHUNK)
        feat = slice(hd * HEAD_DIM, (hd + 1) * HEAD_DIM)
        mask = (s_idx <= t_idx) if d == 0 else (s_idx >= t_idx)
        r_col = col_ref[toks, N_CHAINS + j:N_CHAINS + j + 1]

        w_t = jnp.exp(jnp.where(mask, r_col - big_m_row, -jnp.inf))
        s_t = (kq[0:CHUNK] * w_t).astype(BF16)
        num_t = _dot(jnp.concatenate([vtc, ones_rows], axis=0), s_t)
        nd = num_t + jnp.exp(m_t - big_m_row) * kq[CHUNK:]
        inv = 1.0 / jnp.maximum(jnp.abs(nd[HEAD_DIM:HEAD_DIM + 1]), jnp.exp(-(b_row + big_m_row)))
        ht_ref[feat, toks] = nd[0:HEAD_DIM] * inv

    tasks = [(c if d == 0 else cg - 1 - c, d, hd) for c in range(cg) for d in range(2) for hd in range(N_HEADS)]
    pending = []
    for task in tasks:
        pending.append(task + stage_scores(*task))
        if len(pending) > SKEW:
            stage_output(*pending.pop(0))
    for args in pending:
        stage_output(*args)

    if want_state:
        @pl.when(g == pl.num_programs(1) - 1)
        def _():
            for d in range(2):
                for hd in range(N_HEADS):
                    j = d * N_HEADS + hd
                    cout_ref[d, hd] = e_ref[j, 0:HEAD_DIM]
                    nout_ref[d, hd] = e_ref[j, HEAD_DIM:HEAD_DIM + 1]
                    mout_ref[d, hd] = m_ref[j, 0:1, :]


def _mlstm_call(bsz, t, cg, qt, k, vt, gcol, grow, init, want_state):
    n_tok = bsz * t
    tg = cg * CHUNK
    n_groups = t // tg
    fwd = lambda b, g: b * n_groups + g
    bwd = lambda b, g: b * n_groups + (n_groups - 1 - g)

    def stream_specs(pos):
        return [
            pl.BlockSpec((D_MLSTM, tg), lambda b, g: (0, pos(b, g))),
            pl.BlockSpec((tg, D_MLSTM), lambda b, g: (pos(b, g), 0)),
            pl.BlockSpec((D_MLSTM, tg), lambda b, g: (0, pos(b, g))),
            pl.BlockSpec((cg, 3 * N_CHAINS, CHUNK), lambda b, g: (pos(b, g), 0, 0)),
            pl.BlockSpec((tg, LANES), lambda b, g: (pos(b, g), 0)),
        ]

    in_specs = stream_specs(fwd) + stream_specs(bwd)
    args = [qt, k, vt, grow, gcol] * 2
    if init is not None:
        c0, n0, m0 = init
        in_specs += [
            pl.BlockSpec((None, 2, N_HEADS, HEAD_DIM, HEAD_DIM), lambda b, g: (b, 0, 0, 0, 0)),
            pl.BlockSpec((None, 2, N_HEADS, 1, HEAD_DIM), lambda b, g: (b, 0, 0, 0, 0)),
            pl.BlockSpec((None, 2, N_HEADS, 1, LANES), lambda b, g: (b, 0, 0, 0, 0)),
        ]
        args += [c0, n0, m0]
    out_specs = [
        pl.BlockSpec((D_MLSTM, tg), lambda b, g: (0, fwd(b, g))),
        pl.BlockSpec((D_MLSTM, tg), lambda b, g: (0, bwd(b, g))),
    ]
    out_shape = [jax.ShapeDtypeStruct((D_MLSTM, n_tok), F32)] * 2
    if want_state:
        out_specs += [
            pl.BlockSpec((None, None, 2, N_HEADS, HEAD_DIM, HEAD_DIM), lambda b, g: (b, 0, 0, 0, 0, 0)),
            pl.BlockSpec((None, None, 2, N_HEADS, 1, HEAD_DIM), lambda b, g: (b, 0, 0, 0, 0, 0)),
            pl.BlockSpec((None, None, 2, N_HEADS, 1, LANES), lambda b, g: (b, 0, 0, 0, 0, 0)),
        ]
        out_shape += [
            jax.ShapeDtypeStruct((bsz, 1, 2, N_HEADS, HEAD_DIM, HEAD_DIM), F32),
            jax.ShapeDtypeStruct((bsz, 1, 2, N_HEADS, 1, HEAD_DIM), F32),
            jax.ShapeDtypeStruct((bsz, 1, 2, N_HEADS, 1, LANES), F32),
        ]
    return pl.pallas_call(
        functools.partial(_mlstm_kernel, cg, init is not None, want_state),
        grid=(bsz, n_groups),
        in_specs=in_specs,
        out_specs=out_specs,
        out_shape=out_shape,
        scratch_shapes=[
            pltpu.VMEM((N_CHAINS, E_ROWS, HEAD_DIM), F32),
            pltpu.VMEM((N_CHAINS, SUBLANES, LANES), F32),
        ],
        compiler_params=_cparams(2),
        name="mlstm",
    )(*args)


def _out_proj_kernel(x_ref, ysc_ref, htf_ref, htb_ref, sot_ref, mh_ref, mod_ref, n2_ref, wa_ref, wb_ref,
                     x1_ref, h2_ref):
    mod = mod_ref[...]
    gate1, shift2, scale2 = mod[2:3], mod[3:4], mod[4:5]
    hs = htf_ref[...] + htb_ref[...]
    heads = []
    for hd in range(N_HEADS):
        blk = hs[hd * HEAD_DIM:(hd + 1) * HEAD_DIM]
        heads.append(blk * lax.rsqrt(jnp.mean(blk * blk, axis=0, keepdims=True) + EPS))
    hm_t = ((jnp.concatenate(heads, axis=0) * mh_ref[...]) * sot_ref[...]).astype(BF16)
    mix = _dot(ysc_ref[...], wa_ref[...]) + _dot_tn(hm_t, wb_ref[...])
    x1 = x_ref[...] + gate1 * mix
    x1_ref[...] = x1
    h2_ref[...] = ((_rms_scale(x1) * n2_ref[...]) * (1.0 + scale2) + shift2).astype(BF16)


def _out_proj_call(x2d, ysc, htf, htb, sot, mod3, mod_row, tt, w):
    n_tok = x2d.shape[0]
    const = lambda i: (0, 0)
    tok = lambda i: (i, 0)
    tok_t = lambda i: (0, i)
    return pl.pallas_call(
        _out_proj_kernel,
        grid=(n_tok // tt,),
        in_specs=[
            pl.BlockSpec((tt, D_MODEL), tok),
            pl.BlockSpec((tt, D_CONV), tok),
            pl.BlockSpec((D_MLSTM, tt), tok_t),
            pl.BlockSpec((D_MLSTM, tt), tok_t),
            pl.BlockSpec((D_MLSTM, tt), tok_t),
            pl.BlockSpec((D_MLSTM, 1), const),
            pl.BlockSpec((None, 6, D_MODEL), lambda i: (mod_row(i), 0, 0)),
            pl.BlockSpec((1, D_MODEL), const),
            pl.BlockSpec(w["w_out_a"].shape, const),
            pl.BlockSpec(w["w_out_b"].shape, const),
        ],
        out_specs=[pl.BlockSpec((tt, D_MODEL), tok), pl.BlockSpec((tt, D_MODEL), tok)],
        out_shape=[jax.ShapeDtypeStruct((n_tok, D_MODEL), F32),
                   jax.ShapeDtypeStruct((n_tok, D_MODEL), BF16)],
        compiler_params=_cparams(1),
        name="out_proj",
    )(x2d, ysc, htf, htb, sot, w["mh_norm_col"], mod3, w["norm2"], w["w_out_a"], w["w_out_b"])


def _ffn_kernel(halo, seq_len, tiles_per_image, *refs):
    if halo:
        h2_ref, prev_ref, next_ref = refs[:3]
        refs = refs[3:]
    else:
        h2_ref = refs[0]
        refs = refs[1:]
    (x1_ref, mod_ref, fn_ref, wu_ref, cw_ref, cb_ref, wd_ref, y_ref,
     act_ref, a0_ref, a1_ref, g0_ref, g1_ref, h2e_ref) = refs
    slots = ((a0_ref, g0_ref), (a1_ref, g1_ref))
    tt = x1_ref.shape[0]
    i = pl.program_id(0)
    last_blk = N_FF_BLK - 1
    pad = halo if halo else SUBLANES
    shift = halo if halo else 1
    nr = CONV_ROWS
    row = lax.broadcasted_iota(jnp.int32, (nr, FF_BLK), 0)

    def up_proj(blk, slot):
        a_ref, g_ref = slots[slot]
        off_a = pl.multiple_of(blk * FF_BLK, FF_BLK)
        off_g = pl.multiple_of(blk * FF_BLK + D_FF, FF_BLK)
        hin = h2e_ref[...]
        rows = slice(0, tt + 2 * halo) if halo else slice(pad, pad + tt)
        a_ref[rows] = _dot(hin, wu_ref[:, pl.ds(off_a, FF_BLK)])
        g_ref[rows] = _dot(hin, wu_ref[:, pl.ds(off_g, FF_BLK)])

    def conv_act(blk, slot):
        a_ref, g_ref = slots[slot]
        off_a = pl.multiple_of(blk * FF_BLK, FF_BLK)
        off_g = pl.multiple_of(blk * FF_BLK + D_FF, FF_BLK)

        def conv(z_ref, off, r0):
            cw = cw_ref[:, pl.ds(off, FF_BLK)]
            cb = cb_ref[:, pl.ds(off, FF_BLK)]
            prev = z_ref[pad - shift + r0:pad - shift + r0 + nr]
            nxt = z_ref[pad + shift + r0:pad + shift + r0 + nr]
            if not halo:
                if r0 % seq_len == 0:
                    prev = jnp.where(row == 0, 0.0, prev)
                if (r0 + nr) % seq_len == 0:
                    nxt = jnp.where(row == nr - 1, 0.0, nxt)
            return cw[0:1] * prev + cw[1:2] * z_ref[pad + r0:pad + r0 + nr] + cw[2:3] * nxt + cb

        for r0 in range(0, tt, nr):
            ac = conv(a_ref, off_a, r0)
            gc = conv(g_ref, off_g, r0)
            act_ref[r0:r0 + nr, pl.ds(off_a, FF_BLK)] = (gc * jax.nn.sigmoid(gc) * ac).astype(BF16)

    h2e_ref[halo:halo + tt] = h2_ref[...]
    if halo:
        h2e_ref[0:halo] = prev_ref[...]
        h2e_ref[halo + tt:] = next_ref[...]

        @pl.when(i % tiles_per_image == 0)
        def _():
            h2e_ref[0:halo] = jnp.zeros((halo, D_MODEL), BF16)

        @pl.when(i % tiles_per_image == tiles_per_image - 1)
        def _():
            h2e_ref[halo + tt:] = jnp.zeros((halo, D_MODEL), BF16)
    else:
        for z_ref in (a0_ref, a1_ref, g0_ref, g1_ref):
            z_ref[0:pad] = jnp.zeros((pad, FF_BLK), F32)
            z_ref[pad + tt:] = jnp.zeros((pad, FF_BLK), F32)

    up_proj(0, 0)

    def block_pair(p, carry):
        blk = 2 * p + 1
        conv_act(blk - 1, 0)
        up_proj(blk, 1)
        conv_act(blk, 1)
        up_proj(blk + 1, 0)
        return carry

    lax.fori_loop(0, last_blk // 2, block_pair, 0)

    k0 = last_blk * FF_BLK
    part = _dot(act_ref[:, 0:k0], wd_ref[0:k0, :])
    conv_act(last_blk, 0)
    ffn = part + _dot(act_ref[:, k0:], wd_ref[k0:, :])
    x2 = x1_ref[...] + mod_ref[5:6] * ffn
    y_ref[...] = _rms_scale(x2) * fn_ref[...]


def _ffn_call(h2, x1, mod3, mod_row, tt, halo, seq_len, w):
    n_tok = x1.shape[0]
    n_tiles = n_tok // tt
    tok = lambda i: (i, 0)
    const = lambda i: (0, 0)
    resident = dict(pipeline_mode=pl.Buffered(1))
    in_specs = [pl.BlockSpec((tt, D_MODEL), tok)]
    args = [h2]
    up_rows = tt + 2 * (halo if halo else SUBLANES)
    scratch = [pltpu.VMEM((tt, D_FF), BF16)] + [pltpu.VMEM((up_rows, FF_BLK), F32)] * 4
    tiles_per_image = 1
    if halo:
        tiles_per_image = seq_len // tt
        per = tt // halo
        n_halo_blk = n_tok // halo
        in_specs += [
            pl.BlockSpec((halo, D_MODEL), lambda i: (jnp.maximum(i * per - 1, 0), 0)),
            pl.BlockSpec((halo, D_MODEL), lambda i: (jnp.minimum((i + 1) * per, n_halo_blk - 1), 0)),
        ]
        args += [h2, h2]
    scratch.append(pltpu.VMEM((tt + 2 * halo, D_MODEL), BF16))
    in_specs += [
        pl.BlockSpec((tt, D_MODEL), tok),
        pl.BlockSpec((None, 6, D_MODEL), lambda i: (mod_row(i), 0, 0)),
        pl.BlockSpec((1, D_MODEL), const),
        pl.BlockSpec((D_MODEL, 2 * D_FF), const, **resident),
        pl.BlockSpec((3, 2 * D_FF), const),
        pl.BlockSpec((1, 2 * D_FF), const),
        pl.BlockSpec((D_FF, D_MODEL), const, **resident),
    ]
    args += [x1, mod3, w["final_norm"], w["w_up"], w["conv_ffn_w"], w["conv_ffn_b"], w["w_down"]]
    return pl.pallas_call(
        functools.partial(_ffn_kernel, halo, seq_len, tiles_per_image),
        grid=(n_tiles,),
        in_specs=in_specs,
        out_specs=pl.BlockSpec((tt, D_MODEL), tok),
        out_shape=jax.ShapeDtypeStruct((n_tok, D_MODEL), F32),
        scratch_shapes=scratch,
        compiler_params=_cparams(1),
        name="ffn",
    )(*args)


def _trunk(x, mod3, mod_row_of_batch, row_len, ffn_halo, init, want_state, tt_in, tt_ffn, cg, w):
    bsz, t, _ = x.shape
    x2d = x.reshape(bsz * t, D_MODEL)

    def mod_row(tile_tokens):
        return lambda i: mod_row_of_batch((i * tile_tokens) // t)

    ysc, qt, k, vt, sot, gcol, grow = _in_proj_call(x2d, mod3, mod_row(tt_in), row_len, tt_in, w)
    outs = _mlstm_call(bsz, t, cg, qt, k, vt, gcol, grow, init, want_state)
    x1, h2 = _out_proj_call(x2d, ysc, outs[0], outs[1], sot, mod3, mod_row(tt_in), tt_in, w)
    y = _ffn_call(h2, x1, mod3, mod_row(tt_ffn), tt_ffn, ffn_halo, t, w)
    return y.reshape(bsz, t, D_MODEL), outs[2:]


def kernel(x_prompt, x_sample, state_C, state_n, state_m, c, c_ctx, w_mod, b_mod, norm1, w_in, b_gate,
           conv_sc_w, conv_sc_b, mh_norm, w_out, norm2, w_up, conv_ffn_w, conv_ffn_b, w_down, final_norm):
    n_lat = c.shape[0]
    n_ctx = x_prompt.shape[0]
    c8 = jnp.concatenate([c, c_ctx[None], jnp.zeros((8 - n_lat - 1, D_MODEL), F32)], axis=0)
    l = 0
    mod3 = _mod_call(c8, w_mod[l], b_mod[l][None]).reshape(8, 6, D_MODEL)

    wi = w_in[l]
    q0 = 3 * D_CONV
    g0 = q0 + 4 * D_MLSTM
    zpad = jnp.zeros((D_MODEL, LANES - N_CHAINS), F32)
    cols = lambda a: wi[:, g0 + a * N_HEADS:g0 + (a + 1) * N_HEADS]
    w_g = jnp.concatenate([cols(0), cols(2), zpad, cols(1), cols(3), zpad], axis=1).astype(BF16)
    bpad = jnp.zeros((LANES - N_CHAINS,), F32)
    bg = b_gate[l].astype(F32)
    b_g = jnp.concatenate([bg[0], bg[2], bpad, bg[1], bg[3], bpad])[None]
    w = dict(
        norm1=norm1[l][None], norm2=norm2[l][None], final_norm=final_norm[None],
        w_sc=wi[:, 0:q0].astype(BF16),
        w_k=wi[:, q0 + D_MLSTM:q0 + 2 * D_MLSTM].astype(BF16),
        w_qvot=jnp.concatenate([wi[:, q0:q0 + D_MLSTM], wi[:, q0 + 2 * D_MLSTM:g0]], axis=1).T.astype(BF16),
        w_g=w_g, b_g=b_g,
        conv_sc_w=conv_sc_w[l], conv_sc_b=conv_sc_b[l][None],
        mh_norm_col=mh_norm[l][:, None],
        w_out_a=w_out[l][0:D_CONV].astype(BF16), w_out_b=w_out[l][D_CONV:].astype(BF16),
        w_up=w_up[l].astype(BF16), conv_ffn_w=conv_ffn_w[l], conv_ffn_b=conv_ffn_b[l][None],
        w_down=w_down[l].astype(BF16),
    )

    seq = x_prompt.shape[1]
    y_prompt, (new_c, new_n, new_m) = _trunk(
        x_prompt, mod3, lambda b: n_lat, seq, 0, None, True, 512, 1024, 1, w)

    init = (state_C[:, l], state_n[:, l][:, :, :, None, :],
            jnp.broadcast_to(state_m[:, l][:, :, :, None, None], state_m[:, l].shape + (1, LANES)))
    y_sample, _ = _trunk(x_sample, mod3, lambda b: b, GRID_W, GRID_W, init, False, 512, 1024, 2, w)

    new_n = new_n.reshape(n_ctx, 1, 2, N_HEADS, HEAD_DIM)
    new_m = new_m[..., 0, 0]
    return y_prompt, y_sample, new_c, new_n, new_m
```

```python
import functools

import jax
import jax.numpy as jnp
from jax import lax
from jax.experimental import pallas as pl
from jax.experimental.pallas import tpu as pltpu

F32 = jnp.float32
BF16 = jnp.bfloat16

D_MODEL = 1024
GRID_W = 64
D_CONV = 512
D_MLSTM = 512
N_HEADS = 4
HEAD_DIM = 128
D_FF = 2816
CHUNK = 256
EPS = 1e-6

LANES = 128
SUBLANES = 8
BF16_ROWS = 16
FF_BLK = 256
N_FF_BLK = D_FF // FF_BLK
CONV_ROWS = 128
VMEM_LIMIT = 60 * 1024 * 1024
N_CHAINS = 2 * N_HEADS
E_ROWS = HEAD_DIM + BF16_ROWS
SKEW = 3
IN_SUB = 512
SKEW_FF = 2


def _cparams(n_axes):
    return pltpu.CompilerParams(
        dimension_semantics=("arbitrary",) * n_axes, vmem_limit_bytes=VMEM_LIMIT)


def _rms_scale(x):
    return x * lax.rsqrt(jnp.mean(x * x, axis=-1, keepdims=True) + EPS)


def _dot(a, b):
    return jnp.dot(a, b, preferred_element_type=F32)


def _dot_nt(a, b):
    return lax.dot_general(a, b, (((1,), (1,)), ((), ())), preferred_element_type=F32)


def _dot_tn(a, b):
    return lax.dot_general(a, b, (((0,), (0,)), ((), ())), preferred_element_type=F32)


def _mod_kernel(c_ref, w_ref, b_ref, o_ref):
    c = c_ref[...]
    s = c * jax.nn.sigmoid(c)
    o_ref[...] = _dot(s.astype(BF16), w_ref[...].astype(BF16)) + b_ref[...]


def _mod_call(c8, w_mod, b_mod):
    n_out = w_mod.shape[1]
    blk = 1024
    return pl.pallas_call(
        _mod_kernel,
        grid=(n_out // blk,),
        in_specs=[
            pl.BlockSpec((8, D_MODEL), lambda i: (0, 0)),
            pl.BlockSpec((D_MODEL, blk), lambda i: (0, i)),
            pl.BlockSpec((1, blk), lambda i: (0, i)),
        ],
        out_specs=pl.BlockSpec((8, blk), lambda i: (0, i)),
        out_shape=jax.ShapeDtypeStruct((8, n_out), F32),
        compiler_params=_cparams(1),
        name="mod",
    )(c8, w_mod, b_mod)


def _chunk_scan(x, pos, op, fill, reverse):
    n = x.shape[0]
    k = 1
    while k < CHUNK:
        if reverse:
            shifted = pltpu.roll(x, n - k, axis=0)
            ok = pos < CHUNK - k
        else:
            shifted = pltpu.roll(x, k, axis=0)
            ok = pos >= k
        x = op(x, jnp.where(ok, shifted, fill))
        k *= 2
    return x


def _in_proj_kernel(row_len, x_ref, mod_ref, n1_ref, wsc_ref, wk_ref, wqvot_ref, wg_ref, bg_ref, cw_ref, cb_ref,
                    ysc_ref, qt_ref, k_ref, vt_ref, sot_ref, gcol_ref, grow_ref):
    tt = x_ref.shape[0]
    sub = min(tt, IN_SUB)
    mod = mod_ref[...]
    shift1, scale1 = mod[0:1], mod[1:2]

    def prep(r0):
        x = x_ref[r0:r0 + sub]
        h = (_rms_scale(x) * n1_ref[...]) * (1.0 + scale1) + shift1
        return h.astype(BF16)

    def project(r0, hb):
        rows = slice(r0, r0 + sub)
        g = _dot(hb, wg_ref[...]) + bg_ref[...]
        gi = g[:, 0:LANES]
        gf = g[:, LANES:2 * LANES]
        logf = jnp.minimum(gf, 0.0) - jnp.log1p(jnp.exp(-jnp.abs(gf)))
        lane = lax.broadcasted_iota(jnp.int32, gi.shape, 1)
        cpos = lax.broadcasted_iota(jnp.int32, gi.shape, 0) % CHUNK
        fwd = lane < N_HEADS
        b = jnp.where(fwd, _chunk_scan(logf, cpos, jnp.add, 0.0, False),
                      _chunk_scan(logf, cpos, jnp.add, 0.0, True))
        r = gi - b
        cm = jnp.where(fwd, _chunk_scan(r, cpos, jnp.maximum, -jnp.inf, False),
                       _chunk_scan(r, cpos, jnp.maximum, -jnp.inf, True))
        packed = jnp.where(lane < N_CHAINS, cm,
                           jnp.where(lane < 2 * N_CHAINS, pltpu.roll(r, N_CHAINS, axis=1),
                                     pltpu.roll(b, 2 * N_CHAINS, axis=1)))
        gcol_ref[rows] = packed
        for c in range(sub // CHUNK):
            grow_ref[r0 // CHUNK + c] = packed[c * CHUNK:(c + 1) * CHUNK].T[0:3 * N_CHAINS]

        zb = _dot(hb, wsc_ref[:, 0:D_CONV])
        zc = _dot(hb, wsc_ref[:, D_CONV:2 * D_CONV])
        zx = _dot(hb, wsc_ref[:, 2 * D_CONV:3 * D_CONV])
        u = zc * zx
        pos = lax.broadcasted_iota(jnp.int32, u.shape, 0) % row_len
        prev = jnp.where(pos == 0, 0.0, pltpu.roll(u, 1, axis=0))
        nxt = jnp.where(pos == row_len - 1, 0.0, pltpu.roll(u, sub - 1, axis=0))
        cw = cw_ref[...]
        conv = cw[0:1] * prev + cw[1:2] * u + cw[2:3] * nxt + cb_ref[...]
        ysc_ref[rows] = (zb * conv).astype(BF16)

        k_ref[rows] = _dot(hb, wk_ref[...]).astype(BF16)
        qvo_t = _dot_nt(wqvot_ref[...], hb)
        qt_ref[:, rows] = (qvo_t[0:D_MLSTM] * (HEAD_DIM ** -0.5)).astype(BF16)
        vt_ref[:, rows] = qvo_t[D_MLSTM:2 * D_MLSTM].astype(BF16)
        sot_ref[:, rows] = jax.nn.sigmoid(qvo_t[2 * D_MLSTM:3 * D_MLSTM])

    starts = range(0, tt, sub)
    normed = [prep(r0) for r0 in starts]
    for r0, hb in zip(starts, normed):
        project(r0, hb)


def _in_proj_call(x2d, mod3, mod_row, row_len, tt, w):
    n_tok = x2d.shape[0]
    n_tiles = n_tok // tt
    const = lambda i: (0, 0)
    tok = lambda i: (i, 0)
    tok_t = lambda i: (0, i)
    weights = [w["norm1"], w["w_sc"], w["w_k"], w["w_qvot"], w["w_g"], w["b_g"], w["conv_sc_w"], w["conv_sc_b"]]
    out_shapes = (
        jax.ShapeDtypeStruct((n_tok, D_CONV), BF16),
        jax.ShapeDtypeStruct((D_MLSTM, n_tok), BF16),
        jax.ShapeDtypeStruct((n_tok, D_MLSTM), BF16),
        jax.ShapeDtypeStruct((D_MLSTM, n_tok), BF16),
        jax.ShapeDtypeStruct((D_MLSTM, n_tok), F32),
        jax.ShapeDtypeStruct((n_tok, LANES), F32),
        jax.ShapeDtypeStruct((n_tok // CHUNK, 3 * N_CHAINS, CHUNK), F32),
    )
    return pl.pallas_call(
        functools.partial(_in_proj_kernel, row_len),
        grid=(n_tiles,),
        in_specs=[
            pl.BlockSpec((tt, D_MODEL), tok),
            pl.BlockSpec((None, 6, D_MODEL), lambda i: (mod_row(i), 0, 0)),
        ] + [pl.BlockSpec(a.shape, const) for a in weights],
        out_specs=[
            pl.BlockSpec((tt, D_CONV), tok),
            pl.BlockSpec((D_MLSTM, tt), tok_t),
            pl.BlockSpec((tt, D_MLSTM), tok),
            pl.BlockSpec((D_MLSTM, tt), tok_t),
            pl.BlockSpec((D_MLSTM, tt), tok_t),
            pl.BlockSpec((tt, LANES), tok),
            pl.BlockSpec((tt // CHUNK, 3 * N_CHAINS, CHUNK), lambda i: (i, 0, 0)),
        ],
        out_shape=out_shapes,
        compiler_params=_cparams(1),
        name="in_proj",
    )(x2d, mod3, *weights)


def _mlstm_kernel(cg, has_init, want_state, *refs):
    refs = list(refs)
    fwd_refs, bwd_refs = refs[0:5], refs[5:10]
    refs = refs[10:]
    if has_init:
        c0_ref, n0_ref, m0_ref = refs[:3]
        refs = refs[3:]
    htf_ref, htb_ref = refs[:2]
    refs = refs[2:]
    if want_state:
        cout_ref, nout_ref, mout_ref = refs[:3]
        refs = refs[3:]
    e_ref, m_ref = refs

    g = pl.program_id(1)

    @pl.when(g == 0)
    def _():
        if has_init:
            for d in range(2):
                for hd in range(N_HEADS):
                    j = d * N_HEADS + hd
                    e_ref[j, 0:HEAD_DIM] = c0_ref[d, hd]
                    e_ref[j, HEAD_DIM:] = jnp.broadcast_to(n0_ref[d, hd], (BF16_ROWS, HEAD_DIM))
                    m_ref[j] = jnp.broadcast_to(m0_ref[d, hd], (SUBLANES, LANES))
        else:
            e_ref[...] = jnp.zeros(e_ref.shape, F32)
            m_ref[...] = jnp.zeros(m_ref.shape, F32)

    s_idx = lax.broadcasted_iota(jnp.int32, (CHUNK, CHUNK), 0)
    t_idx = lax.broadcasted_iota(jnp.int32, (CHUNK, CHUNK), 1)
    ones_rows = jnp.ones((BF16_ROWS, CHUNK), BF16)

    def stage_scores(c, d, hd):
        qt_ref, k_ref, vt_ref, row_ref, col_ref = fwd_refs if d == 0 else bwd_refs
        j = d * N_HEADS + hd
        toks = slice(c * CHUNK, (c + 1) * CHUNK)
        feat = slice(hd * HEAD_DIM, (hd + 1) * HEAD_DIM)
        last = CHUNK - 1 if d == 0 else 0

        qtc = qt_ref[feat, toks]
        kc = k_ref[toks, feat]
        vtc = vt_ref[feat, toks]
        cm_row = row_ref[c, j:j + 1, :]
        r_row = row_ref[c, N_CHAINS + j:N_CHAINS + j + 1, :]
        b_row = row_ref[c, 2 * N_CHAINS + j:2 * N_CHAINS + j + 1, :]
        m_row = m_ref[j, 0:1, :]
        m_t = m_row[:, 0:1]
        big_m_row = jnp.maximum(m_t, cm_row)
        m_last = big_m_row[:, last:last + 1]

        e = e_ref[j]
        kq = _dot(jnp.concatenate([kc, e.astype(BF16)], axis=0), qtc)

        wg = jnp.exp(r_row - m_last)
        lhs = jnp.concatenate([(vtc.astype(F32) * wg).astype(BF16),
                               jnp.broadcast_to(wg, (BF16_ROWS, CHUNK)).astype(BF16)], axis=0)
        e_ref[j] = jnp.exp(m_row - m_last) * e + _dot(lhs, kc)
        m_ref[j] = jnp.broadcast_to(b_row[:, last:last + 1] + m_last, (SUBLANES, LANES))
        return kq, vtc, big_m_row, m_t, b_row

    def stage_output(c, d, hd, kq, vtc, big_m_row, m_t, b_row):
        col_ref = (fwd_refs if d == 0 else bwd_refs)[4]
        ht_ref = htf_ref if d == 0 else htb_ref
        j = d * N_HEADS + hd
        toks = slice(c * CHUNK, (c + 1) * CHUNK)
        feat = slice(hd * HEAD_DIM, (hd + 1) * HEAD_DIM)
        mask = (s_idx <= t_idx) if d == 0 else (s_idx >= t_idx)
        r_col = col_ref[toks, N_CHAINS + j:N_CHAINS + j + 1]

        w_t = jnp.exp(jnp.where(mask, r_col - big_m_row, -jnp.inf))
        s_t = (kq[0:CHUNK] * w_t).astype(BF16)
        num_t = _dot(jnp.concatenate([vtc, ones_rows], axis=0), s_t)
        nd = num_t + jnp.exp(m_t - big_m_row) * kq[CHUNK:]
        inv = 1.0 / jnp.maximum(jnp.abs(nd[HEAD_DIM:HEAD_DIM + 1]), jnp.exp(-(b_row + big_m_row)))
        ht_ref[feat, toks] = nd[0:HEAD_DIM] * inv

    tasks = [(c if d == 0 else cg - 1 - c, d, hd) for c in range(cg) for d in range(2) for hd in range(N_HEADS)]
    pending = []
    for task in tasks:
        pending.append(task + stage_scores(*task))
        if len(pending) > SKEW:
            stage_output(*pending.pop(0))
    for args in pending:
        stage_output(*args)

    if want_state:
        @pl.when(g == pl.num_programs(1) - 1)
        def _():
            for d in range(2):
                for hd in range(N_HEADS):
                    j = d * N_HEADS + hd
                    cout_ref[d, hd] = e_ref[j, 0:HEAD_DIM]
                    nout_ref[d, hd] = e_ref[j, HEAD_DIM:HEAD_DIM + 1]
                    mout_ref[d, hd] = m_ref[j, 0:1, :]


def _mlstm_call(bsz, t, cg, qt, k, vt, gcol, grow, init, want_state):
    n_tok = bsz * t
    tg = cg * CHUNK
    n_groups = t // tg
    fwd = lambda b, g: b * n_groups + g
    bwd = lambda b, g: b * n_groups + (n_groups - 1 - g)

    def stream_specs(pos):
        return [
            pl.BlockSpec((D_MLSTM, tg), lambda b, g: (0, pos(b, g))),
            pl.BlockSpec((tg, D_MLSTM), lambda b, g: (pos(b, g), 0)),
            pl.BlockSpec((D_MLSTM, tg), lambda b, g: (0, pos(b, g))),
            pl.BlockSpec((cg, 3 * N_CHAINS, CHUNK), lambda b, g: (pos(b, g), 0, 0)),
            pl.BlockSpec((tg, LANES), lambda b, g: (pos(b, g), 0)),
        ]

    in_specs = stream_specs(fwd) + stream_specs(bwd)
    args = [qt, k, vt, grow, gcol] * 2
    if init is not None:
        c0, n0, m0 = init
        in_specs += [
            pl.BlockSpec((None, 2, N_HEADS, HEAD_DIM, HEAD_DIM), lambda b, g: (b, 0, 0, 0, 0)),
            pl.BlockSpec((None, 2, N_HEADS, 1, HEAD_DIM), lambda b, g: (b, 0, 0, 0, 0)),
            pl.BlockSpec((None, 2, N_HEADS, 1, LANES), lambda b, g: (b, 0, 0, 0, 0)),
        ]
        args += [c0, n0, m0]
    out_specs = [
        pl.BlockSpec((D_MLSTM, tg), lambda b, g: (0, fwd(b, g))),
        pl.BlockSpec((D_MLSTM, tg), lambda b, g: (0, bwd(b, g))),
    ]
    out_shape = [jax.ShapeDtypeStruct((D_MLSTM, n_tok), F32)] * 2
    if want_state:
        out_specs += [
            pl.BlockSpec((None, None, 2, N_HEADS, HEAD_DIM, HEAD_DIM), lambda b, g: (b, 0, 0, 0, 0, 0)),
            pl.BlockSpec((None, None, 2, N_HEADS, 1, HEAD_DIM), lambda b, g: (b, 0, 0, 0, 0, 0)),
            pl.BlockSpec((None, None, 2, N_HEADS, 1, LANES), lambda b, g: (b, 0, 0, 0, 0, 0)),
        ]
        out_shape += [
            jax.ShapeDtypeStruct((bsz, 1, 2, N_HEADS, HEAD_DIM, HEAD_DIM), F32),
            jax.ShapeDtypeStruct((bsz, 1, 2, N_HEADS, 1, HEAD_DIM), F32),
            jax.ShapeDtypeStruct((bsz, 1, 2, N_HEADS, 1, LANES), F32),
        ]
    return pl.pallas_call(
        functools.partial(_mlstm_kernel, cg, init is not None, want_state),
        grid=(bsz, n_groups),
        in_specs=in_specs,
        out_specs=out_specs,
        out_shape=out_shape,
        scratch_shapes=[
            pltpu.VMEM((N_CHAINS, E_ROWS, HEAD_DIM), F32),
            pltpu.VMEM((N_CHAINS, SUBLANES, LANES), F32),
        ],
        compiler_params=_cparams(2),
        name="mlstm",
    )(*args)


def _out_proj_kernel(x_ref, ysc_ref, htf_ref, htb_ref, sot_ref, mh_ref, mod_ref, n2_ref, wa_ref, wb_ref,
                     x1_ref, h2_ref):
    mod = mod_ref[...]
    gate1, shift2, scale2 = mod[2:3], mod[3:4], mod[4:5]
    hs = htf_ref[...] + htb_ref[...]
    heads = []
    for hd in range(N_HEADS):
        blk = hs[hd * HEAD_DIM:(hd + 1) * HEAD_DIM]
        heads.append(blk * lax.rsqrt(jnp.mean(blk * blk, axis=0, keepdims=True) + EPS))
    hm_t = ((jnp.concatenate(heads, axis=0) * mh_ref[...]) * sot_ref[...]).astype(BF16)
    mix = _dot(ysc_ref[...], wa_ref[...]) + _dot_tn(hm_t, wb_ref[...])
    x1 = x_ref[...] + gate1 * mix
    x1_ref[...] = x1
    h2_ref[...] = ((_rms_scale(x1) * n2_ref[...]) * (1.0 + scale2) + shift2).astype(BF16)


def _out_proj_call(x2d, ysc, htf, htb, sot, mod3, mod_row, tt, w):
    n_tok = x2d.shape[0]
    const = lambda i: (0, 0)
    tok = lambda i: (i, 0)
    tok_t = lambda i: (0, i)
    return pl.pallas_call(
        _out_proj_kernel,
        grid=(n_tok // tt,),
        in_specs=[
            pl.BlockSpec((tt, D_MODEL), tok),
            pl.BlockSpec((tt, D_CONV), tok),
            pl.BlockSpec((D_MLSTM, tt), tok_t),
            pl.BlockSpec((D_MLSTM, tt), tok_t),
            pl.BlockSpec((D_MLSTM, tt), tok_t),
            pl.BlockSpec((D_MLSTM, 1), const),
            pl.BlockSpec((None, 6, D_MODEL), lambda i: (mod_row(i), 0, 0)),
            pl.BlockSpec((1, D_MODEL), const),
            pl.BlockSpec(w["w_out_a"].shape, const),
            pl.BlockSpec(w["w_out_b"].shape, const),
        ],
        out_specs=[pl.BlockSpec((tt, D_MODEL), tok), pl.BlockSpec((tt, D_MODEL), tok)],
        out_shape=[jax.ShapeDtypeStruct((n_tok, D_MODEL), F32),
                   jax.ShapeDtypeStruct((n_tok, D_MODEL), BF16)],
        compiler_params=_cparams(1),
        name="out_proj",
    )(x2d, ysc, htf, htb, sot, w["mh_norm_col"], mod3, w["norm2"], w["w_out_a"], w["w_out_b"])


def _ffn_kernel(halo, seq_len, tiles_per_image, *refs):
    if halo:
        h2_ref, prev_ref, next_ref = refs[:3]
        refs = refs[3:]
    else:
        h2_ref = refs[0]
        refs = refs[1:]
    (x1_ref, mod_ref, fn_ref, wu_ref, cw_ref, cb_ref, wd_ref, y_ref,
     act_ref, a0_ref, a1_ref, a2_ref, g0_ref, g1_ref, g2_ref, h2e_ref) = refs
    slots = ((a0_ref, g0_ref), (a1_ref, g1_ref), (a2_ref, g2_ref))
    tt = x1_ref.shape[0]
    i = pl.program_id(0)
    pad = halo if halo else SUBLANES
    shift = halo if halo else 1
    nr = CONV_ROWS
    row = lax.broadcasted_iota(jnp.int32, (nr, FF_BLK), 0)

    def up_proj(blk, slot, part=None):
        a_ref, g_ref = slots[slot]
        off_a = pl.multiple_of(blk * FF_BLK, FF_BLK)
        off_g = pl.multiple_of(blk * FF_BLK + D_FF, FF_BLK)
        hin = h2e_ref[...]
        rows = slice(0, tt + 2 * halo) if halo else slice(pad, pad + tt)
        if part in (None, 0):
            a_ref[rows] = _dot(hin, wu_ref[:, pl.ds(off_a, FF_BLK)])
        if part in (None, 1):
            g_ref[rows] = _dot(hin, wu_ref[:, pl.ds(off_g, FF_BLK)])

    def conv_act(blk, slot, part=None):
        a_ref, g_ref = slots[slot]
        off_a = pl.multiple_of(blk * FF_BLK, FF_BLK)
        off_g = pl.multiple_of(blk * FF_BLK + D_FF, FF_BLK)

        def conv(z_ref, off, r0):
            cw = cw_ref[:, pl.ds(off, FF_BLK)]
            cb = cb_ref[:, pl.ds(off, FF_BLK)]
            prev = z_ref[pad - shift + r0:pad - shift + r0 + nr]
            nxt = z_ref[pad + shift + r0:pad + shift + r0 + nr]
            if not halo:
                if r0 % seq_len == 0:
                    prev = jnp.where(row == 0, 0.0, prev)
                if (r0 + nr) % seq_len == 0:
                    nxt = jnp.where(row == nr - 1, 0.0, nxt)
            return cw[0:1] * prev + cw[1:2] * z_ref[pad + r0:pad + r0 + nr] + cw[2:3] * nxt + cb

        half = tt // 2
        lo, hi = {None: (0, tt), 0: (0, half), 1: (half, tt)}[part]
        for r0 in range(lo, hi, nr):
            ac = conv(a_ref, off_a, r0)
            gc = conv(g_ref, off_g, r0)
            act_ref[r0:r0 + nr, pl.ds(off_a, FF_BLK)] = (gc * jax.nn.sigmoid(gc) * ac).astype(BF16)

    h2e_ref[halo:halo + tt] = h2_ref[...]
    if halo:
        h2e_ref[0:halo] = prev_ref[...]
        h2e_ref[halo + tt:] = next_ref[...]

        @pl.when(i % tiles_per_image == 0)
        def _():
            h2e_ref[0:halo] = jnp.zeros((halo, D_MODEL), BF16)

        @pl.when(i % tiles_per_image == tiles_per_image - 1)
        def _():
            h2e_ref[halo + tt:] = jnp.zeros((halo, D_MODEL), BF16)
    else:
        for a_ref, g_ref in slots:
            for z_ref in (a_ref, g_ref):
                z_ref[0:pad] = jnp.zeros((pad, FF_BLK), F32)
                z_ref[pad + tt:] = jnp.zeros((pad, FF_BLK), F32)

    n_slots = len(slots)
    for blk in range(SKEW_FF):
        up_proj(blk, blk % n_slots)

    def block_group(p, carry):
        for s in range(n_slots):
            blk = n_slots * p + SKEW_FF + s
            up_proj(blk, (SKEW_FF + s) % n_slots)
            conv_act(blk - SKEW_FF, s % n_slots)
        return carry

    n_groups = (N_FF_BLK - SKEW_FF) // n_slots
    lax.fori_loop(0, n_groups, block_group, 0)
    done = n_slots * n_groups + SKEW_FF
    for blk in range(done, N_FF_BLK):
        up_proj(blk, blk % n_slots)
        conv_act(blk - SKEW_FF, (blk - SKEW_FF) % n_slots)

    k0 = (N_FF_BLK - SKEW_FF) * FF_BLK
    part = _dot(act_ref[:, 0:k0], wd_ref[0:k0, :])
    for blk in range(N_FF_BLK - SKEW_FF, N_FF_BLK):
        conv_act(blk, blk % n_slots)
    ffn = part + _dot(act_ref[:, k0:], wd_ref[k0:, :])
    x2 = x1_ref[...] + mod_ref[5:6] * ffn
    y_ref[...] = _rms_scale(x2) * fn_ref[...]


def _ffn_call(h2, x1, mod3, mod_row, tt, halo, seq_len, w):
    n_tok = x1.shape[0]
    n_tiles = n_tok // tt
    tok = lambda i: (i, 0)
    const = lambda i: (0, 0)
    resident = dict(pipeline_mode=pl.Buffered(1))
    in_specs = [pl.BlockSpec((tt, D_MODEL), tok)]
    args = [h2]
    up_rows = tt + 2 * (halo if halo else SUBLANES)
    scratch = [pltpu.VMEM((tt, D_FF), BF16)] + [pltpu.VMEM((up_rows, FF_BLK), F32)] * (2 * (SKEW_FF + 1))
    tiles_per_image = 1
    if halo:
        tiles_per_image = seq_len // tt
        per = tt // halo
        n_halo_blk = n_tok // halo
        in_specs += [
            pl.BlockSpec((halo, D_MODEL), lambda i: (jnp.maximum(i * per - 1, 0), 0)),
            pl.BlockSpec((halo, D_MODEL), lambda i: (jnp.minimum((i + 1) * per, n_halo_blk - 1), 0)),
        ]
        args += [h2, h2]
    scratch.append(pltpu.VMEM((tt + 2 * halo, D_MODEL), BF16))
    in_specs += [
        pl.BlockSpec((tt, D_MODEL), tok),
        pl.BlockSpec((None, 6, D_MODEL), lambda i: (mod_row(i), 0, 0)),
        pl.BlockSpec((1, D_MODEL), const),
        pl.BlockSpec((D_MODEL, 2 * D_FF), const, **resident),
        pl.BlockSpec((3, 2 * D_FF), const),
        pl.BlockSpec((1, 2 * D_FF), const),
        pl.BlockSpec((D_FF, D_MODEL), const, **resident),
    ]
    args += [x1, mod3, w["final_norm"], w["w_up"], w["conv_ffn_w"], w["conv_ffn_b"], w["w_down"]]
    return pl.pallas_call(
        functools.partial(_ffn_kernel, halo, seq_len, tiles_per_image),
        grid=(n_tiles,),
        in_specs=in_specs,
        out_specs=pl.BlockSpec((tt, D_MODEL), tok),
        out_shape=jax.ShapeDtypeStruct((n_tok, D_MODEL), F32),
        scratch_shapes=scratch,
        compiler_params=_cparams(1),
        name="ffn",
    )(*args)


def _trunk(x, mod3, mod_row_of_batch, row_len, ffn_halo, init, want_state, tt_out, tt_in, tt_ffn, cg, w):
    bsz, t, _ = x.shape
    x2d = x.reshape(bsz * t, D_MODEL)

    def mod_row(tile_tokens):
        return lambda i: mod_row_of_batch((i * tile_tokens) // t)

    ysc, qt, k, vt, sot, gcol, grow = _in_proj_call(x2d, mod3, mod_row(tt_in), row_len, tt_in, w)
    outs = _mlstm_call(bsz, t, cg, qt, k, vt, gcol, grow, init, want_state)
    x1, h2 = _out_proj_call(x2d, ysc, outs[0], outs[1], sot, mod3, mod_row(tt_out), tt_out, w)
    y = _ffn_call(h2, x1, mod3, mod_row(tt_ffn), tt_ffn, ffn_halo, t, w)
    return y.reshape(bsz, t, D_MODEL), outs[2:]


def kernel(x_prompt, x_sample, state_C, state_n, state_m, c, c_ctx, w_mod, b_mod, norm1, w_in, b_gate,
           conv_sc_w, conv_sc_b, mh_norm, w_out, norm2, w_up, conv_ffn_w, conv_ffn_b, w_down, final_norm):
    n_lat = c.shape[0]
    n_ctx = x_prompt.shape[0]
    c8 = jnp.concatenate([c, c_ctx[None], jnp.zeros((8 - n_lat - 1, D_MODEL), F32)], axis=0)
    l = 0
    mod3 = _mod_call(c8, w_mod[l], b_mod[l][None]).reshape(8, 6, D_MODEL)

    wi = w_in[l]
    q0 = 3 * D_CONV
    g0 = q0 + 4 * D_MLSTM
    zpad = jnp.zeros((D_MODEL, LANES - N_CHAINS), F32)
    cols = lambda a: wi[:, g0 + a * N_HEADS:g0 + (a + 1) * N_HEADS]
    w_g = jnp.concatenate([cols(0), cols(2), zpad, cols(1), cols(3), zpad], axis=1).astype(BF16)
    bpad = jnp.zeros((LANES - N_CHAINS,), F32)
    bg = b_gate[l].astype(F32)
    b_g = jnp.concatenate([bg[0], bg[2], bpad, bg[1], bg[3], bpad])[None]
    w = dict(
        norm1=norm1[l][None], norm2=norm2[l][None], final_norm=final_norm[None],
        w_sc=wi[:, 0:q0].astype(BF16),
        w_k=wi[:, q0 + D_MLSTM:q0 + 2 * D_MLSTM].astype(BF16),
        w_qvot=jnp.concatenate([wi[:, q0:q0 + D_MLSTM], wi[:, q0 + 2 * D_MLSTM:g0]], axis=1).T.astype(BF16),
        w_g=w_g, b_g=b_g,
        conv_sc_w=conv_sc_w[l], conv_sc_b=conv_sc_b[l][None],
        mh_norm_col=mh_norm[l][:, None],
        w_out_a=w_out[l][0:D_CONV].astype(BF16), w_out_b=w_out[l][D_CONV:].astype(BF16),
        w_up=w_up[l].astype(BF16), conv_ffn_w=conv_ffn_w[l], conv_ffn_b=conv_ffn_b[l][None],
        w_down=w_down[l].astype(BF16),
    )

    seq = x_prompt.shape[1]
    y_prompt, (new_c, new_n, new_m) = _trunk(
        x_prompt, mod3, lambda b: n_lat, seq, 0, None, True, 512, 512, 1024, 1, w)

    init = (state_C[:, l], state_n[:, l][:, :, :, None, :],
            jnp.broadcast_to(state_m[:, l][:, :, :, None, None], state_m[:, l].shape + (1, LANES)))
    y_sample, _ = _trunk(x_sample, mod3, lambda b: b, GRID_W, GRID_W, init, False, 512, 512, 1024, 2, w)

    new_n = new_n.reshape(n_ctx, 1, 2, N_HEADS, HEAD_DIM)
    new_m = new_m[..., 0, 0]
    return y_prompt, y_sample, new_c, new_n, new_m
```

```python
import functools

import jax
import jax.numpy as jnp
from jax import lax
from jax.experimental import pallas as pl
from jax.experimental.pallas import tpu as pltpu

F32 = jnp.float32
BF16 = jnp.bfloat16

D_MODEL = 1024
GRID_W = 64
D_CONV = 512
D_MLSTM = 512
N_HEADS = 4
HEAD_DIM = 128
D_FF = 2816
CHUNK = 256
EPS = 1e-6

LANES = 128
SUBLANES = 8
BF16_ROWS = 16
FF_BLK = 256
N_FF_BLK = D_FF // FF_BLK
CONV_ROWS = 128
FFN_TILE = 1024
FFN_COLS = 16
VMEM_LIMIT = 60 * 1024 * 1024
N_CHAINS = 2 * N_HEADS
E_ROWS = HEAD_DIM + BF16_ROWS
SKEW = 3
IN_SUB = 512
SKEW_FF = 2


def _cparams(n_axes):
    return pltpu.CompilerParams(
        dimension_semantics=("arbitrary",) * n_axes, vmem_limit_bytes=VMEM_LIMIT)


def _rms_scale(x):
    return x * lax.rsqrt(jnp.mean(x * x, axis=-1, keepdims=True) + EPS)


def _dot(a, b):
    return jnp.dot(a, b, preferred_element_type=F32)


def _dot_nt(a, b):
    return lax.dot_general(a, b, (((1,), (1,)), ((), ())), preferred_element_type=F32)


def _dot_tn(a, b):
    return lax.dot_general(a, b, (((0,), (0,)), ((), ())), preferred_element_type=F32)


def _mod_kernel(c_ref, w_ref, b_ref, o_ref):
    c = c_ref[...]
    s = c * jax.nn.sigmoid(c)
    o_ref[...] = _dot(s.astype(BF16), w_ref[...].astype(BF16)) + b_ref[...]


def _mod_call(c8, w_mod, b_mod):
    n_out = w_mod.shape[1]
    blk = 1024
    return pl.pallas_call(
        _mod_kernel,
        grid=(n_out // blk,),
        in_specs=[
            pl.BlockSpec((8, D_MODEL), lambda i: (0, 0)),
            pl.BlockSpec((D_MODEL, blk), lambda i: (0, i)),
            pl.BlockSpec((1, blk), lambda i: (0, i)),
        ],
        out_specs=pl.BlockSpec((8, blk), lambda i: (0, i)),
        out_shape=jax.ShapeDtypeStruct((8, n_out), F32),
        compiler_params=_cparams(1),
        name="mod",
    )(c8, w_mod, b_mod)


def _chunk_scan(x, pos, op, fill, reverse):
    n = x.shape[0]
    k = 1
    while k < CHUNK:
        if reverse:
            shifted = pltpu.roll(x, n - k, axis=0)
            ok = pos < CHUNK - k
        else:
            shifted = pltpu.roll(x, k, axis=0)
            ok = pos >= k
        x = op(x, jnp.where(ok, shifted, fill))
        k *= 2
    return x


def _in_proj_kernel(row_len, x_ref, mod_ref, n1_ref, wsc_ref, wk_ref, wqvot_ref, wg_ref, bg_ref, cw_ref, cb_ref,
                    ysc_ref, qt_ref, k_ref, vt_ref, sot_ref, gcol_ref, grow_ref):
    tt = x_ref.shape[0]
    sub = min(tt, IN_SUB)
    mod = mod_ref[...]
    shift1, scale1 = mod[0:1], mod[1:2]

    def prep(r0):
        x = x_ref[r0:r0 + sub]
        h = (_rms_scale(x) * n1_ref[...]) * (1.0 + scale1) + shift1
        return h.astype(BF16)

    def project(r0, hb):
        rows = slice(r0, r0 + sub)
        g = _dot(hb, wg_ref[...]) + bg_ref[...]
        gi = g[:, 0:LANES]
        gf = g[:, LANES:2 * LANES]
        logf = jnp.minimum(gf, 0.0) - jnp.log1p(jnp.exp(-jnp.abs(gf)))
        lane = lax.broadcasted_iota(jnp.int32, gi.shape, 1)
        cpos = lax.broadcasted_iota(jnp.int32, gi.shape, 0) % CHUNK
        fwd = lane < N_HEADS
        b = jnp.where(fwd, _chunk_scan(logf, cpos, jnp.add, 0.0, False),
                      _chunk_scan(logf, cpos, jnp.add, 0.0, True))
        r = gi - b
        cm = jnp.where(fwd, _chunk_scan(r, cpos, jnp.maximum, -jnp.inf, False),
                       _chunk_scan(r, cpos, jnp.maximum, -jnp.inf, True))
        packed = jnp.where(lane < N_CHAINS, cm,
                           jnp.where(lane < 2 * N_CHAINS, pltpu.roll(r, N_CHAINS, axis=1),
                                     pltpu.roll(b, 2 * N_CHAINS, axis=1)))
        gcol_ref[rows] = packed
        for c in range(sub // CHUNK):
            grow_ref[r0 // CHUNK + c] = packed[c * CHUNK:(c + 1) * CHUNK].T[0:3 * N_CHAINS]

        zb = _dot(hb, wsc_ref[:, 0:D_CONV])
        zc = _dot(hb, wsc_ref[:, D_CONV:2 * D_CONV])
        zx = _dot(hb, wsc_ref[:, 2 * D_CONV:3 * D_CONV])
        u = zc * zx
        pos = lax.broadcasted_iota(jnp.int32, u.shape, 0) % row_len
        prev = jnp.where(pos == 0, 0.0, pltpu.roll(u, 1, axis=0))
        nxt = jnp.where(pos == row_len - 1, 0.0, pltpu.roll(u, sub - 1, axis=0))
        cw = cw_ref[...]
        conv = cw[0:1] * prev + cw[1:2] * u + cw[2:3] * nxt + cb_ref[...]
        ysc_ref[rows] = (zb * conv).astype(BF16)

        k_ref[rows] = _dot(hb, wk_ref[...]).astype(BF16)
        qvo_t = _dot_nt(wqvot_ref[...], hb)
        qt_ref[:, rows] = (qvo_t[0:D_MLSTM] * (HEAD_DIM ** -0.5)).astype(BF16)
        vt_ref[:, rows] = qvo_t[D_MLSTM:2 * D_MLSTM].astype(BF16)
        sot_ref[:, rows] = jax.nn.sigmoid(qvo_t[2 * D_MLSTM:3 * D_MLSTM])

    starts = range(0, tt, sub)
    normed = [prep(r0) for r0 in starts]
    for r0, hb in zip(starts, normed):
        project(r0, hb)


def _in_proj_call(x2d, mod3, mod_row, row_len, tt, w):
    n_tok = x2d.shape[0]
    n_tiles = n_tok // tt
    const = lambda i: (0, 0)
    tok = lambda i: (i, 0)
    tok_t = lambda i: (0, i)
    weights = [w["norm1"], w["w_sc"], w["w_k"], w["w_qvot"], w["w_g"], w["b_g"], w["conv_sc_w"], w["conv_sc_b"]]
    out_shapes = (
        jax.ShapeDtypeStruct((n_tok, D_CONV), BF16),
        jax.ShapeDtypeStruct((D_MLSTM, n_tok), BF16),
        jax.ShapeDtypeStruct((n_tok, D_MLSTM), BF16),
        jax.ShapeDtypeStruct((D_MLSTM, n_tok), BF16),
        jax.ShapeDtypeStruct((D_MLSTM, n_tok), F32),
        jax.ShapeDtypeStruct((n_tok, LANES), F32),
        jax.ShapeDtypeStruct((n_tok // CHUNK, 3 * N_CHAINS, CHUNK), F32),
    )
    return pl.pallas_call(
        functools.partial(_in_proj_kernel, row_len),
        grid=(n_tiles,),
        in_specs=[
            pl.BlockSpec((tt, D_MODEL), tok),
            pl.BlockSpec((None, 6, D_MODEL), lambda i: (mod_row(i), 0, 0)),
        ] + [pl.BlockSpec(a.shape, const) for a in weights],
        out_specs=[
            pl.BlockSpec((tt, D_CONV), tok),
            pl.BlockSpec((D_MLSTM, tt), tok_t),
            pl.BlockSpec((tt, D_MLSTM), tok),
            pl.BlockSpec((D_MLSTM, tt), tok_t),
            pl.BlockSpec((D_MLSTM, tt), tok_t),
            pl.BlockSpec((tt, LANES), tok),
            pl.BlockSpec((tt // CHUNK, 3 * N_CHAINS, CHUNK), lambda i: (i, 0, 0)),
        ],
        out_shape=out_shapes,
        compiler_params=_cparams(1),
        name="in_proj",
    )(x2d, mod3, *weights)


def _mlstm_kernel(cg, has_init, want_state, *refs):
    refs = list(refs)
    fwd_refs, bwd_refs = refs[0:5], refs[5:10]
    refs = refs[10:]
    if has_init:
        c0_ref, n0_ref, m0_ref = refs[:3]
        refs = refs[3:]
    htf_ref, htb_ref = refs[:2]
    refs = refs[2:]
    if want_state:
        cout_ref, nout_ref, mout_ref = refs[:3]
        refs = refs[3:]
    e_ref, m_ref = refs

    g = pl.program_id(1)

    @pl.when(g == 0)
    def _():
        if has_init:
            for d in range(2):
                for hd in range(N_HEADS):
                    j = d * N_HEADS + hd
                    e_ref[j, 0:HEAD_DIM] = c0_ref[d, hd]
                    e_ref[j, HEAD_DIM:] = jnp.broadcast_to(n0_ref[d, hd], (BF16_ROWS, HEAD_DIM))
                    m_ref[j] = jnp.broadcast_to(m0_ref[d, hd], (SUBLANES, LANES))
        else:
            e_ref[...] = jnp.zeros(e_ref.shape, F32)
            m_ref[...] = jnp.zeros(m_ref.shape, F32)

    s_idx = lax.broadcasted_iota(jnp.int32, (CHUNK, CHUNK), 0)
    t_idx = lax.broadcasted_iota(jnp.int32, (CHUNK, CHUNK), 1)
    ones_rows = jnp.ones((BF16_ROWS, CHUNK), BF16)

    def stage_scores(c, d, hd):
        qt_ref, k_ref, vt_ref, row_ref, col_ref = fwd_refs if d == 0 else bwd_refs
        j = d * N_HEADS + hd
        toks = slice(c * CHUNK, (c + 1) * CHUNK)
        feat = slice(hd * HEAD_DIM, (hd + 1) * HEAD_DIM)
        last = CHUNK - 1 if d == 0 else 0

        qtc = qt_ref[feat, toks]
        kc = k_ref[toks, feat]
        vtc = vt_ref[feat, toks]
        cm_row = row_ref[c, j:j + 1, :]
        r_row = row_ref[c, N_CHAINS + j:N_CHAINS + j + 1, :]
        b_row = row_ref[c, 2 * N_CHAINS + j:2 * N_CHAINS + j + 1, :]
        m_row = m_ref[j, 0:1, :]
        m_t = m_row[:, 0:1]
        big_m_row = jnp.maximum(m_t, cm_row)
        m_last = big_m_row[:, last:last + 1]

        e = e_ref[j]
        kq = _dot(jnp.concatenate([kc, e.astype(BF16)], axis=0), qtc)

        wg = jnp.exp(r_row - m_last)
        lhs = jnp.concatenate([(vtc.astype(F32) * wg).astype(BF16),
                               jnp.broadcast_to(wg, (BF16_ROWS, CHUNK)).astype(BF16)], axis=0)
        e_ref[j] = jnp.exp(m_row - m_last) * e + _dot(lhs, kc)
        m_ref[j] = jnp.broadcast_to(b_row[:, last:last + 1] + m_last, (SUBLANES, LANES))
        return kq, vtc, big_m_row, m_t, b_row

    def stage_output(c, d, hd, kq, vtc, big_m_row, m_t, b_row):
        col_ref = (fwd_refs if d == 0 else bwd_refs)[4]
        ht_ref = htf_ref if d == 0 else htb_ref
        j = d * N_HEADS + hd
        toks = slice(c * CHUNK, (c + 1) * CHUNK)
        feat = slice(hd * HEAD_DIM, (hd + 1) * HEAD_DIM)
        mask = (s_idx <= t_idx) if d == 0 else (s_idx >= t_idx)
        r_col = col_ref[toks, N_CHAINS + j:N_CHAINS + j + 1]

        w_t = jnp.exp(jnp.where(mask, r_col - big_m_row, -jnp.inf))
        s_t = (kq[0:CHUNK] * w_t).astype(BF16)
        num_t = _dot(jnp.concatenate([vtc, ones_rows], axis=0), s_t)
        nd = num_t + jnp.exp(m_t - big_m_row) * kq[CHUNK:]
        inv = 1.0 / jnp.maximum(jnp.abs(nd[HEAD_DIM:HEAD_DIM + 1]), jnp.exp(-(b_row + big_m_row)))
        ht_ref[feat, toks] = nd[0:HEAD_DIM] * inv

    tasks = [(c if d == 0 else cg - 1 - c, d, hd) for c in range(cg) for d in range(2) for hd in range(N_HEADS)]
    pending = []
    for task in tasks:
        pending.append(task + stage_scores(*task))
        if len(pending) > SKEW:
            stage_output(*pending.pop(0))
    for args in pending:
        stage_output(*args)

    if want_state:
        @pl.when(g == pl.num_programs(1) - 1)
        def _():
            for d in range(2):
                for hd in range(N_HEADS):
                    j = d * N_HEADS + hd
                    cout_ref[d, hd] = e_ref[j, 0:HEAD_DIM]
                    nout_ref[d, hd] = e_ref[j, HEAD_DIM:HEAD_DIM + 1]
                    mout_ref[d, hd] = m_ref[j, 0:1, :]


def _mlstm_call(bsz, t, cg, qt, k, vt, gcol, grow, init, want_state):
    n_tok = bsz * t
    tg = cg * CHUNK
    n_groups = t // tg
    fwd = lambda b, g: b * n_groups + g
    bwd = lambda b, g: b * n_groups + (n_groups - 1 - g)

    def stream_specs(pos):
        return [
            pl.BlockSpec((D_MLSTM, tg), lambda b, g: (0, pos(b, g))),
            pl.BlockSpec((tg, D_MLSTM), lambda b, g: (pos(b, g), 0)),
            pl.BlockSpec((D_MLSTM, tg), lambda b, g: (0, pos(b, g))),
            pl.BlockSpec((cg, 3 * N_CHAINS, CHUNK), lambda b, g: (pos(b, g), 0, 0)),
            pl.BlockSpec((tg, LANES), lambda b, g: (pos(b, g), 0)),
        ]

    in_specs = stream_specs(fwd) + stream_specs(bwd)
    args = [qt, k, vt, grow, gcol] * 2
    if init is not None:
        c0, n0, m0 = init
        in_specs += [
            pl.BlockSpec((None, 2, N_HEADS, HEAD_DIM, HEAD_DIM), lambda b, g: (b, 0, 0, 0, 0)),
            pl.BlockSpec((None, 2, N_HEADS, 1, HEAD_DIM), lambda b, g: (b, 0, 0, 0, 0)),
            pl.BlockSpec((None, 2, N_HEADS, 1, LANES), lambda b, g: (b, 0, 0, 0, 0)),
        ]
        args += [c0, n0, m0]
    out_specs = [
        pl.BlockSpec((D_MLSTM, tg), lambda b, g: (0, fwd(b, g))),
        pl.BlockSpec((D_MLSTM, tg), lambda b, g: (0, bwd(b, g))),
    ]
    out_shape = [jax.ShapeDtypeStruct((D_MLSTM, n_tok), F32)] * 2
    if want_state:
        out_specs += [
            pl.BlockSpec((None, None, 2, N_HEADS, HEAD_DIM, HEAD_DIM), lambda b, g: (b, 0, 0, 0, 0, 0)),
            pl.BlockSpec((None, None, 2, N_HEADS, 1, HEAD_DIM), lambda b, g: (b, 0, 0, 0, 0, 0)),
            pl.BlockSpec((None, None, 2, N_HEADS, 1, LANES), lambda b, g: (b, 0, 0, 0, 0, 0)),
        ]
        out_shape += [
            jax.ShapeDtypeStruct((bsz, 1, 2, N_HEADS, HEAD_DIM, HEAD_DIM), F32),
            jax.ShapeDtypeStruct((bsz, 1, 2, N_HEADS, 1, HEAD_DIM), F32),
            jax.ShapeDtypeStruct((bsz, 1, 2, N_HEADS, 1, LANES), F32),
        ]
    return pl.pallas_call(
        functools.partial(_mlstm_kernel, cg, init is not None, want_state),
        grid=(bsz, n_groups),
        in_specs=in_specs,
        out_specs=out_specs,
        out_shape=out_shape,
        scratch_shapes=[
            pltpu.VMEM((N_CHAINS, E_ROWS, HEAD_DIM), F32),
            pltpu.VMEM((N_CHAINS, SUBLANES, LANES), F32),
        ],
        compiler_params=_cparams(2),
        name="mlstm",
    )(*args)


def _out_proj_kernel(x_ref, ysc_ref, htf_ref, htb_ref, sot_ref, mh_ref, mod_ref, n2_ref, wa_ref, wb_ref,
                     x1_ref, h2_ref):
    mod = mod_ref[...]
    gate1, shift2, scale2 = mod[2:3], mod[3:4], mod[4:5]
    hs = htf_ref[...] + htb_ref[...]
    heads = []
    for hd in range(N_HEADS):
        blk = hs[hd * HEAD_DIM:(hd + 1) * HEAD_DIM]
        heads.append(blk * lax.rsqrt(jnp.mean(blk * blk, axis=0, keepdims=True) + EPS))
    hm_t = ((jnp.concatenate(heads, axis=0) * mh_ref[...]) * sot_ref[...]).astype(BF16)
    mix = _dot(ysc_ref[...], wa_ref[...]) + _dot_tn(hm_t, wb_ref[...])
    x1 = x_ref[...] + gate1 * mix
    x1_ref[...] = x1
    h2_ref[...] = ((_rms_scale(x1) * n2_ref[...]) * (1.0 + scale2) + shift2).astype(BF16)


def _out_proj_call(x2d, ysc, htf, htb, sot, mod3, mod_row, tt, w):
    n_tok = x2d.shape[0]
    const = lambda i: (0, 0)
    tok = lambda i: (i, 0)
    tok_t = lambda i: (0, i)
    return pl.pallas_call(
        _out_proj_kernel,
        grid=(n_tok // tt,),
        in_specs=[
            pl.BlockSpec((tt, D_MODEL), tok),
            pl.BlockSpec((tt, D_CONV), tok),
            pl.BlockSpec((D_MLSTM, tt), tok_t),
            pl.BlockSpec((D_MLSTM, tt), tok_t),
            pl.BlockSpec((D_MLSTM, tt), tok_t),
            pl.BlockSpec((D_MLSTM, 1), const),
            pl.BlockSpec((None, 6, D_MODEL), lambda i: (mod_row(i), 0, 0)),
            pl.BlockSpec((1, D_MODEL), const),
            pl.BlockSpec(w["w_out_a"].shape, const),
            pl.BlockSpec(w["w_out_b"].shape, const),
        ],
        out_specs=[pl.BlockSpec((tt, D_MODEL), tok), pl.BlockSpec((tt, D_MODEL), tok)],
        out_shape=[jax.ShapeDtypeStruct((n_tok, D_MODEL), F32),
                   jax.ShapeDtypeStruct((n_tok, D_MODEL), BF16)],
        compiler_params=_cparams(1),
        name="out_proj",
    )(x2d, ysc, htf, htb, sot, w["mh_norm_col"], mod3, w["norm2"], w["w_out_a"], w["w_out_b"])


def _ffn_kernel(shift, seq_len, h2_ref, x1_ref, mod_ref, fn_ref, wu_ref, cw_ref, cb_ref, wd_ref, y_ref,
                act_ref, a0_ref, a1_ref, a2_ref, g0_ref, g1_ref, g2_ref, h2s_ref):
    slots = ((a0_ref, g0_ref), (a1_ref, g1_ref), (a2_ref, g2_ref))
    tt = act_ref.shape[0]
    pad = _ffn_pad(shift)
    nr = CONV_ROWS
    row = lax.broadcasted_iota(jnp.int32, (nr, FF_BLK), 0)

    def up_proj(blk, slot):
        a_ref, g_ref = slots[slot]
        off_a = pl.multiple_of(blk * FF_BLK, FF_BLK)
        off_g = pl.multiple_of(blk * FF_BLK + D_FF, FF_BLK)
        hin = h2s_ref[...]
        a_ref[pad:pad + tt] = _dot(hin, wu_ref[:, pl.ds(off_a, FF_BLK)])
        g_ref[pad:pad + tt] = _dot(hin, wu_ref[:, pl.ds(off_g, FF_BLK)])

    def conv_act(blk, slot):
        a_ref, g_ref = slots[slot]
        off_a = pl.multiple_of(blk * FF_BLK, FF_BLK)
        off_g = pl.multiple_of(blk * FF_BLK + D_FF, FF_BLK)

        def conv(z_ref, off, r0):
            cw = cw_ref[:, pl.ds(off, FF_BLK)]
            cb = cb_ref[:, pl.ds(off, FF_BLK)]
            prev = z_ref[pad - shift + r0:pad - shift + r0 + nr]
            nxt = z_ref[pad + shift + r0:pad + shift + r0 + nr]
            if seq_len is not None:
                if r0 % seq_len == 0:
                    prev = jnp.where(row == 0, 0.0, prev)
                if (r0 + nr) % seq_len == 0:
                    nxt = jnp.where(row == nr - 1, 0.0, nxt)
            return cw[0:1] * prev + cw[1:2] * z_ref[pad + r0:pad + r0 + nr] + cw[2:3] * nxt + cb

        for r0 in range(0, tt, nr):
            ac = conv(a_ref, off_a, r0)
            gc = conv(g_ref, off_g, r0)
            act_ref[r0:r0 + nr, pl.ds(off_a, FF_BLK)] = (gc * jax.nn.sigmoid(gc) * ac).astype(BF16)

    h2s_ref[...] = h2_ref[...].reshape(tt, D_MODEL)
    for a_ref, g_ref in slots:
        for z_ref in (a_ref, g_ref):
            z_ref[0:pad] = jnp.zeros((pad, FF_BLK), F32)
            z_ref[pad + tt:] = jnp.zeros((pad, FF_BLK), F32)

    n_slots = len(slots)
    for blk in range(SKEW_FF):
        up_proj(blk, blk % n_slots)

    def block_group(p, carry):
        for s in range(n_slots):
            blk = n_slots * p + SKEW_FF + s
            up_proj(blk, (SKEW_FF + s) % n_slots)
            conv_act(blk - SKEW_FF, s % n_slots)
        return carry

    n_groups = (N_FF_BLK - SKEW_FF) // n_slots
    lax.fori_loop(0, n_groups, block_group, 0)
    done = n_slots * n_groups + SKEW_FF
    for blk in range(done, N_FF_BLK):
        up_proj(blk, blk % n_slots)
        conv_act(blk - SKEW_FF, (blk - SKEW_FF) % n_slots)

    k0 = (N_FF_BLK - SKEW_FF) * FF_BLK
    part = _dot(act_ref[:, 0:k0], wd_ref[0:k0, :])
    for blk in range(N_FF_BLK - SKEW_FF, N_FF_BLK):
        conv_act(blk, blk % n_slots)
    ffn = part + _dot(act_ref[:, k0:], wd_ref[k0:, :])
    x2 = x1_ref[...].reshape(tt, D_MODEL) + mod_ref[5:6] * ffn
    y_ref[...] = (_rms_scale(x2) * fn_ref[...]).reshape(y_ref.shape)


def _ffn_pad(shift):
    return -(-shift // SUBLANES) * SUBLANES


def _ffn_call(h2, x1, mod3, mod_row_of_batch, grid_w, w):
    bsz, t, _ = x1.shape
    const = lambda i: (0, 0)
    resident = dict(pipeline_mode=pl.Buffered(1))
    if grid_w is None:
        tt, shift, seq_len = FFN_TILE, 1, t
        per_tile = tt // t
        blk = (per_tile, t, D_MODEL)
        n_tiles = bsz // per_tile
        tile_idx = lambda i: (i, 0, 0)
        batch_of = lambda i: i * per_tile
        as_blocks = lambda a: a
    else:
        rows = t // grid_w
        tt, shift, seq_len = rows * FFN_COLS, FFN_COLS, None
        strips = grid_w // FFN_COLS
        blk = (None, rows, FFN_COLS, D_MODEL)
        n_tiles = bsz * strips
        tile_idx = lambda i: (i // strips, 0, i % strips, 0)
        batch_of = lambda i: i // strips
        as_blocks = lambda a: a.reshape(bsz, rows, grid_w, D_MODEL)
    up_rows = tt + 2 * _ffn_pad(shift)
    scratch = ([pltpu.VMEM((tt, D_FF), BF16)] + [pltpu.VMEM((up_rows, FF_BLK), F32)] * (2 * (SKEW_FF + 1))
               + [pltpu.VMEM((tt, D_MODEL), BF16)])
    in_specs = [
        pl.BlockSpec(blk, tile_idx),
        pl.BlockSpec(blk, tile_idx),
        pl.BlockSpec((None, 6, D_MODEL), lambda i: (mod_row_of_batch(batch_of(i)), 0, 0)),
        pl.BlockSpec((1, D_MODEL), const),
        pl.BlockSpec((D_MODEL, 2 * D_FF), const, **resident),
        pl.BlockSpec((3, 2 * D_FF), const),
        pl.BlockSpec((1, 2 * D_FF), const),
        pl.BlockSpec((D_FF, D_MODEL), const, **resident),
    ]
    y = pl.pallas_call(
        functools.partial(_ffn_kernel, shift, seq_len),
        grid=(n_tiles,),
        in_specs=in_specs,
        out_specs=pl.BlockSpec(blk, tile_idx),
        out_shape=jax.ShapeDtypeStruct(as_blocks(x1).shape, F32),
        scratch_shapes=scratch,
        compiler_params=_cparams(1),
        name="ffn",
    )(as_blocks(h2), as_blocks(x1), mod3, w["final_norm"], w["w_up"], w["conv_ffn_w"], w["conv_ffn_b"],
      w["w_down"])
    return y.reshape(bsz, t, D_MODEL)


def _trunk(x, mod3, mod_row_of_batch, row_len, ffn_grid_w, init, want_state, tt_out, tt_in, cg, w):
    bsz, t, _ = x.shape
    x2d = x.reshape(bsz * t, D_MODEL)

    def mod_row(tile_tokens):
        return lambda i: mod_row_of_batch((i * tile_tokens) // t)

    ysc, qt, k, vt, sot, gcol, grow = _in_proj_call(x2d, mod3, mod_row(tt_in), row_len, tt_in, w)
    outs = _mlstm_call(bsz, t, cg, qt, k, vt, gcol, grow, init, want_state)
    x1, h2 = _out_proj_call(x2d, ysc, outs[0], outs[1], sot, mod3, mod_row(tt_out), tt_out, w)
    y = _ffn_call(h2.reshape(bsz, t, D_MODEL), x1.reshape(bsz, t, D_MODEL), mod3, mod_row_of_batch, ffn_grid_w, w)
    return y, outs[2:]


def kernel(x_prompt, x_sample, state_C, state_n, state_m, c, c_ctx, w_mod, b_mod, norm1, w_in, b_gate,
           conv_sc_w, conv_sc_b, mh_norm, w_out, norm2, w_up, conv_ffn_w, conv_ffn_b, w_down, final_norm):
    n_lat = c.shape[0]
    n_ctx = x_prompt.shape[0]
    c8 = jnp.concatenate([c, c_ctx[None], jnp.zeros((8 - n_lat - 1, D_MODEL), F32)], axis=0)
    l = 0
    mod3 = _mod_call(c8, w_mod[l], b_mod[l][None]).reshape(8, 6, D_MODEL)

    wi = w_in[l]
    q0 = 3 * D_CONV
    g0 = q0 + 4 * D_MLSTM
    zpad = jnp.zeros((D_MODEL, LANES - N_CHAINS), F32)
    cols = lambda a: wi[:, g0 + a * N_HEADS:g0 + (a + 1) * N_HEADS]
    w_g = jnp.concatenate([cols(0), cols(2), zpad, cols(1), cols(3), zpad], axis=1).astype(BF16)
    bpad = jnp.zeros((LANES - N_CHAINS,), F32)
    bg = b_gate[l].astype(F32)
    b_g = jnp.concatenate([bg[0], bg[2], bpad, bg[1], bg[3], bpad])[None]
    w = dict(
        norm1=norm1[l][None], norm2=norm2[l][None], final_norm=final_norm[None],
        w_sc=wi[:, 0:q0].astype(BF16),
        w_k=wi[:, q0 + D_MLSTM:q0 + 2 * D_MLSTM].astype(BF16),
        w_qvot=jnp.concatenate([wi[:, q0:q0 + D_MLSTM], wi[:, q0 + 2 * D_MLSTM:g0]], axis=1).T.astype(BF16),
        w_g=w_g, b_g=b_g,
        conv_sc_w=conv_sc_w[l], conv_sc_b=conv_sc_b[l][None],
        mh_norm_col=mh_norm[l][:, None],
        w_out_a=w_out[l][0:D_CONV].astype(BF16), w_out_b=w_out[l][D_CONV:].astype(BF16),
        w_up=w_up[l].astype(BF16), conv_ffn_w=conv_ffn_w[l], conv_ffn_b=conv_ffn_b[l][None],
        w_down=w_down[l].astype(BF16),
    )

    seq = x_prompt.shape[1]
    y_prompt, (new_c, new_n, new_m) = _trunk(
        x_prompt, mod3, lambda b: n_lat, seq, None, None, True, 512, 512, 1, w)

    init = (state_C[:, l], state_n[:, l][:, :, :, None, :],
            jnp.broadcast_to(state_m[:, l][:, :, :, None, None], state_m[:, l].shape + (1, LANES)))
    y_sample, _ = _trunk(x_sample, mod3, lambda b: b, GRID_W, GRID_W, init, False, 512, 512, 2, w)

    new_n = new_n.reshape(n_ctx, 1, 2, N_HEADS, HEAD_DIM)
    new_m = new_m[..., 0, 0]
    return y_prompt, y_sample, new_c, new_n, new_m
```

```python
import functools

import jax
import jax.numpy as jnp
from jax import lax
from jax.experimental import pallas as pl
from jax.experimental.pallas import tpu as pltpu

F32 = jnp.float32
BF16 = jnp.bfloat16

D_MODEL = 1024
GRID_W = 64
D_CONV = 512
D_MLSTM = 512
N_HEADS = 4
HEAD_DIM = 128
D_FF = 2816
CHUNK = 256
EPS = 1e-6

LANES = 128
SUBLANES = 8
BF16_ROWS = 16
FF_BLK = 256
N_FF_BLK = D_FF // FF_BLK
CONV_ROWS = 128
FFN_TILE = 1024
FFN_COLS = 16
VMEM_LIMIT = 60 * 1024 * 1024
N_CHAINS = 2 * N_HEADS
E_ROWS = HEAD_DIM + BF16_ROWS
SKEW = 3
IN_SUB = 512
SKEW_FF = 2


def _cparams(n_axes):
    return pltpu.CompilerParams(
        dimension_semantics=("arbitrary",) * n_axes, vmem_limit_bytes=VMEM_LIMIT)


def _rms_scale(x):
    return x * lax.rsqrt(jnp.mean(x * x, axis=-1, keepdims=True) + EPS)


def _dot(a, b):
    return jnp.dot(a, b, preferred_element_type=F32)


def _dot_nt(a, b):
    return lax.dot_general(a, b, (((1,), (1,)), ((), ())), preferred_element_type=F32)


def _dot_tn(a, b):
    return lax.dot_general(a, b, (((0,), (0,)), ((), ())), preferred_element_type=F32)


def _mod_kernel(c_ref, w_ref, b_ref, o_ref):
    c = c_ref[...]
    s = c * jax.nn.sigmoid(c)
    o_ref[...] = _dot(s.astype(BF16), w_ref[...].astype(BF16)) + b_ref[...]


def _mod_call(c8, w_mod, b_mod):
    n_out = w_mod.shape[1]
    blk = 1024
    return pl.pallas_call(
        _mod_kernel,
        grid=(n_out // blk,),
        in_specs=[
            pl.BlockSpec((8, D_MODEL), lambda i: (0, 0)),
            pl.BlockSpec((D_MODEL, blk), lambda i: (0, i)),
            pl.BlockSpec((1, blk), lambda i: (0, i)),
        ],
        out_specs=pl.BlockSpec((8, blk), lambda i: (0, i)),
        out_shape=jax.ShapeDtypeStruct((8, n_out), F32),
        compiler_params=_cparams(1),
        name="mod",
    )(c8, w_mod, b_mod)


def _chunk_scan(x, pos, op, fill, reverse):
    n = x.shape[0]
    k = 1
    while k < CHUNK:
        if reverse:
            shifted = pltpu.roll(x, n - k, axis=0)
            ok = pos < CHUNK - k
        else:
            shifted = pltpu.roll(x, k, axis=0)
            ok = pos >= k
        x = op(x, jnp.where(ok, shifted, fill))
        k *= 2
    return x


def _in_proj_kernel(row_len, x_ref, mod_ref, n1_ref, wsc_ref, wk_ref, wqvot_ref, wg_ref, bg_ref, cw_ref, cb_ref,
                    ysc_ref, qt_ref, k_ref, vt_ref, sot_ref, gcol_ref, grow_ref):
    tt = x_ref.shape[0]
    sub = min(tt, IN_SUB)
    mod = mod_ref[...]
    shift1, scale1 = mod[0:1], mod[1:2]

    def prep(r0):
        x = x_ref[r0:r0 + sub]
        h = (_rms_scale(x) * n1_ref[...]) * (1.0 + scale1) + shift1
        return h.astype(BF16)

    def project(r0, hb):
        rows = slice(r0, r0 + sub)
        g = _dot(hb, wg_ref[...]) + bg_ref[...]
        gi = g[:, 0:LANES]
        gf = g[:, LANES:2 * LANES]
        logf = jnp.minimum(gf, 0.0) - jnp.log1p(jnp.exp(-jnp.abs(gf)))
        lane = lax.broadcasted_iota(jnp.int32, gi.shape, 1)
        cpos = lax.broadcasted_iota(jnp.int32, gi.shape, 0) % CHUNK
        fwd = lane < N_HEADS
        b = jnp.where(fwd, _chunk_scan(logf, cpos, jnp.add, 0.0, False),
                      _chunk_scan(logf, cpos, jnp.add, 0.0, True))
        r = gi - b
        cm = jnp.where(fwd, _chunk_scan(r, cpos, jnp.maximum, -jnp.inf, False),
                       _chunk_scan(r, cpos, jnp.maximum, -jnp.inf, True))
        packed = jnp.where(lane < N_CHAINS, cm,
                           jnp.where(lane < 2 * N_CHAINS, pltpu.roll(r, N_CHAINS, axis=1),
                                     pltpu.roll(b, 2 * N_CHAINS, axis=1)))
        gcol_ref[rows] = packed
        for c in range(sub // CHUNK):
            grow_ref[r0 // CHUNK + c] = packed[c * CHUNK:(c + 1) * CHUNK].T[0:3 * N_CHAINS]

        zb = _dot(hb, wsc_ref[:, 0:D_CONV])
        zc = _dot(hb, wsc_ref[:, D_CONV:2 * D_CONV])
        zx = _dot(hb, wsc_ref[:, 2 * D_CONV:3 * D_CONV])
        u = zc * zx
        pos = lax.broadcasted_iota(jnp.int32, u.shape, 0) % row_len
        prev = jnp.where(pos == 0, 0.0, pltpu.roll(u, 1, axis=0))
        nxt = jnp.where(pos == row_len - 1, 0.0, pltpu.roll(u, sub - 1, axis=0))
        cw = cw_ref[...]
        conv = cw[0:1] * prev + cw[1:2] * u + cw[2:3] * nxt + cb_ref[...]
        ysc_ref[rows] = (zb * conv).astype(BF16)

        k_ref[rows] = _dot(hb, wk_ref[...]).astype(BF16)
        qvo_t = _dot_nt(wqvot_ref[...], hb)
        qt_ref[:, rows] = (qvo_t[0:D_MLSTM] * (HEAD_DIM ** -0.5)).astype(BF16)
        vt_ref[:, rows] = qvo_t[D_MLSTM:2 * D_MLSTM].astype(BF16)
        sot_ref[:, rows] = jax.nn.sigmoid(qvo_t[2 * D_MLSTM:3 * D_MLSTM]).astype(BF16)

    starts = range(0, tt, sub)
    normed = [prep(r0) for r0 in starts]
    for r0, hb in zip(starts, normed):
        project(r0, hb)


def _in_proj_call(x2d, mod3, mod_row, row_len, tt, w):
    n_tok = x2d.shape[0]
    n_tiles = n_tok // tt
    const = lambda i: (0, 0)
    tok = lambda i: (i, 0)
    tok_t = lambda i: (0, i)
    weights = [w["norm1"], w["w_sc"], w["w_k"], w["w_qvot"], w["w_g"], w["b_g"], w["conv_sc_w"], w["conv_sc_b"]]
    out_shapes = (
        jax.ShapeDtypeStruct((n_tok, D_CONV), BF16),
        jax.ShapeDtypeStruct((D_MLSTM, n_tok), BF16),
        jax.ShapeDtypeStruct((n_tok, D_MLSTM), BF16),
        jax.ShapeDtypeStruct((D_MLSTM, n_tok), BF16),
        jax.ShapeDtypeStruct((D_MLSTM, n_tok), BF16),
        jax.ShapeDtypeStruct((n_tok, LANES), F32),
        jax.ShapeDtypeStruct((n_tok // CHUNK, 3 * N_CHAINS, CHUNK), F32),
    )
    return pl.pallas_call(
        functools.partial(_in_proj_kernel, row_len),
        grid=(n_tiles,),
        in_specs=[
            pl.BlockSpec((tt, D_MODEL), tok),
            pl.BlockSpec((None, 6, D_MODEL), lambda i: (mod_row(i), 0, 0)),
        ] + [pl.BlockSpec(a.shape, const) for a in weights],
        out_specs=[
            pl.BlockSpec((tt, D_CONV), tok),
            pl.BlockSpec((D_MLSTM, tt), tok_t),
            pl.BlockSpec((tt, D_MLSTM), tok),
            pl.BlockSpec((D_MLSTM, tt), tok_t),
            pl.BlockSpec((D_MLSTM, tt), tok_t),
            pl.BlockSpec((tt, LANES), tok),
            pl.BlockSpec((tt // CHUNK, 3 * N_CHAINS, CHUNK), lambda i: (i, 0, 0)),
        ],
        out_shape=out_shapes,
        compiler_params=_cparams(1),
        name="in_proj",
    )(x2d, mod3, *weights)


def _mlstm_kernel(cg, has_init, want_state, *refs):
    refs = list(refs)
    fwd_refs, bwd_refs = refs[0:5], refs[5:10]
    refs = refs[10:]
    if has_init:
        c0_ref, n0_ref, m0_ref = refs[:3]
        refs = refs[3:]
    htf_ref, htb_ref = refs[:2]
    refs = refs[2:]
    if want_state:
        cout_ref, nout_ref, mout_ref = refs[:3]
        refs = refs[3:]
    e_ref, m_ref = refs

    g = pl.program_id(1)

    @pl.when(g == 0)
    def _():
        if has_init:
            for d in range(2):
                for hd in range(N_HEADS):
                    j = d * N_HEADS + hd
                    e_ref[j, 0:HEAD_DIM] = c0_ref[d, hd]
                    e_ref[j, HEAD_DIM:] = jnp.broadcast_to(n0_ref[d, hd], (BF16_ROWS, HEAD_DIM))
                    m_ref[j] = jnp.broadcast_to(m0_ref[d, hd], (SUBLANES, LANES))
        else:
            e_ref[...] = jnp.zeros(e_ref.shape, F32)
            m_ref[...] = jnp.zeros(m_ref.shape, F32)

    s_idx = lax.broadcasted_iota(jnp.int32, (CHUNK, CHUNK), 0)
    t_idx = lax.broadcasted_iota(jnp.int32, (CHUNK, CHUNK), 1)
    ones_rows = jnp.ones((BF16_ROWS, CHUNK), BF16)

    def stage_scores(c, d, hd):
        qt_ref, k_ref, vt_ref, row_ref, col_ref = fwd_refs if d == 0 else bwd_refs
        j = d * N_HEADS + hd
        toks = slice(c * CHUNK, (c + 1) * CHUNK)
        feat = slice(hd * HEAD_DIM, (hd + 1) * HEAD_DIM)
        last = CHUNK - 1 if d == 0 else 0

        qtc = qt_ref[feat, toks]
        kc = k_ref[toks, feat]
        vtc = vt_ref[feat, toks]
        cm_row = row_ref[c, j:j + 1, :]
        r_row = row_ref[c, N_CHAINS + j:N_CHAINS + j + 1, :]
        b_row = row_ref[c, 2 * N_CHAINS + j:2 * N_CHAINS + j + 1, :]
        m_row = m_ref[j, 0:1, :]
        m_t = m_row[:, 0:1]
        big_m_row = jnp.maximum(m_t, cm_row)
        m_last = big_m_row[:, last:last + 1]

        e = e_ref[j]
        kq = _dot(jnp.concatenate([kc, e.astype(BF16)], axis=0), qtc)

        wg = jnp.exp(r_row - m_last)
        lhs = jnp.concatenate([(vtc.astype(F32) * wg).astype(BF16),
                               jnp.broadcast_to(wg, (BF16_ROWS, CHUNK)).astype(BF16)], axis=0)
        e_ref[j] = jnp.exp(m_row - m_last) * e + _dot(lhs, kc)
        m_ref[j] = jnp.broadcast_to(b_row[:, last:last + 1] + m_last, (SUBLANES, LANES))
        return kq, vtc, big_m_row, m_t, b_row

    def stage_output(c, d, hd, kq, vtc, big_m_row, m_t, b_row):
        col_ref = (fwd_refs if d == 0 else bwd_refs)[4]
        ht_ref = htf_ref if d == 0 else htb_ref
        j = d * N_HEADS + hd
        toks = slice(c * CHUNK, (c + 1) * CHUNK)
        feat = slice(hd * HEAD_DIM, (hd + 1) * HEAD_DIM)
        mask = (s_idx <= t_idx) if d == 0 else (s_idx >= t_idx)
        r_col = col_ref[toks, N_CHAINS + j:N_CHAINS + j + 1]

        w_t = jnp.exp(jnp.where(mask, r_col - big_m_row, -jnp.inf))
        s_t = (kq[0:CHUNK] * w_t).astype(BF16)
        num_t = _dot(jnp.concatenate([vtc, ones_rows], axis=0), s_t)
        nd = num_t + jnp.exp(m_t - big_m_row) * kq[CHUNK:]
        inv = 1.0 / jnp.maximum(jnp.abs(nd[HEAD_DIM:HEAD_DIM + 1]), jnp.exp(-(b_row + big_m_row)))
        ht_ref[feat, toks] = (nd[0:HEAD_DIM] * inv).astype(BF16)

    tasks = [(c if d == 0 else cg - 1 - c, d, hd) for c in range(cg) for d in range(2) for hd in range(N_HEADS)]
    pending = []
    for task in tasks:
        pending.append(task + stage_scores(*task))
        if len(pending) > SKEW:
            stage_output(*pending.pop(0))
    for args in pending:
        stage_output(*args)

    if want_state:
        @pl.when(g == pl.num_programs(1) - 1)
        def _():
            for d in range(2):
                for hd in range(N_HEADS):
                    j = d * N_HEADS + hd
                    cout_ref[d, hd] = e_ref[j, 0:HEAD_DIM]
                    nout_ref[d, hd] = e_ref[j, HEAD_DIM:HEAD_DIM + 1]
                    mout_ref[d, hd] = m_ref[j, 0:1, :]


def _mlstm_call(bsz, t, cg, qt, k, vt, gcol, grow, init, want_state):
    n_tok = bsz * t
    tg = cg * CHUNK
    n_groups = t // tg
    fwd = lambda b, g: b * n_groups + g
    bwd = lambda b, g: b * n_groups + (n_groups - 1 - g)

    def stream_specs(pos):
        return [
            pl.BlockSpec((D_MLSTM, tg), lambda b, g: (0, pos(b, g))),
            pl.BlockSpec((tg, D_MLSTM), lambda b, g: (pos(b, g), 0)),
            pl.BlockSpec((D_MLSTM, tg), lambda b, g: (0, pos(b, g))),
            pl.BlockSpec((cg, 3 * N_CHAINS, CHUNK), lambda b, g: (pos(b, g), 0, 0)),
            pl.BlockSpec((tg, LANES), lambda b, g: (pos(b, g), 0)),
        ]

    in_specs = stream_specs(fwd) + stream_specs(bwd)
    args = [qt, k, vt, grow, gcol] * 2
    if init is not None:
        c0, n0, m0 = init
        in_specs += [
            pl.BlockSpec((None, 2, N_HEADS, HEAD_DIM, HEAD_DIM), lambda b, g: (b, 0, 0, 0, 0)),
            pl.BlockSpec((None, 2, N_HEADS, 1, HEAD_DIM), lambda b, g: (b, 0, 0, 0, 0)),
            pl.BlockSpec((None, 2, N_HEADS, 1, LANES), lambda b, g: (b, 0, 0, 0, 0)),
        ]
        args += [c0, n0, m0]
    out_specs = [
        pl.BlockSpec((D_MLSTM, tg), lambda b, g: (0, fwd(b, g))),
        pl.BlockSpec((D_MLSTM, tg), lambda b, g: (0, bwd(b, g))),
    ]
    out_shape = [jax.ShapeDtypeStruct((D_MLSTM, n_tok), BF16)] * 2
    if want_state:
        out_specs += [
            pl.BlockSpec((None, None, 2, N_HEADS, HEAD_DIM, HEAD_DIM), lambda b, g: (b, 0, 0, 0, 0, 0)),
            pl.BlockSpec((None, None, 2, N_HEADS, 1, HEAD_DIM), lambda b, g: (b, 0, 0, 0, 0, 0)),
            pl.BlockSpec((None, None, 2, N_HEADS, 1, LANES), lambda b, g: (b, 0, 0, 0, 0, 0)),
        ]
        out_shape += [
            jax.ShapeDtypeStruct((bsz, 1, 2, N_HEADS, HEAD_DIM, HEAD_DIM), F32),
            jax.ShapeDtypeStruct((bsz, 1, 2, N_HEADS, 1, HEAD_DIM), F32),
            jax.ShapeDtypeStruct((bsz, 1, 2, N_HEADS, 1, LANES), F32),
        ]
    return pl.pallas_call(
        functools.partial(_mlstm_kernel, cg, init is not None, want_state),
        grid=(bsz, n_groups),
        in_specs=in_specs,
        out_specs=out_specs,
        out_shape=out_shape,
        scratch_shapes=[
            pltpu.VMEM((N_CHAINS, E_ROWS, HEAD_DIM), F32),
            pltpu.VMEM((N_CHAINS, SUBLANES, LANES), F32),
        ],
        compiler_params=_cparams(2),
        name="mlstm",
    )(*args)


def _out_proj_kernel(x_ref, ysc_ref, htf_ref, htb_ref, sot_ref, mh_ref, mod_ref, n2_ref, wa_ref, wb_ref,
                     x1_ref, h2_ref):
    mod = mod_ref[...]
    gate1, shift2, scale2 = mod[2:3], mod[3:4], mod[4:5]
    hs = htf_ref[...].astype(F32) + htb_ref[...].astype(F32)
    heads = []
    for hd in range(N_HEADS):
        blk = hs[hd * HEAD_DIM:(hd + 1) * HEAD_DIM]
        heads.append(blk * lax.rsqrt(jnp.mean(blk * blk, axis=0, keepdims=True) + EPS))
    hm_t = ((jnp.concatenate(heads, axis=0) * mh_ref[...]) * sot_ref[...].astype(F32)).astype(BF16)
    mix = _dot(ysc_ref[...], wa_ref[...]) + _dot_tn(hm_t, wb_ref[...])
    x1 = x_ref[...] + gate1 * mix
    x1_ref[...] = x1
    h2_ref[...] = ((_rms_scale(x1) * n2_ref[...]) * (1.0 + scale2) + shift2).astype(BF16)


def _out_proj_call(x2d, ysc, htf, htb, sot, mod3, mod_row, tt, w):
    n_tok = x2d.shape[0]
    const = lambda i: (0, 0)
    tok = lambda i: (i, 0)
    tok_t = lambda i: (0, i)
    return pl.pallas_call(
        _out_proj_kernel,
        grid=(n_tok // tt,),
        in_specs=[
            pl.BlockSpec((tt, D_MODEL), tok),
            pl.BlockSpec((tt, D_CONV), tok),
            pl.BlockSpec((D_MLSTM, tt), tok_t),
            pl.BlockSpec((D_MLSTM, tt), tok_t),
            pl.BlockSpec((D_MLSTM, tt), tok_t),
            pl.BlockSpec((D_MLSTM, 1), const),
            pl.BlockSpec((None, 6, D_MODEL), lambda i: (mod_row(i), 0, 0)),
            pl.BlockSpec((1, D_MODEL), const),
            pl.BlockSpec(w["w_out_a"].shape, const),
            pl.BlockSpec(w["w_out_b"].shape, const),
        ],
        out_specs=[pl.BlockSpec((tt, D_MODEL), tok), pl.BlockSpec((tt, D_MODEL), tok)],
        out_shape=[jax.ShapeDtypeStruct((n_tok, D_MODEL), F32),
                   jax.ShapeDtypeStruct((n_tok, D_MODEL), BF16)],
        compiler_params=_cparams(1),
        name="out_proj",
    )(x2d, ysc, htf, htb, sot, w["mh_norm_col"], mod3, w["norm2"], w["w_out_a"], w["w_out_b"])


def _ffn_kernel(shift, seq_len, h2_ref, x1_ref, mod_ref, fn_ref, wu_ref, cw_ref, cb_ref, wd_ref, y_ref,
                act_ref, a0_ref, a1_ref, a2_ref, g0_ref, g1_ref, g2_ref, h2s_ref):
    slots = ((a0_ref, g0_ref), (a1_ref, g1_ref), (a2_ref, g2_ref))
    tt = act_ref.shape[0]
    pad = _ffn_pad(shift)
    nr = CONV_ROWS
    row = lax.broadcasted_iota(jnp.int32, (nr, FF_BLK), 0)

    def up_proj(blk, slot):
        a_ref, g_ref = slots[slot]
        off_a = pl.multiple_of(blk * FF_BLK, FF_BLK)
        off_g = pl.multiple_of(blk * FF_BLK + D_FF, FF_BLK)
        hin = h2s_ref[...]
        a_ref[pad:pad + tt] = _dot(hin, wu_ref[:, pl.ds(off_a, FF_BLK)])
        g_ref[pad:pad + tt] = _dot(hin, wu_ref[:, pl.ds(off_g, FF_BLK)])

    def conv_act(blk, slot):
        a_ref, g_ref = slots[slot]
        off_a = pl.multiple_of(blk * FF_BLK, FF_BLK)
        off_g = pl.multiple_of(blk * FF_BLK + D_FF, FF_BLK)

        def conv(z_ref, off, r0):
            cw = cw_ref[:, pl.ds(off, FF_BLK)]
            cb = cb_ref[:, pl.ds(off, FF_BLK)]
            prev = z_ref[pad - shift + r0:pad - shift + r0 + nr]
            nxt = z_ref[pad + shift + r0:pad + shift + r0 + nr]
            if seq_len is not None:
                if r0 % seq_len == 0:
                    prev = jnp.where(row == 0, 0.0, prev)
                if (r0 + nr) % seq_len == 0:
                    nxt = jnp.where(row == nr - 1, 0.0, nxt)
            return cw[0:1] * prev + cw[1:2] * z_ref[pad + r0:pad + r0 + nr] + cw[2:3] * nxt + cb

        for r0 in range(0, tt, nr):
            ac = conv(a_ref, off_a, r0)
            gc = conv(g_ref, off_g, r0)
            act_ref[r0:r0 + nr, pl.ds(off_a, FF_BLK)] = (gc * jax.nn.sigmoid(gc) * ac).astype(BF16)

    h2s_ref[...] = h2_ref[...].reshape(tt, D_MODEL)
    for a_ref, g_ref in slots:
        for z_ref in (a_ref, g_ref):
            z_ref[0:pad] = jnp.zeros((pad, FF_BLK), F32)
            z_ref[pad + tt:] = jnp.zeros((pad, FF_BLK), F32)

    n_slots = len(slots)
    for blk in range(SKEW_FF):
        up_proj(blk, blk % n_slots)

    def block_group(p, carry):
        for s in range(n_slots):
            blk = n_slots * p + SKEW_FF + s
            up_proj(blk, (SKEW_FF + s) % n_slots)
            conv_act(blk - SKEW_FF, s % n_slots)
        return carry

    n_groups = (N_FF_BLK - SKEW_FF) // n_slots
    lax.fori_loop(0, n_groups, block_group, 0)
    done = n_slots * n_groups + SKEW_FF
    for blk in range(done, N_FF_BLK):
        up_proj(blk, blk % n_slots)
        conv_act(blk - SKEW_FF, (blk - SKEW_FF) % n_slots)

    k0 = (N_FF_BLK - SKEW_FF) * FF_BLK
    part = _dot(act_ref[:, 0:k0], wd_ref[0:k0, :])
    for blk in range(N_FF_BLK - SKEW_FF, N_FF_BLK):
        conv_act(blk, blk % n_slots)
    ffn = part + _dot(act_ref[:, k0:], wd_ref[k0:, :])
    x2 = x1_ref[...].reshape(tt, D_MODEL) + mod_ref[5:6] * ffn
    y_ref[...] = (_rms_scale(x2) * fn_ref[...]).reshape(y_ref.shape)


def _ffn_pad(shift):
    return -(-shift // SUBLANES) * SUBLANES


def _ffn_call(h2, x1, mod3, mod_row_of_batch, grid_w, w):
    bsz, t, _ = x1.shape
    const = lambda i: (0, 0)
    resident = dict(pipeline_mode=pl.Buffered(1))
    if grid_w is None:
        tt, shift, seq_len = FFN_TILE, 1, t
        per_tile = tt // t
        blk = (per_tile, t, D_MODEL)
        n_tiles = bsz // per_tile
        tile_idx = lambda i: (i, 0, 0)
        batch_of = lambda i: i * per_tile
        as_blocks = lambda a: a
    else:
        rows = t // grid_w
        tt, shift, seq_len = rows * FFN_COLS, FFN_COLS, None
        strips = grid_w // FFN_COLS
        blk = (None, rows, FFN_COLS, D_MODEL)
        n_tiles = bsz * strips
        tile_idx = lambda i: (i // strips, 0, i % strips, 0)
        batch_of = lambda i: i // strips
        as_blocks = lambda a: a.reshape(bsz, rows, grid_w, D_MODEL)
    up_rows = tt + 2 * _ffn_pad(shift)
    scratch = ([pltpu.VMEM((tt, D_FF), BF16)] + [pltpu.VMEM((up_rows, FF_BLK), F32)] * (2 * (SKEW_FF + 1))
               + [pltpu.VMEM((tt, D_MODEL), BF16)])
    in_specs = [
        pl.BlockSpec(blk, tile_idx),
        pl.BlockSpec(blk, tile_idx),
        pl.BlockSpec((None, 6, D_MODEL), lambda i: (mod_row_of_batch(batch_of(i)), 0, 0)),
        pl.BlockSpec((1, D_MODEL), const),
        pl.BlockSpec((D_MODEL, 2 * D_FF), const, **resident),
        pl.BlockSpec((3, 2 * D_FF), const),
        pl.BlockSpec((1, 2 * D_FF), const),
        pl.BlockSpec((D_FF, D_MODEL), const, **resident),
    ]
    y = pl.pallas_call(
        functools.partial(_ffn_kernel, shift, seq_len),
        grid=(n_tiles,),
        in_specs=in_specs,
        out_specs=pl.BlockSpec(blk, tile_idx),
        out_shape=jax.ShapeDtypeStruct(as_blocks(x1).shape, F32),
        scratch_shapes=scratch,
        compiler_params=_cparams(1),
        name="ffn",
    )(as_blocks(h2), as_blocks(x1), mod3, w["final_norm"], w["w_up"], w["conv_ffn_w"], w["conv_ffn_b"],
      w["w_down"])
    return y.reshape(bsz, t, D_MODEL)


def _trunk(x, mod3, mod_row_of_batch, row_len, ffn_grid_w, init, want_state, tt_out, tt_in, cg, w):
    bsz, t, _ = x.shape
    x2d = x.reshape(bsz * t, D_MODEL)

    def mod_row(tile_tokens):
        return lambda i: mod_row_of_batch((i * tile_tokens) // t)

    ysc, qt, k, vt, sot, gcol, grow = _in_proj_call(x2d, mod3, mod_row(tt_in), row_len, tt_in, w)
    outs = _mlstm_call(bsz, t, cg, qt, k, vt, gcol, grow, init, want_state)
    x1, h2 = _out_proj_call(x2d, ysc, outs[0], outs[1], sot, mod3, mod_row(tt_out), tt_out, w)
    y = _ffn_call(h2.reshape(bsz, t, D_MODEL), x1.reshape(bsz, t, D_MODEL), mod3, mod_row_of_batch, ffn_grid_w, w)
    return y, outs[2:]


def kernel(x_prompt, x_sample, state_C, state_n, state_m, c, c_ctx, w_mod, b_mod, norm1, w_in, b_gate,
           conv_sc_w, conv_sc_b, mh_norm, w_out, norm2, w_up, conv_ffn_w, conv_ffn_b, w_down, final_norm):
    n_lat = c.shape[0]
    n_ctx = x_prompt.shape[0]
    c8 = jnp.concatenate([c, c_ctx[None], jnp.zeros((8 - n_lat - 1, D_MODEL), F32)], axis=0)
    l = 0
    mod3 = _mod_call(c8, w_mod[l], b_mod[l][None]).reshape(8, 6, D_MODEL)

    wi = w_in[l]
    q0 = 3 * D_CONV
    g0 = q0 + 4 * D_MLSTM
    zpad = jnp.zeros((D_MODEL, LANES - N_CHAINS), F32)
    cols = lambda a: wi[:, g0 + a * N_HEADS:g0 + (a + 1) * N_HEADS]
    w_g = jnp.concatenate([cols(0), cols(2), zpad, cols(1), cols(3), zpad], axis=1).astype(BF16)
    bpad = jnp.zeros((LANES - N_CHAINS,), F32)
    bg = b_gate[l].astype(F32)
    b_g = jnp.concatenate([bg[0], bg[2], bpad, bg[1], bg[3], bpad])[None]
    w = dict(
        norm1=norm1[l][None], norm2=norm2[l][None], final_norm=final_norm[None],
        w_sc=wi[:, 0:q0].astype(BF16),
        w_k=wi[:, q0 + D_MLSTM:q0 + 2 * D_MLSTM].astype(BF16),
        w_qvot=jnp.concatenate([wi[:, q0:q0 + D_MLSTM], wi[:, q0 + 2 * D_MLSTM:g0]], axis=1).T.astype(BF16),
        w_g=w_g, b_g=b_g,
        conv_sc_w=conv_sc_w[l], conv_sc_b=conv_sc_b[l][None],
        mh_norm_col=mh_norm[l][:, None],
        w_out_a=w_out[l][0:D_CONV].astype(BF16), w_out_b=w_out[l][D_CONV:].astype(BF16),
        w_up=w_up[l].astype(BF16), conv_ffn_w=conv_ffn_w[l], conv_ffn_b=conv_ffn_b[l][None],
        w_down=w_down[l].astype(BF16),
    )

    seq = x_prompt.shape[1]
    y_prompt, (new_c, new_n, new_m) = _trunk(
        x_prompt, mod3, lambda b: n_lat, seq, None, None, True, 512, 512, 1, w)

    init = (state_C[:, l], state_n[:, l][:, :, :, None, :],
            jnp.broadcast_to(state_m[:, l][:, :, :, None, None], state_m[:, l].shape + (1, LANES)))
    y_sample, _ = _trunk(x_sample, mod3, lambda b: b, GRID_W, GRID_W, init, False, 512, 512, 2, w)

    new_n = new_n.reshape(n_ctx, 1, 2, N_HEADS, HEAD_DIM)
    new_m = new_m[..., 0, 0]
    return y_prompt, y_sample, new_c, new_n, new_m
```

```python
import functools

import jax
import jax.numpy as jnp
from jax import lax
from jax.experimental import pallas as pl
from jax.experimental.pallas import tpu as pltpu

F32 = jnp.float32
BF16 = jnp.bfloat16

D_MODEL = 1024
GRID_W = 64
D_CONV = 512
D_MLSTM = 512
N_HEADS = 4
HEAD_DIM = 128
D_FF = 2816
CHUNK = 256
EPS = 1e-6

LANES = 128
SUBLANES = 8
BF16_ROWS = 16
FF_BLK = 256
N_FF_BLK = D_FF // FF_BLK
CONV_ROWS = 128
FFN_TILE = 1024
FFN_COLS = 16
VMEM_LIMIT = 60 * 1024 * 1024
N_CHAINS = 2 * N_HEADS
E_ROWS = HEAD_DIM + BF16_ROWS
SKEW = 3
IN_SUB = 512
SKEW_FF = 2


def _cparams(n_axes):
    return pltpu.CompilerParams(
        dimension_semantics=("arbitrary",) * n_axes, vmem_limit_bytes=VMEM_LIMIT)


def _rms_scale(x):
    return x * lax.rsqrt(jnp.mean(x * x, axis=-1, keepdims=True) + EPS)


def _dot(a, b):
    return jnp.dot(a, b, preferred_element_type=F32)


def _dot_nt(a, b):
    return lax.dot_general(a, b, (((1,), (1,)), ((), ())), preferred_element_type=F32)


def _dot_tn(a, b):
    return lax.dot_general(a, b, (((0,), (0,)), ((), ())), preferred_element_type=F32)


def _mod_kernel(c_ref, w_ref, b_ref, o_ref):
    c = c_ref[...]
    s = c * jax.nn.sigmoid(c)
    o_ref[...] = _dot(s.astype(BF16), w_ref[...].astype(BF16)) + b_ref[...]


def _mod_call(c8, w_mod, b_mod):
    n_out = w_mod.shape[1]
    blk = 1024
    return pl.pallas_call(
        _mod_kernel,
        grid=(n_out // blk,),
        in_specs=[
            pl.BlockSpec((8, D_MODEL), lambda i: (0, 0)),
            pl.BlockSpec((D_MODEL, blk), lambda i: (0, i)),
            pl.BlockSpec((1, blk), lambda i: (0, i)),
        ],
        out_specs=pl.BlockSpec((8, blk), lambda i: (0, i)),
        out_shape=jax.ShapeDtypeStruct((8, n_out), F32),
        compiler_params=_cparams(1),
        name="mod",
    )(c8, w_mod, b_mod)


def _chunk_scan(x, pos, op, fill, reverse):
    n = x.shape[1]
    k = 1
    while k < CHUNK:
        if reverse:
            shifted = pltpu.roll(x, n - k, axis=1)
            ok = pos < CHUNK - k
        else:
            shifted = pltpu.roll(x, k, axis=1)
            ok = pos >= k
        x = op(x, jnp.where(ok, shifted, fill))
        k *= 2
    return x


def _in_proj_kernel(row_len, x_ref, mod_ref, n1_ref, wsc_ref, wk_ref, wqvogt_ref, bg_ref, cw_ref, cb_ref,
                    ysc_ref, qt_ref, k_ref, vt_ref, sot_ref, rcol_ref, grow_ref):
    tt = x_ref.shape[0]
    sub = min(tt, IN_SUB)
    mod = mod_ref[...]
    shift1, scale1 = mod[0:1], mod[1:2]

    def prep(r0):
        x = x_ref[r0:r0 + sub]
        h = (_rms_scale(x) * n1_ref[...]) * (1.0 + scale1) + shift1
        return h.astype(BF16)

    def project(r0, hb):
        rows = slice(r0, r0 + sub)
        qvog_t = _dot_nt(wqvogt_ref[...], hb)
        qt_ref[:, rows] = (qvog_t[0:D_MLSTM] * (HEAD_DIM ** -0.5)).astype(BF16)
        vt_ref[:, rows] = qvog_t[D_MLSTM:2 * D_MLSTM].astype(BF16)
        sot_ref[:, rows] = jax.nn.sigmoid(qvog_t[2 * D_MLSTM:3 * D_MLSTM]).astype(BF16)

        g_t = qvog_t[3 * D_MLSTM:3 * D_MLSTM + 2 * N_CHAINS] + bg_ref[...]
        gi = g_t[0:N_CHAINS]
        gf = g_t[N_CHAINS:2 * N_CHAINS]
        logf = jnp.minimum(gf, 0.0) - jnp.log1p(jnp.exp(-jnp.abs(gf)))
        fwd = lax.broadcasted_iota(jnp.int32, gi.shape, 0) < N_HEADS
        cpos = lax.broadcasted_iota(jnp.int32, gi.shape, 1) % CHUNK
        b = jnp.where(fwd, _chunk_scan(logf, cpos, jnp.add, 0.0, False),
                      _chunk_scan(logf, cpos, jnp.add, 0.0, True))
        r = gi - b
        cm = jnp.where(fwd, _chunk_scan(r, cpos, jnp.maximum, -jnp.inf, False),
                       _chunk_scan(r, cpos, jnp.maximum, -jnp.inf, True))
        for c in range(sub // CHUNK):
            toks = slice(c * CHUNK, (c + 1) * CHUNK)
            grow_ref[r0 // CHUNK + c] = jnp.concatenate([cm[:, toks], r[:, toks], b[:, toks]], axis=0)
        rcol_ref[rows] = jnp.concatenate([r, jnp.zeros((LANES - N_CHAINS, sub), F32)], axis=0).T

        zb = _dot(hb, wsc_ref[:, 0:D_CONV])
        zc = _dot(hb, wsc_ref[:, D_CONV:2 * D_CONV])
        zx = _dot(hb, wsc_ref[:, 2 * D_CONV:3 * D_CONV])
        u = zc * zx
        pos = lax.broadcasted_iota(jnp.int32, u.shape, 0) % row_len
        prev = jnp.where(pos == 0, 0.0, pltpu.roll(u, 1, axis=0))
        nxt = jnp.where(pos == row_len - 1, 0.0, pltpu.roll(u, sub - 1, axis=0))
        cw = cw_ref[...]
        conv = cw[0:1] * prev + cw[1:2] * u + cw[2:3] * nxt + cb_ref[...]
        ysc_ref[rows] = (zb * conv).astype(BF16)

        k_ref[rows] = _dot(hb, wk_ref[...]).astype(BF16)

    starts = range(0, tt, sub)
    normed = [prep(r0) for r0 in starts]
    for r0, hb in zip(starts, normed):
        project(r0, hb)


def _in_proj_call(x2d, mod3, mod_row, row_len, tt, w):
    n_tok = x2d.shape[0]
    n_tiles = n_tok // tt
    const = lambda i: (0, 0)
    tok = lambda i: (i, 0)
    tok_t = lambda i: (0, i)
    weights = [w["norm1"], w["w_sc"], w["w_k"], w["w_qvogt"], w["b_g"], w["conv_sc_w"], w["conv_sc_b"]]
    out_shapes = (
        jax.ShapeDtypeStruct((n_tok, D_CONV), BF16),
        jax.ShapeDtypeStruct((D_MLSTM, n_tok), BF16),
        jax.ShapeDtypeStruct((n_tok, D_MLSTM), BF16),
        jax.ShapeDtypeStruct((D_MLSTM, n_tok), BF16),
        jax.ShapeDtypeStruct((D_MLSTM, n_tok), BF16),
        jax.ShapeDtypeStruct((n_tok, LANES), F32),
        jax.ShapeDtypeStruct((n_tok // CHUNK, 3 * N_CHAINS, CHUNK), F32),
    )
    return pl.pallas_call(
        functools.partial(_in_proj_kernel, row_len),
        grid=(n_tiles,),
        in_specs=[
            pl.BlockSpec((tt, D_MODEL), tok),
            pl.BlockSpec((None, 6, D_MODEL), lambda i: (mod_row(i), 0, 0)),
        ] + [pl.BlockSpec(a.shape, const) for a in weights],
        out_specs=[
            pl.BlockSpec((tt, D_CONV), tok),
            pl.BlockSpec((D_MLSTM, tt), tok_t),
            pl.BlockSpec((tt, D_MLSTM), tok),
            pl.BlockSpec((D_MLSTM, tt), tok_t),
            pl.BlockSpec((D_MLSTM, tt), tok_t),
            pl.BlockSpec((tt, LANES), tok),
            pl.BlockSpec((tt // CHUNK, 3 * N_CHAINS, CHUNK), lambda i: (i, 0, 0)),
        ],
        out_shape=out_shapes,
        compiler_params=_cparams(1),
        name="in_proj",
    )(x2d, mod3, *weights)


def _mlstm_kernel(cg, has_init, want_state, *refs):
    refs = list(refs)
    fwd_refs, bwd_refs = refs[0:5], refs[5:10]
    refs = refs[10:]
    if has_init:
        c0_ref, n0_ref, m0_ref = refs[:3]
        refs = refs[3:]
    htf_ref, htb_ref = refs[:2]
    refs = refs[2:]
    if want_state:
        cout_ref, nout_ref, mout_ref = refs[:3]
        refs = refs[3:]
    e_ref, m_ref = refs

    g = pl.program_id(1)

    @pl.when(g == 0)
    def _():
        if has_init:
            for d in range(2):
                for hd in range(N_HEADS):
                    j = d * N_HEADS + hd
                    e_ref[j, 0:HEAD_DIM] = c0_ref[d, hd]
                    e_ref[j, HEAD_DIM:] = jnp.broadcast_to(n0_ref[d, hd], (BF16_ROWS, HEAD_DIM))
                    m_ref[j] = jnp.broadcast_to(m0_ref[d, hd], (SUBLANES, LANES))
        else:
            e_ref[...] = jnp.zeros(e_ref.shape, F32)
            m_ref[...] = jnp.zeros(m_ref.shape, F32)

    s_idx = lax.broadcasted_iota(jnp.int32, (CHUNK, CHUNK), 0)
    t_idx = lax.broadcasted_iota(jnp.int32, (CHUNK, CHUNK), 1)
    ones_rows = jnp.ones((BF16_ROWS, CHUNK), BF16)

    def stage_scores(c, d, hd):
        qt_ref, k_ref, vt_ref, row_ref, col_ref = fwd_refs if d == 0 else bwd_refs
        j = d * N_HEADS + hd
        toks = slice(c * CHUNK, (c + 1) * CHUNK)
        feat = slice(hd * HEAD_DIM, (hd + 1) * HEAD_DIM)
        last = CHUNK - 1 if d == 0 else 0

        qtc = qt_ref[feat, toks]
        kc = k_ref[toks, feat]
        vtc = vt_ref[feat, toks]
        cm_row = row_ref[c, j:j + 1, :]
        r_row = row_ref[c, N_CHAINS + j:N_CHAINS + j + 1, :]
        b_row = row_ref[c, 2 * N_CHAINS + j:2 * N_CHAINS + j + 1, :]
        m_row = m_ref[j, 0:1, :]
        m_t = m_row[:, 0:1]
        big_m_row = jnp.maximum(m_t, cm_row)
        m_last = big_m_row[:, last:last + 1]

        e = e_ref[j]
        kq = _dot(jnp.concatenate([kc, e.astype(BF16)], axis=0), qtc)

        wg = jnp.exp(r_row - m_last)
        lhs = jnp.concatenate([(vtc.astype(F32) * wg).astype(BF16),
                               jnp.broadcast_to(wg, (BF16_ROWS, CHUNK)).astype(BF16)], axis=0)
        e_ref[j] = jnp.exp(m_row - m_last) * e + _dot(lhs, kc)
        m_ref[j] = jnp.broadcast_to(b_row[:, last:last + 1] + m_last, (SUBLANES, LANES))
        return kq, vtc, big_m_row, m_t, b_row

    def stage_output(c, d, hd, kq, vtc, big_m_row, m_t, b_row):
        col_ref = (fwd_refs if d == 0 else bwd_refs)[4]
        ht_ref = htf_ref if d == 0 else htb_ref
        j = d * N_HEADS + hd
        toks = slice(c * CHUNK, (c + 1) * CHUNK)
        feat = slice(hd * HEAD_DIM, (hd + 1) * HEAD_DIM)
        mask = (s_idx <= t_idx) if d == 0 else (s_idx >= t_idx)
        r_col = col_ref[toks, j:j + 1]

        w_t = jnp.exp(jnp.where(mask, r_col - big_m_row, -jnp.inf))
        s_t = (kq[0:CHUNK] * w_t).astype(BF16)
        num_t = _dot(jnp.concatenate([vtc, ones_rows], axis=0), s_t)
        nd = num_t + jnp.exp(m_t - big_m_row) * kq[CHUNK:]
        inv = 1.0 / jnp.maximum(jnp.abs(nd[HEAD_DIM:HEAD_DIM + 1]), jnp.exp(-(b_row + big_m_row)))
        ht_ref[feat, toks] = (nd[0:HEAD_DIM] * inv).astype(BF16)

    tasks = [(c if d == 0 else cg - 1 - c, d, hd) for c in range(cg) for d in range(2) for hd in range(N_HEADS)]
    pending = []
    for task in tasks:
        pending.append(task + stage_scores(*task))
        if len(pending) > SKEW:
            stage_output(*pending.pop(0))
    for args in pending:
        stage_output(*args)

    if want_state:
        @pl.when(g == pl.num_programs(1) - 1)
        def _():
            for d in range(2):
                for hd in range(N_HEADS):
                    j = d * N_HEADS + hd
                    cout_ref[d, hd] = e_ref[j, 0:HEAD_DIM]
                    nout_ref[d, hd] = e_ref[j, HEAD_DIM:HEAD_DIM + 1]
                    mout_ref[d, hd] = m_ref[j, 0:1, :]


def _mlstm_call(bsz, t, cg, qt, k, vt, grow, rcol, init, want_state):
    n_tok = bsz * t
    tg = cg * CHUNK
    n_groups = t // tg
    fwd = lambda b, g: b * n_groups + g
    bwd = lambda b, g: b * n_groups + (n_groups - 1 - g)

    def stream_specs(pos):
        return [
            pl.BlockSpec((D_MLSTM, tg), lambda b, g: (0, pos(b, g))),
            pl.BlockSpec((tg, D_MLSTM), lambda b, g: (pos(b, g), 0)),
            pl.BlockSpec((D_MLSTM, tg), lambda b, g: (0, pos(b, g))),
            pl.BlockSpec((cg, 3 * N_CHAINS, CHUNK), lambda b, g: (pos(b, g), 0, 0)),
            pl.BlockSpec((tg, LANES), lambda b, g: (pos(b, g), 0)),
        ]

    in_specs = stream_specs(fwd) + stream_specs(bwd)
    args = [qt, k, vt, grow, rcol] * 2
    if init is not None:
        c0, n0, m0 = init
        in_specs += [
            pl.BlockSpec((None, 2, N_HEADS, HEAD_DIM, HEAD_DIM), lambda b, g: (b, 0, 0, 0, 0)),
            pl.BlockSpec((None, 2, N_HEADS, 1, HEAD_DIM), lambda b, g: (b, 0, 0, 0, 0)),
            pl.BlockSpec((None, 2, N_HEADS, 1, LANES), lambda b, g: (b, 0, 0, 0, 0)),
        ]
        args += [c0, n0, m0]
    out_specs = [
        pl.BlockSpec((D_MLSTM, tg), lambda b, g: (0, fwd(b, g))),
        pl.BlockSpec((D_MLSTM, tg), lambda b, g: (0, bwd(b, g))),
    ]
    out_shape = [jax.ShapeDtypeStruct((D_MLSTM, n_tok), BF16)] * 2
    if want_state:
        out_specs += [
            pl.BlockSpec((None, None, 2, N_HEADS, HEAD_DIM, HEAD_DIM), lambda b, g: (b, 0, 0, 0, 0, 0)),
            pl.BlockSpec((None, None, 2, N_HEADS, 1, HEAD_DIM), lambda b, g: (b, 0, 0, 0, 0, 0)),
            pl.BlockSpec((None, None, 2, N_HEADS, 1, LANES), lambda b, g: (b, 0, 0, 0, 0, 0)),
        ]
        out_shape += [
            jax.ShapeDtypeStruct((bsz, 1, 2, N_HEADS, HEAD_DIM, HEAD_DIM), F32),
            jax.ShapeDtypeStruct((bsz, 1, 2, N_HEADS, 1, HEAD_DIM), F32),
            jax.ShapeDtypeStruct((bsz, 1, 2, N_HEADS, 1, LANES), F32),
        ]
    return pl.pallas_call(
        functools.partial(_mlstm_kernel, cg, init is not None, want_state),
        grid=(bsz, n_groups),
        in_specs=in_specs,
        out_specs=out_specs,
        out_shape=out_shape,
        scratch_shapes=[
            pltpu.VMEM((N_CHAINS, E_ROWS, HEAD_DIM), F32),
            pltpu.VMEM((N_CHAINS, SUBLANES, LANES), F32),
        ],
        compiler_params=_cparams(2),
        name="mlstm",
    )(*args)


def _out_proj_kernel(x_ref, ysc_ref, htf_ref, htb_ref, sot_ref, mh_ref, mod_ref, n2_ref, wa_ref, wb_ref,
                     x1_ref, h2_ref):
    mod = mod_ref[...]
    gate1, shift2, scale2 = mod[2:3], mod[3:4], mod[4:5]
    hs = htf_ref[...].astype(F32) + htb_ref[...].astype(F32)
    heads = []
    for hd in range(N_HEADS):
        blk = hs[hd * HEAD_DIM:(hd + 1) * HEAD_DIM]
        heads.append(blk * lax.rsqrt(jnp.mean(blk * blk, axis=0, keepdims=True) + EPS))
    hm_t = ((jnp.concatenate(heads, axis=0) * mh_ref[...]) * sot_ref[...].astype(F32)).astype(BF16)
    mix = _dot(ysc_ref[...], wa_ref[...]) + _dot_tn(hm_t, wb_ref[...])
    x1 = x_ref[...] + gate1 * mix
    x1_ref[...] = x1
    h2_ref[...] = ((_rms_scale(x1) * n2_ref[...]) * (1.0 + scale2) + shift2).astype(BF16)


def _out_proj_call(x2d, ysc, htf, htb, sot, mod3, mod_row, tt, w):
    n_tok = x2d.shape[0]
    const = lambda i: (0, 0)
    tok = lambda i: (i, 0)
    tok_t = lambda i: (0, i)
    return pl.pallas_call(
        _out_proj_kernel,
        grid=(n_tok // tt,),
        in_specs=[
            pl.BlockSpec((tt, D_MODEL), tok),
            pl.BlockSpec((tt, D_CONV), tok),
            pl.BlockSpec((D_MLSTM, tt), tok_t),
            pl.BlockSpec((D_MLSTM, tt), tok_t),
            pl.BlockSpec((D_MLSTM, tt), tok_t),
            pl.BlockSpec((D_MLSTM, 1), const),
            pl.BlockSpec((None, 6, D_MODEL), lambda i: (mod_row(i), 0, 0)),
            pl.BlockSpec((1, D_MODEL), const),
            pl.BlockSpec(w["w_out_a"].shape, const),
            pl.BlockSpec(w["w_out_b"].shape, const),
        ],
        out_specs=[pl.BlockSpec((tt, D_MODEL), tok), pl.BlockSpec((tt, D_MODEL), tok)],
        out_shape=[jax.ShapeDtypeStruct((n_tok, D_MODEL), F32),
                   jax.ShapeDtypeStruct((n_tok, D_MODEL), BF16)],
        compiler_params=_cparams(1),
        name="out_proj",
    )(x2d, ysc, htf, htb, sot, w["mh_norm_col"], mod3, w["norm2"], w["w_out_a"], w["w_out_b"])


def _ffn_kernel(shift, seq_len, h2_ref, x1_ref, mod_ref, fn_ref, wu_ref, cw_ref, cb_ref, wd_ref, y_ref,
                act_ref, a0_ref, a1_ref, a2_ref, g0_ref, g1_ref, g2_ref, h2s_ref):
    slots = ((a0_ref, g0_ref), (a1_ref, g1_ref), (a2_ref, g2_ref))
    tt = act_ref.shape[0]
    pad = _ffn_pad(shift)
    nr = CONV_ROWS
    row = lax.broadcasted_iota(jnp.int32, (nr, FF_BLK), 0)

    def up_proj(blk, slot):
        a_ref, g_ref = slots[slot]
        off_a = pl.multiple_of(blk * FF_BLK, FF_BLK)
        off_g = pl.multiple_of(blk * FF_BLK + D_FF, FF_BLK)
        hin = h2s_ref[...]
        a_ref[pad:pad + tt] = _dot(hin, wu_ref[:, pl.ds(off_a, FF_BLK)])
        g_ref[pad:pad + tt] = _dot(hin, wu_ref[:, pl.ds(off_g, FF_BLK)])

    def conv_act(blk, slot):
        a_ref, g_ref = slots[slot]
        off_a = pl.multiple_of(blk * FF_BLK, FF_BLK)
        off_g = pl.multiple_of(blk * FF_BLK + D_FF, FF_BLK)

        def conv(z_ref, off, r0):
            cw = cw_ref[:, pl.ds(off, FF_BLK)]
            cb = cb_ref[:, pl.ds(off, FF_BLK)]
            prev = z_ref[pad - shift + r0:pad - shift + r0 + nr]
            nxt = z_ref[pad + shift + r0:pad + shift + r0 + nr]
            if seq_len is not None:
                if r0 % seq_len == 0:
                    prev = jnp.where(row == 0, 0.0, prev)
                if (r0 + nr) % seq_len == 0:
                    nxt = jnp.where(row == nr - 1, 0.0, nxt)
            return cw[0:1] * prev + cw[1:2] * z_ref[pad + r0:pad + r0 + nr] + cw[2:3] * nxt + cb

        for r0 in range(0, tt, nr):
            ac = conv(a_ref, off_a, r0)
            gc = conv(g_ref, off_g, r0)
            act_ref[r0:r0 + nr, pl.ds(off_a, FF_BLK)] = (gc * jax.nn.sigmoid(gc) * ac).astype(BF16)

    h2s_ref[...] = h2_ref[...].reshape(tt, D_MODEL)
    for a_ref, g_ref in slots:
        for z_ref in (a_ref, g_ref):
            z_ref[0:pad] = jnp.zeros((pad, FF_BLK), F32)
            z_ref[pad + tt:] = jnp.zeros((pad, FF_BLK), F32)

    n_slots = len(slots)
    for blk in range(SKEW_FF):
        up_proj(blk, blk % n_slots)

    def block_group(p, carry):
        for s in range(n_slots):
            blk = n_slots * p + SKEW_FF + s
            up_proj(blk, (SKEW_FF + s) % n_slots)
            conv_act(blk - SKEW_FF, s % n_slots)
        return carry

    n_groups = (N_FF_BLK - SKEW_FF) // n_slots
    lax.fori_loop(0, n_groups, block_group, 0)
    done = n_slots * n_groups + SKEW_FF
    for blk in range(done, N_FF_BLK):
        up_proj(blk, blk % n_slots)
        conv_act(blk - SKEW_FF, (blk - SKEW_FF) % n_slots)

    k0 = (N_FF_BLK - SKEW_FF) * FF_BLK
    part = _dot(act_ref[:, 0:k0], wd_ref[0:k0, :])
    for blk in range(N_FF_BLK - SKEW_FF, N_FF_BLK):
        conv_act(blk, blk % n_slots)
    ffn = part + _dot(act_ref[:, k0:], wd_ref[k0:, :])
    x2 = x1_ref[...].reshape(tt, D_MODEL) + mod_ref[5:6] * ffn
    y_ref[...] = (_rms_scale(x2) * fn_ref[...]).reshape(y_ref.shape)


def _ffn_pad(shift):
    return -(-shift // SUBLANES) * SUBLANES


def _ffn_call(h2, x1, mod3, mod_row_of_batch, grid_w, w):
    bsz, t, _ = x1.shape
    const = lambda i: (0, 0)
    resident = dict(pipeline_mode=pl.Buffered(1))
    if grid_w is None:
        tt, shift, seq_len = FFN_TILE, 1, t
        per_tile = tt // t
        blk = (per_tile, t, D_MODEL)
        n_tiles = bsz // per_tile
        tile_idx = lambda i: (i, 0, 0)
        batch_of = lambda i: i * per_tile
        as_blocks = lambda a: a
    else:
        rows = t // grid_w
        tt, shift, seq_len = rows * FFN_COLS, FFN_COLS, None
        strips = grid_w // FFN_COLS
        blk = (None, rows, FFN_COLS, D_MODEL)
        n_tiles = bsz * strips
        tile_idx = lambda i: (i // strips, 0, i % strips, 0)
        batch_of = lambda i: i // strips
        as_blocks = lambda a: a.reshape(bsz, rows, grid_w, D_MODEL)
    up_rows = tt + 2 * _ffn_pad(shift)
    scratch = ([pltpu.VMEM((tt, D_FF), BF16)] + [pltpu.VMEM((up_rows, FF_BLK), F32)] * (2 * (SKEW_FF + 1))
               + [pltpu.VMEM((tt, D_MODEL), BF16)])
    in_specs = [
        pl.BlockSpec(blk, tile_idx),
        pl.BlockSpec(blk, tile_idx),
        pl.BlockSpec((None, 6, D_MODEL), lambda i: (mod_row_of_batch(batch_of(i)), 0, 0)),
        pl.BlockSpec((1, D_MODEL), const),
        pl.BlockSpec((D_MODEL, 2 * D_FF), const, **resident),
        pl.BlockSpec((3, 2 * D_FF), const),
        pl.BlockSpec((1, 2 * D_FF), const),
        pl.BlockSpec((D_FF, D_MODEL), const, **resident),
    ]
    y = pl.pallas_call(
        functools.partial(_ffn_kernel, shift, seq_len),
        grid=(n_tiles,),
        in_specs=in_specs,
        out_specs=pl.BlockSpec(blk, tile_idx),
        out_shape=jax.ShapeDtypeStruct(as_blocks(x1).shape, F32),
        scratch_shapes=scratch,
        compiler_params=_cparams(1),
        name="ffn",
    )(as_blocks(h2), as_blocks(x1), mod3, w["final_norm"], w["w_up"], w["conv_ffn_w"], w["conv_ffn_b"],
      w["w_down"])
    return y.reshape(bsz, t, D_MODEL)


def _trunk(x, mod3, mod_row_of_batch, row_len, ffn_grid_w, init, want_state, tt_out, tt_in, cg, w):
    bsz, t, _ = x.shape
    x2d = x.reshape(bsz * t, D_MODEL)

    def mod_row(tile_tokens):
        return lambda i: mod_row_of_batch((i * tile_tokens) // t)

    ysc, qt, k, vt, sot, rcol, grow = _in_proj_call(x2d, mod3, mod_row(tt_in), row_len, tt_in, w)
    outs = _mlstm_call(bsz, t, cg, qt, k, vt, grow, rcol, init, want_state)
    x1, h2 = _out_proj_call(x2d, ysc, outs[0], outs[1], sot, mod3, mod_row(tt_out), tt_out, w)
    y = _ffn_call(h2.reshape(bsz, t, D_MODEL), x1.reshape(bsz, t, D_MODEL), mod3, mod_row_of_batch, ffn_grid_w, w)
    return y, outs[2:]


def kernel(x_prompt, x_sample, state_C, state_n, state_m, c, c_ctx, w_mod, b_mod, norm1, w_in, b_gate,
           conv_sc_w, conv_sc_b, mh_norm, w_out, norm2, w_up, conv_ffn_w, conv_ffn_b, w_down, final_norm):
    n_lat = c.shape[0]
    n_ctx = x_prompt.shape[0]
    c8 = jnp.concatenate([c, c_ctx[None], jnp.zeros((8 - n_lat - 1, D_MODEL), F32)], axis=0)
    l = 0
    mod3 = _mod_call(c8, w_mod[l], b_mod[l][None]).reshape(8, 6, D_MODEL)

    wi = w_in[l]
    q0 = 3 * D_CONV
    g0 = q0 + 4 * D_MLSTM
    cols = lambda a: wi[:, g0 + a * N_HEADS:g0 + (a + 1) * N_HEADS]
    bg = b_gate[l].astype(F32)
    b_g = jnp.concatenate([bg[0], bg[2], bg[1], bg[3]])[:, None]
    w = dict(
        norm1=norm1[l][None], norm2=norm2[l][None], final_norm=final_norm[None],
        w_sc=wi[:, 0:q0].astype(BF16),
        w_k=wi[:, q0 + D_MLSTM:q0 + 2 * D_MLSTM].astype(BF16),
        w_qvogt=jnp.concatenate([wi[:, q0:q0 + D_MLSTM], wi[:, q0 + 2 * D_MLSTM:g0],
                                 cols(0), cols(2), cols(1), cols(3)], axis=1).T.astype(BF16),
        b_g=b_g,
        conv_sc_w=conv_sc_w[l], conv_sc_b=conv_sc_b[l][None],
        mh_norm_col=mh_norm[l][:, None],
        w_out_a=w_out[l][0:D_CONV].astype(BF16), w_out_b=w_out[l][D_CONV:].astype(BF16),
        w_up=w_up[l].astype(BF16), conv_ffn_w=conv_ffn_w[l], conv_ffn_b=conv_ffn_b[l][None],
        w_down=w_down[l].astype(BF16),
    )

    seq = x_prompt.shape[1]
    y_prompt, (new_c, new_n, new_m) = _trunk(
        x_prompt, mod3, lambda b: n_lat, seq, None, None, True, 512, 512, 1, w)

    init = (state_C[:, l], state_n[:, l][:, :, :, None, :],
            jnp.broadcast_to(state_m[:, l][:, :, :, None, None], state_m[:, l].shape + (1, LANES)))
    y_sample, _ = _trunk(x_sample, mod3, lambda b: b, GRID_W, GRID_W, init, False, 512, 512, 2, w)

    new_n = new_n.reshape(n_ctx, 1, 2, N_HEADS, HEAD_DIM)
    new_m = new_m[..., 0, 0]
    return y_prompt, y_sample, new_c, new_n, new_m
```

```python
import functools

import jax
import jax.numpy as jnp
from jax import lax
from jax.experimental import pallas as pl
from jax.experimental.pallas import tpu as pltpu

F32 = jnp.float32
BF16 = jnp.bfloat16

D_MODEL = 1024
GRID_W = 64
D_CONV = 512
D_MLSTM = 512
N_HEADS = 4
HEAD_DIM = 128
D_FF = 2816
CHUNK = 256
EPS = 1e-6

LANES = 128
SUBLANES = 8
BF16_ROWS = 16
FF_BLK = 256
N_FF_BLK = D_FF // FF_BLK
CONV_ROWS = 128
FFN_TILE = 1024
FFN_COLS = 16
VMEM_LIMIT = 60 * 1024 * 1024
N_CHAINS = 2 * N_HEADS
E_ROWS = HEAD_DIM + BF16_ROWS
SKEW = 3
IN_SUB = 512
SKEW_FF = 2


def _cparams(n_axes):
    return pltpu.CompilerParams(
        dimension_semantics=("arbitrary",) * n_axes, vmem_limit_bytes=VMEM_LIMIT)


def _rms_scale(x):
    return x * lax.rsqrt(jnp.mean(x * x, axis=-1, keepdims=True) + EPS)


def _dot(a, b):
    return jnp.dot(a, b, preferred_element_type=F32)


def _dot_nt(a, b):
    return lax.dot_general(a, b, (((1,), (1,)), ((), ())), preferred_element_type=F32)


def _dot_tn(a, b):
    return lax.dot_general(a, b, (((0,), (0,)), ((), ())), preferred_element_type=F32)


def _mod_kernel(c_ref, w_ref, b_ref, o_ref):
    c = c_ref[...]
    s = c * jax.nn.sigmoid(c)
    o_ref[...] = _dot(s.astype(BF16), w_ref[...].astype(BF16)) + b_ref[...]


def _mod_call(c8, w_mod, b_mod):
    n_out = w_mod.shape[1]
    blk = 1024
    return pl.pallas_call(
        _mod_kernel,
        grid=(n_out // blk,),
        in_specs=[
            pl.BlockSpec((8, D_MODEL), lambda i: (0, 0)),
            pl.BlockSpec((D_MODEL, blk), lambda i: (0, i)),
            pl.BlockSpec((1, blk), lambda i: (0, i)),
        ],
        out_specs=pl.BlockSpec((8, blk), lambda i: (0, i)),
        out_shape=jax.ShapeDtypeStruct((8, n_out), F32),
        compiler_params=_cparams(1),
        name="mod",
    )(c8, w_mod, b_mod)


def _chunk_scan(x, pos, op, fill, reverse):
    n = x.shape[1]
    k = 1
    while k < CHUNK:
        if reverse:
            shifted = pltpu.roll(x, n - k, axis=1)
            ok = pos < CHUNK - k
        else:
            shifted = pltpu.roll(x, k, axis=1)
            ok = pos >= k
        x = op(x, jnp.where(ok, shifted, fill))
        k *= 2
    return x


def _in_proj_kernel(row_len, x_ref, mod_ref, n1_ref, wsc_ref, wk_ref, wqvogt_ref, bg_ref, cw_ref, cb_ref,
                    ysc_ref, qt_ref, k_ref, vt_ref, sot_ref, rcol_ref, grow_ref):
    tt = x_ref.shape[0]
    sub = min(tt, IN_SUB)
    mod = mod_ref[...]
    shift1, scale1 = mod[0:1], mod[1:2]

    def prep(r0):
        x = x_ref[r0:r0 + sub]
        h = (_rms_scale(x) * n1_ref[...]) * (1.0 + scale1) + shift1
        return h.astype(BF16)

    def project_mlstm(r0, hb):
        rows = slice(r0, r0 + sub)
        qvog_t = _dot_nt(wqvogt_ref[...], hb)
        qt_ref[:, rows] = (qvog_t[0:D_MLSTM] * (HEAD_DIM ** -0.5)).astype(BF16)
        vt_ref[:, rows] = qvog_t[D_MLSTM:2 * D_MLSTM].astype(BF16)
        sot_ref[:, rows] = jax.nn.sigmoid(qvog_t[2 * D_MLSTM:3 * D_MLSTM]).astype(BF16)

        g_t = qvog_t[3 * D_MLSTM:3 * D_MLSTM + 2 * N_CHAINS] + bg_ref[...]
        gi = g_t[0:N_CHAINS]
        gf = g_t[N_CHAINS:2 * N_CHAINS]
        logf = jnp.minimum(gf, 0.0) - jnp.log1p(jnp.exp(-jnp.abs(gf)))
        fwd = lax.broadcasted_iota(jnp.int32, gi.shape, 0) < N_HEADS
        cpos = lax.broadcasted_iota(jnp.int32, gi.shape, 1) % CHUNK
        b = jnp.where(fwd, _chunk_scan(logf, cpos, jnp.add, 0.0, False),
                      _chunk_scan(logf, cpos, jnp.add, 0.0, True))
        r = gi - b
        cm = jnp.where(fwd, _chunk_scan(r, cpos, jnp.maximum, -jnp.inf, False),
                       _chunk_scan(r, cpos, jnp.maximum, -jnp.inf, True))
        for c in range(sub // CHUNK):
            toks = slice(c * CHUNK, (c + 1) * CHUNK)
            grow_ref[r0 // CHUNK + c] = jnp.concatenate([cm[:, toks], r[:, toks], b[:, toks]], axis=0)
        rcol_ref[rows] = jnp.concatenate([r, jnp.zeros((LANES - N_CHAINS, sub), F32)], axis=0).T

    def project_conv(r0, hb):
        rows = slice(r0, r0 + sub)
        zb = _dot(hb, wsc_ref[:, 0:D_CONV])
        zc = _dot(hb, wsc_ref[:, D_CONV:2 * D_CONV])
        zx = _dot(hb, wsc_ref[:, 2 * D_CONV:3 * D_CONV])
        u = zc * zx
        pos = lax.broadcasted_iota(jnp.int32, u.shape, 0) % row_len
        prev = jnp.where(pos == 0, 0.0, pltpu.roll(u, 1, axis=0))
        nxt = jnp.where(pos == row_len - 1, 0.0, pltpu.roll(u, sub - 1, axis=0))
        cw = cw_ref[...]
        conv = cw[0:1] * prev + cw[1:2] * u + cw[2:3] * nxt + cb_ref[...]
        ysc_ref[rows] = (zb * conv).astype(BF16)

        k_ref[rows] = _dot(hb, wk_ref[...]).astype(BF16)

    starts = list(range(0, tt, sub))
    hb = prep(starts[0])
    for idx, r0 in enumerate(starts):
        project_mlstm(r0, hb)
        hb_next = prep(starts[idx + 1]) if idx + 1 < len(starts) else None
        project_conv(r0, hb)
        hb = hb_next


def _in_proj_call(x2d, mod3, mod_row, row_len, tt, w):
    n_tok = x2d.shape[0]
    n_tiles = n_tok // tt
    const = lambda i: (0, 0)
    tok = lambda i: (i, 0)
    tok_t = lambda i: (0, i)
    weights = [w["norm1"], w["w_sc"], w["w_k"], w["w_qvogt"], w["b_g"], w["conv_sc_w"], w["conv_sc_b"]]
    out_shapes = (
        jax.ShapeDtypeStruct((n_tok, D_CONV), BF16),
        jax.ShapeDtypeStruct((D_MLSTM, n_tok), BF16),
        jax.ShapeDtypeStruct((n_tok, D_MLSTM), BF16),
        jax.ShapeDtypeStruct((D_MLSTM, n_tok), BF16),
        jax.ShapeDtypeStruct((D_MLSTM, n_tok), BF16),
        jax.ShapeDtypeStruct((n_tok, LANES), F32),
        jax.ShapeDtypeStruct((n_tok // CHUNK, 3 * N_CHAINS, CHUNK), F32),
    )
    return pl.pallas_call(
        functools.partial(_in_proj_kernel, row_len),
        grid=(n_tiles,),
        in_specs=[
            pl.BlockSpec((tt, D_MODEL), tok),
            pl.BlockSpec((None, 6, D_MODEL), lambda i: (mod_row(i), 0, 0)),
        ] + [pl.BlockSpec(a.shape, const) for a in weights],
        out_specs=[
            pl.BlockSpec((tt, D_CONV), tok),
            pl.BlockSpec((D_MLSTM, tt), tok_t),
            pl.BlockSpec((tt, D_MLSTM), tok),
            pl.BlockSpec((D_MLSTM, tt), tok_t),
            pl.BlockSpec((D_MLSTM, tt), tok_t),
            pl.BlockSpec((tt, LANES), tok),
            pl.BlockSpec((tt // CHUNK, 3 * N_CHAINS, CHUNK), lambda i: (i, 0, 0)),
        ],
        out_shape=out_shapes,
        compiler_params=_cparams(1),
        name="in_proj",
    )(x2d, mod3, *weights)


def _mlstm_kernel(cg, has_init, want_state, *refs):
    refs = list(refs)
    fwd_refs, bwd_refs = refs[0:5], refs[5:10]
    refs = refs[10:]
    if has_init:
        c0_ref, n0_ref, m0_ref = refs[:3]
        refs = refs[3:]
    htf_ref, htb_ref = refs[:2]
    refs = refs[2:]
    if want_state:
        cout_ref, nout_ref, mout_ref = refs[:3]
        refs = refs[3:]
    e_ref, m_ref = refs

    g = pl.program_id(1)

    @pl.when(g == 0)
    def _():
        if has_init:
            for d in range(2):
                for hd in range(N_HEADS):
                    j = d * N_HEADS + hd
                    e_ref[j, 0:HEAD_DIM] = c0_ref[d, hd]
                    e_ref[j, HEAD_DIM:] = jnp.broadcast_to(n0_ref[d, hd], (BF16_ROWS, HEAD_DIM))
                    m_ref[j] = jnp.broadcast_to(m0_ref[d, hd], (SUBLANES, LANES))
        else:
            e_ref[...] = jnp.zeros(e_ref.shape, F32)
            m_ref[...] = jnp.zeros(m_ref.shape, F32)

    s_idx = lax.broadcasted_iota(jnp.int32, (CHUNK, CHUNK), 0)
    t_idx = lax.broadcasted_iota(jnp.int32, (CHUNK, CHUNK), 1)
    ones_rows = jnp.ones((BF16_ROWS, CHUNK), BF16)

    def stage_scores(c, d, hd):
        qt_ref, k_ref, vt_ref, row_ref, col_ref = fwd_refs if d == 0 else bwd_refs
        j = d * N_HEADS + hd
        toks = slice(c * CHUNK, (c + 1) * CHUNK)
        feat = slice(hd * HEAD_DIM, (hd + 1) * HEAD_DIM)
        last = CHUNK - 1 if d == 0 else 0

        qtc = qt_ref[feat, toks]
        kc = k_ref[toks, feat]
        vtc = vt_ref[feat, toks]
        cm_row = row_ref[c, j:j + 1, :]
        r_row = row_ref[c, N_CHAINS + j:N_CHAINS + j + 1, :]
        b_row = row_ref[c, 2 * N_CHAINS + j:2 * N_CHAINS + j + 1, :]
        m_row = m_ref[j, 0:1, :]
        m_t = m_row[:, 0:1]
        big_m_row = jnp.maximum(m_t, cm_row)
        m_last = big_m_row[:, last:last + 1]

        e = e_ref[j]
        kq = _dot(jnp.concatenate([kc, e.astype(BF16)], axis=0), qtc)

        wg = jnp.exp(r_row - m_last)
        lhs = jnp.concatenate([(vtc.astype(F32) * wg).astype(BF16),
                               jnp.broadcast_to(wg, (BF16_ROWS, CHUNK)).astype(BF16)], axis=0)
        e_ref[j] = jnp.exp(m_row - m_last) * e + _dot(lhs, kc)
        m_ref[j] = jnp.broadcast_to(b_row[:, last:last + 1] + m_last, (SUBLANES, LANES))
        return kq, vtc, big_m_row, m_t, b_row

    def stage_output(c, d, hd, kq, vtc, big_m_row, m_t, b_row):
        col_ref = (fwd_refs if d == 0 else bwd_refs)[4]
        ht_ref = htf_ref if d == 0 else htb_ref
        j = d * N_HEADS + hd
        toks = slice(c * CHUNK, (c + 1) * CHUNK)
        feat = slice(hd * HEAD_DIM, (hd + 1) * HEAD_DIM)
        mask = (s_idx <= t_idx) if d == 0 else (s_idx >= t_idx)
        r_col = col_ref[toks, j:j + 1]

        w_t = jnp.exp(jnp.where(mask, r_col - big_m_row, -jnp.inf))
        s_t = (kq[0:CHUNK] * w_t).astype(BF16)
        num_t = _dot(jnp.concatenate([vtc, ones_rows], axis=0), s_t)
        nd = num_t + jnp.exp(m_t - big_m_row) * kq[CHUNK:]
        inv = 1.0 / jnp.maximum(jnp.abs(nd[HEAD_DIM:HEAD_DIM + 1]), jnp.exp(-(b_row + big_m_row)))
        ht_ref[feat, toks] = (nd[0:HEAD_DIM] * inv).astype(BF16)

    tasks = [(c if d == 0 else cg - 1 - c, d, hd) for c in range(cg) for d in range(2) for hd in range(N_HEADS)]
    pending = []
    for task in tasks:
        pending.append(task + stage_scores(*task))
        if len(pending) > SKEW:
            stage_output(*pending.pop(0))
    for args in pending:
        stage_output(*args)

    if want_state:
        @pl.when(g == pl.num_programs(1) - 1)
        def _():
            for d in range(2):
                for hd in range(N_HEADS):
                    j = d * N_HEADS + hd
                    cout_ref[d, hd] = e_ref[j, 0:HEAD_DIM]
                    nout_ref[d, hd] = e_ref[j, HEAD_DIM:HEAD_DIM + 1]
                    mout_ref[d, hd] = m_ref[j, 0:1, :]


def _mlstm_call(bsz, t, cg, qt, k, vt, grow, rcol, init, want_state):
    n_tok = bsz * t
    tg = cg * CHUNK
    n_groups = t // tg
    fwd = lambda b, g: b * n_groups + g
    bwd = lambda b, g: b * n_groups + (n_groups - 1 - g)

    def stream_specs(pos):
        return [
            pl.BlockSpec((D_MLSTM, tg), lambda b, g: (0, pos(b, g))),
            pl.BlockSpec((tg, D_MLSTM), lambda b, g: (pos(b, g), 0)),
            pl.BlockSpec((D_MLSTM, tg), lambda b, g: (0, pos(b, g))),
            pl.BlockSpec((cg, 3 * N_CHAINS, CHUNK), lambda b, g: (pos(b, g), 0, 0)),
            pl.BlockSpec((tg, LANES), lambda b, g: (pos(b, g), 0)),
        ]

    in_specs = stream_specs(fwd) + stream_specs(bwd)
    args = [qt, k, vt, grow, rcol] * 2
    if init is not None:
        c0, n0, m0 = init
        in_specs += [
            pl.BlockSpec((None, 2, N_HEADS, HEAD_DIM, HEAD_DIM), lambda b, g: (b, 0, 0, 0, 0)),
            pl.BlockSpec((None, 2, N_HEADS, 1, HEAD_DIM), lambda b, g: (b, 0, 0, 0, 0)),
            pl.BlockSpec((None, 2, N_HEADS, 1, LANES), lambda b, g: (b, 0, 0, 0, 0)),
        ]
        args += [c0, n0, m0]
    out_specs = [
        pl.BlockSpec((D_MLSTM, tg), lambda b, g: (0, fwd(b, g))),
        pl.BlockSpec((D_MLSTM, tg), lambda b, g: (0, bwd(b, g))),
    ]
    out_shape = [jax.ShapeDtypeStruct((D_MLSTM, n_tok), BF16)] * 2
    if want_state:
        out_specs += [
            pl.BlockSpec((None, None, 2, N_HEADS, HEAD_DIM, HEAD_DIM), lambda b, g: (b, 0, 0, 0, 0, 0)),
            pl.BlockSpec((None, None, 2, N_HEADS, 1, HEAD_DIM), lambda b, g: (b, 0, 0, 0, 0, 0)),
            pl.BlockSpec((None, None, 2, N_HEADS, 1, LANES), lambda b, g: (b, 0, 0, 0, 0, 0)),
        ]
        out_shape += [
            jax.ShapeDtypeStruct((bsz, 1, 2, N_HEADS, HEAD_DIM, HEAD_DIM), F32),
            jax.ShapeDtypeStruct((bsz, 1, 2, N_HEADS, 1, HEAD_DIM), F32),
            jax.ShapeDtypeStruct((bsz, 1, 2, N_HEADS, 1, LANES), F32),
        ]
    return pl.pallas_call(
        functools.partial(_mlstm_kernel, cg, init is not None, want_state),
        grid=(bsz, n_groups),
        in_specs=in_specs,
        out_specs=out_specs,
        out_shape=out_shape,
        scratch_shapes=[
            pltpu.VMEM((N_CHAINS, E_ROWS, HEAD_DIM), F32),
            pltpu.VMEM((N_CHAINS, SUBLANES, LANES), F32),
        ],
        compiler_params=_cparams(2),
        name="mlstm",
    )(*args)


def _out_proj_kernel(x_ref, ysc_ref, htf_ref, htb_ref, sot_ref, mh_ref, mod_ref, n2_ref, wa_ref, wb_ref,
                     x1_ref, h2_ref):
    mod = mod_ref[...]
    gate1, shift2, scale2 = mod[2:3], mod[3:4], mod[4:5]
    hs = htf_ref[...].astype(F32) + htb_ref[...].astype(F32)
    heads = []
    for hd in range(N_HEADS):
        blk = hs[hd * HEAD_DIM:(hd + 1) * HEAD_DIM]
        heads.append(blk * lax.rsqrt(jnp.mean(blk * blk, axis=0, keepdims=True) + EPS))
    hm_t = ((jnp.concatenate(heads, axis=0) * mh_ref[...]) * sot_ref[...].astype(F32)).astype(BF16)
    mix = _dot(ysc_ref[...], wa_ref[...]) + _dot_tn(hm_t, wb_ref[...])
    x1 = x_ref[...] + gate1 * mix
    x1_ref[...] = x1
    h2_ref[...] = ((_rms_scale(x1) * n2_ref[...]) * (1.0 + scale2) + shift2).astype(BF16)


def _out_proj_call(x2d, ysc, htf, htb, sot, mod3, mod_row, tt, w):
    n_tok = x2d.shape[0]
    const = lambda i: (0, 0)
    tok = lambda i: (i, 0)
    tok_t = lambda i: (0, i)
    return pl.pallas_call(
        _out_proj_kernel,
        grid=(n_tok // tt,),
        in_specs=[
            pl.BlockSpec((tt, D_MODEL), tok),
            pl.BlockSpec((tt, D_CONV), tok),
            pl.BlockSpec((D_MLSTM, tt), tok_t),
            pl.BlockSpec((D_MLSTM, tt), tok_t),
            pl.BlockSpec((D_MLSTM, tt), tok_t),
            pl.BlockSpec((D_MLSTM, 1), const),
            pl.BlockSpec((None, 6, D_MODEL), lambda i: (mod_row(i), 0, 0)),
            pl.BlockSpec((1, D_MODEL), const),
            pl.BlockSpec(w["w_out_a"].shape, const),
            pl.BlockSpec(w["w_out_b"].shape, const),
        ],
        out_specs=[pl.BlockSpec((tt, D_MODEL), tok), pl.BlockSpec((tt, D_MODEL), tok)],
        out_shape=[jax.ShapeDtypeStruct((n_tok, D_MODEL), F32),
                   jax.ShapeDtypeStruct((n_tok, D_MODEL), BF16)],
        compiler_params=_cparams(1),
        name="out_proj",
    )(x2d, ysc, htf, htb, sot, w["mh_norm_col"], mod3, w["norm2"], w["w_out_a"], w["w_out_b"])


def _ffn_kernel(shift, seq_len, h2_ref, x1_ref, mod_ref, fn_ref, wu_ref, cw_ref, cb_ref, wd_ref, y_ref,
                act_ref, h2s_ref, *slot_refs):
    slots = tuple(zip(slot_refs[0::2], slot_refs[1::2]))
    tt = act_ref.shape[0]
    pad = _ffn_pad(shift)
    nr = CONV_ROWS
    row = lax.broadcasted_iota(jnp.int32, (nr, FF_BLK), 0)

    def up_proj(blk, slot):
        a_ref, g_ref = slots[slot]
        off_a = pl.multiple_of(blk * FF_BLK, FF_BLK)
        off_g = pl.multiple_of(blk * FF_BLK + D_FF, FF_BLK)
        hin = h2s_ref[...]
        a_ref[pad:pad + tt] = _dot(hin, wu_ref[:, pl.ds(off_a, FF_BLK)])
        g_ref[pad:pad + tt] = _dot(hin, wu_ref[:, pl.ds(off_g, FF_BLK)])

    def conv_act(blk, slot):
        a_ref, g_ref = slots[slot]
        off_a = pl.multiple_of(blk * FF_BLK, FF_BLK)
        off_g = pl.multiple_of(blk * FF_BLK + D_FF, FF_BLK)

        def conv(z_ref, off, r0):
            cw = cw_ref[:, pl.ds(off, FF_BLK)]
            cb = cb_ref[:, pl.ds(off, FF_BLK)]
            prev = z_ref[pad - shift + r0:pad - shift + r0 + nr]
            nxt = z_ref[pad + shift + r0:pad + shift + r0 + nr]
            if seq_len is not None:
                if r0 % seq_len == 0:
                    prev = jnp.where(row == 0, 0.0, prev)
                if (r0 + nr) % seq_len == 0:
                    nxt = jnp.where(row == nr - 1, 0.0, nxt)
            return cw[0:1] * prev + cw[1:2] * z_ref[pad + r0:pad + r0 + nr] + cw[2:3] * nxt + cb

        for r0 in range(0, tt, nr):
            ac = conv(a_ref, off_a, r0)
            gc = conv(g_ref, off_g, r0)
            act_ref[r0:r0 + nr, pl.ds(off_a, FF_BLK)] = (gc * jax.nn.sigmoid(gc) * ac).astype(BF16)

    h2s_ref[...] = h2_ref[...].reshape(tt, D_MODEL)
    for a_ref, g_ref in slots:
        for z_ref in (a_ref, g_ref):
            z_ref[0:pad] = jnp.zeros((pad, FF_BLK), F32)
            z_ref[pad + tt:] = jnp.zeros((pad, FF_BLK), F32)

    n_slots = len(slots)
    for blk in range(SKEW_FF):
        up_proj(blk, blk % n_slots)

    def block_group(p, carry):
        for s in range(n_slots):
            blk = n_slots * p + SKEW_FF + s
            up_proj(blk, (SKEW_FF + s) % n_slots)
            conv_act(blk - SKEW_FF, s % n_slots)
        return carry

    n_groups = (N_FF_BLK - SKEW_FF) // n_slots
    lax.fori_loop(0, n_groups, block_group, 0)
    done = n_slots * n_groups + SKEW_FF
    for blk in range(done, N_FF_BLK):
        up_proj(blk, blk % n_slots)
        conv_act(blk - SKEW_FF, (blk - SKEW_FF) % n_slots)

    k0 = (N_FF_BLK - SKEW_FF) * FF_BLK
    part = _dot(act_ref[:, 0:k0], wd_ref[0:k0, :])
    for blk in range(N_FF_BLK - SKEW_FF, N_FF_BLK):
        conv_act(blk, blk % n_slots)
    ffn = part + _dot(act_ref[:, k0:], wd_ref[k0:, :])
    x2 = x1_ref[...].reshape(tt, D_MODEL) + mod_ref[5:6] * ffn
    y_ref[...] = (_rms_scale(x2) * fn_ref[...]).reshape(y_ref.shape)


def _ffn_pad(shift):
    return -(-shift // SUBLANES) * SUBLANES


def _ffn_call(h2, x1, mod3, mod_row_of_batch, grid_w, w):
    bsz, t, _ = x1.shape
    const = lambda i: (0, 0)
    resident = dict(pipeline_mode=pl.Buffered(1))
    if grid_w is None:
        tt, shift, seq_len = FFN_TILE, 1, t
        per_tile = tt // t
        blk = (per_tile, t, D_MODEL)
        n_tiles = bsz // per_tile
        tile_idx = lambda i: (i, 0, 0)
        batch_of = lambda i: i * per_tile
        as_blocks = lambda a: a
    else:
        rows = t // grid_w
        tt, shift, seq_len = rows * FFN_COLS, FFN_COLS, None
        strips = grid_w // FFN_COLS
        blk = (None, rows, FFN_COLS, D_MODEL)
        n_tiles = bsz * strips
        tile_idx = lambda i: (i // strips, 0, i % strips, 0)
        batch_of = lambda i: i // strips
        as_blocks = lambda a: a.reshape(bsz, rows, grid_w, D_MODEL)
    up_rows = tt + 2 * _ffn_pad(shift)
    scratch = ([pltpu.VMEM((tt, D_FF), BF16), pltpu.VMEM((tt, D_MODEL), BF16)]
               + [pltpu.VMEM((up_rows, FF_BLK), F32)] * (2 * (SKEW_FF + 1)))
    in_specs = [
        pl.BlockSpec(blk, tile_idx),
        pl.BlockSpec(blk, tile_idx),
        pl.BlockSpec((None, 6, D_MODEL), lambda i: (mod_row_of_batch(batch_of(i)), 0, 0)),
        pl.BlockSpec((1, D_MODEL), const),
        pl.BlockSpec((D_MODEL, 2 * D_FF), const, **resident),
        pl.BlockSpec((3, 2 * D_FF), const),
        pl.BlockSpec((1, 2 * D_FF), const),
        pl.BlockSpec((D_FF, D_MODEL), const, **resident),
    ]
    y = pl.pallas_call(
        functools.partial(_ffn_kernel, shift, seq_len),
        grid=(n_tiles,),
        in_specs=in_specs,
        out_specs=pl.BlockSpec(blk, tile_idx),
        out_shape=jax.ShapeDtypeStruct(as_blocks(x1).shape, F32),
        scratch_shapes=scratch,
        compiler_params=_cparams(1),
        name="ffn",
    )(as_blocks(h2), as_blocks(x1), mod3, w["final_norm"], w["w_up"], w["conv_ffn_w"], w["conv_ffn_b"],
      w["w_down"])
    return y.reshape(bsz, t, D_MODEL)


def _trunk(x, mod3, mod_row_of_batch, row_len, ffn_grid_w, init, want_state, tt_out, tt_in, cg, w):
    bsz, t, _ = x.shape
    x2d = x.reshape(bsz * t, D_MODEL)

    def mod_row(tile_tokens):
        return lambda i: mod_row_of_batch((i * tile_tokens) // t)

    ysc, qt, k, vt, sot, rcol, grow = _in_proj_call(x2d, mod3, mod_row(tt_in), row_len, tt_in, w)
    outs = _mlstm_call(bsz, t, cg, qt, k, vt, grow, rcol, init, want_state)
    x1, h2 = _out_proj_call(x2d, ysc, outs[0], outs[1], sot, mod3, mod_row(tt_out), tt_out, w)
    y = _ffn_call(h2.reshape(bsz, t, D_MODEL), x1.reshape(bsz, t, D_MODEL), mod3, mod_row_of_batch, ffn_grid_w, w)
    return y, outs[2:]


def kernel(x_prompt, x_sample, state_C, state_n, state_m, c, c_ctx, w_mod, b_mod, norm1, w_in, b_gate,
           conv_sc_w, conv_sc_b, mh_norm, w_out, norm2, w_up, conv_ffn_w, conv_ffn_b, w_down, final_norm):
    n_lat = c.shape[0]
    n_ctx = x_prompt.shape[0]
    c8 = jnp.concatenate([c, c_ctx[None], jnp.zeros((8 - n_lat - 1, D_MODEL), F32)], axis=0)
    l = 0
    mod3 = _mod_call(c8, w_mod[l], b_mod[l][None]).reshape(8, 6, D_MODEL)

    wi = w_in[l]
    q0 = 3 * D_CONV
    g0 = q0 + 4 * D_MLSTM
    cols = lambda a: wi[:, g0 + a * N_HEADS:g0 + (a + 1) * N_HEADS]
    bg = b_gate[l].astype(F32)
    b_g = jnp.concatenate([bg[0], bg[2], bg[1], bg[3]])[:, None]
    w = dict(
        norm1=norm1[l][None], norm2=norm2[l][None], final_norm=final_norm[None],
        w_sc=wi[:, 0:q0].astype(BF16),
        w_k=wi[:, q0 + D_MLSTM:q0 + 2 * D_MLSTM].astype(BF16),
        w_qvogt=jnp.concatenate([wi[:, q0:q0 + D_MLSTM], wi[:, q0 + 2 * D_MLSTM:g0],
                                 cols(0), cols(2), cols(1), cols(3)], axis=1).T.astype(BF16),
        b_g=b_g,
        conv_sc_w=conv_sc_w[l], conv_sc_b=conv_sc_b[l][None],
        mh_norm_col=mh_norm[l][:, None],
        w_out_a=w_out[l][0:D_CONV].astype(BF16), w_out_b=w_out[l][D_CONV:].astype(BF16),
        w_up=w_up[l].astype(BF16), conv_ffn_w=conv_ffn_w[l], conv_ffn_b=conv_ffn_b[l][None],
        w_down=w_down[l].astype(BF16),
    )

    seq = x_prompt.shape[1]
    y_prompt, (new_c, new_n, new_m) = _trunk(
        x_prompt, mod3, lambda b: n_lat, seq, None, None, True, 512, 1024, 1, w)

    init = (state_C[:, l], state_n[:, l][:, :, :, None, :],
            jnp.broadcast_to(state_m[:, l][:, :, :, None, None], state_m[:, l].shape + (1, LANES)))
    y_sample, _ = _trunk(x_sample, mod3, lambda b: b, GRID_W, GRID_W, init, False, 512, 1024, 2, w)

    new_n = new_n.reshape(n_ctx, 1, 2, N_HEADS, HEAD_DIM)
    new_m = new_m[..., 0, 0]
    return y_prompt, y_sample, new_c, new_n, new_m
```

```python
import functools

import jax
import jax.numpy as jnp
from jax import lax
from jax.experimental import pallas as pl
from jax.experimental.pallas import tpu as pltpu

F32 = jnp.float32
BF16 = jnp.bfloat16

D_MODEL = 1024
GRID_W = 64
D_CONV = 512
D_MLSTM = 512
N_HEADS = 4
HEAD_DIM = 128
D_FF = 2816
CHUNK = 256
EPS = 1e-6

LANES = 128
SUBLANES = 8
BF16_ROWS = 16
FF_BLK = 256
N_FF_BLK = D_FF // FF_BLK
CONV_ROWS = 128
FFN_TILE = 1024
FFN_COLS = 16
VMEM_LIMIT = 60 * 1024 * 1024
N_CHAINS = 2 * N_HEADS
E_ROWS = HEAD_DIM + BF16_ROWS
SKEW = 3
IN_SUB = 512
SKEW_FF = 2


def _cparams(n_axes):
    return pltpu.CompilerParams(
        dimension_semantics=("arbitrary",) * n_axes, vmem_limit_bytes=VMEM_LIMIT)


def _rms_scale(x):
    return x * lax.rsqrt(jnp.mean(x * x, axis=-1, keepdims=True) + EPS)


def _dot(a, b):
    return jnp.dot(a, b, preferred_element_type=F32)


def _dot_nt(a, b):
    return lax.dot_general(a, b, (((1,), (1,)), ((), ())), preferred_element_type=F32)


def _dot_tn(a, b):
    return lax.dot_general(a, b, (((0,), (0,)), ((), ())), preferred_element_type=F32)


def _mod_kernel(c_ref, w_ref, b_ref, o_ref):
    c = c_ref[...]
    s = c * jax.nn.sigmoid(c)
    o_ref[...] = _dot(s.astype(BF16), w_ref[...].astype(BF16)) + b_ref[...]


def _mod_call(c8, w_mod, b_mod):
    n_out = w_mod.shape[1]
    blk = 1024
    return pl.pallas_call(
        _mod_kernel,
        grid=(n_out // blk,),
        in_specs=[
            pl.BlockSpec((8, D_MODEL), lambda i: (0, 0)),
            pl.BlockSpec((D_MODEL, blk), lambda i: (0, i)),
            pl.BlockSpec((1, blk), lambda i: (0, i)),
        ],
        out_specs=pl.BlockSpec((8, blk), lambda i: (0, i)),
        out_shape=jax.ShapeDtypeStruct((8, n_out), F32),
        compiler_params=_cparams(1),
        name="mod",
    )(c8, w_mod, b_mod)


def _chunk_scan(x, pos, op, fill, reverse):
    n = x.shape[1]
    k = 1
    while k < CHUNK:
        if reverse:
            shifted = pltpu.roll(x, n - k, axis=1)
            ok = pos < CHUNK - k
        else:
            shifted = pltpu.roll(x, k, axis=1)
            ok = pos >= k
        x = op(x, jnp.where(ok, shifted, fill))
        k *= 2
    return x


def _in_proj_kernel(row_len, x_ref, mod_ref, n1_ref, wsc_ref, wk_ref, wqvogt_ref, bg_ref, cw_ref, cb_ref,
                    ysc_ref, qt_ref, k_ref, vt_ref, sot_ref, rcol_ref, grow_ref):
    tt = x_ref.shape[0]
    sub = min(tt, IN_SUB)
    mod = mod_ref[...]
    shift1, scale1 = mod[0:1], mod[1:2]

    def prep(r0):
        x = x_ref[r0:r0 + sub]
        h = (_rms_scale(x) * n1_ref[...]) * (1.0 + scale1) + shift1
        return h.astype(BF16)

    def project_mlstm(r0, hb):
        rows = slice(r0, r0 + sub)
        qvog_t = _dot_nt(wqvogt_ref[...], hb)
        qt_ref[:, rows] = (qvog_t[0:D_MLSTM] * (HEAD_DIM ** -0.5)).astype(BF16)
        vt_ref[:, rows] = qvog_t[D_MLSTM:2 * D_MLSTM].astype(BF16)
        sot_ref[:, rows] = jax.nn.sigmoid(qvog_t[2 * D_MLSTM:3 * D_MLSTM]).astype(BF16)

        g_t = qvog_t[3 * D_MLSTM:3 * D_MLSTM + 2 * N_CHAINS] + bg_ref[...]
        gi = g_t[0:N_CHAINS]
        gf = g_t[N_CHAINS:2 * N_CHAINS]
        logf = jnp.minimum(gf, 0.0) - jnp.log1p(jnp.exp(-jnp.abs(gf)))
        fwd = lax.broadcasted_iota(jnp.int32, gi.shape, 0) < N_HEADS
        cpos = lax.broadcasted_iota(jnp.int32, gi.shape, 1) % CHUNK
        b = jnp.where(fwd, _chunk_scan(logf, cpos, jnp.add, 0.0, False),
                      _chunk_scan(logf, cpos, jnp.add, 0.0, True))
        r = gi - b
        cm = jnp.where(fwd, _chunk_scan(r, cpos, jnp.maximum, -jnp.inf, False),
                       _chunk_scan(r, cpos, jnp.maximum, -jnp.inf, True))
        for c in range(sub // CHUNK):
            toks = slice(c * CHUNK, (c + 1) * CHUNK)
            grow_ref[r0 // CHUNK + c] = jnp.concatenate([cm[:, toks], r[:, toks], b[:, toks]], axis=0)
        rcol_ref[rows] = jnp.concatenate([r, jnp.zeros((LANES - N_CHAINS, sub), F32)], axis=0).T

    def project_conv(r0, hb):
        rows = slice(r0, r0 + sub)
        zb = _dot(hb, wsc_ref[:, 0:D_CONV])
        zc = _dot(hb, wsc_ref[:, D_CONV:2 * D_CONV])
        zx = _dot(hb, wsc_ref[:, 2 * D_CONV:3 * D_CONV])
        u = zc * zx
        pos = lax.broadcasted_iota(jnp.int32, u.shape, 0) % row_len
        prev = jnp.where(pos == 0, 0.0, pltpu.roll(u, 1, axis=0))
        nxt = jnp.where(pos == row_len - 1, 0.0, pltpu.roll(u, sub - 1, axis=0))
        cw = cw_ref[...]
        conv = cw[0:1] * prev + cw[1:2] * u + cw[2:3] * nxt + cb_ref[...]
        ysc_ref[rows] = (zb * conv).astype(BF16)

        k_ref[rows] = _dot(hb, wk_ref[...]).astype(BF16)

    starts = list(range(0, tt, sub))
    hb = prep(starts[0])
    for idx, r0 in enumerate(starts):
        project_mlstm(r0, hb)
        hb_next = prep(starts[idx + 1]) if idx + 1 < len(starts) else None
        project_conv(r0, hb)
        hb = hb_next


def _in_proj_call(x2d, mod3, mod_row, row_len, tt, w):
    n_tok = x2d.shape[0]
    n_tiles = n_tok // tt
    const = lambda i: (0, 0)
    tok = lambda i: (i, 0)
    tok_t = lambda i: (0, i)
    weights = [w["norm1"], w["w_sc"], w["w_k"], w["w_qvogt"], w["b_g"], w["conv_sc_w"], w["conv_sc_b"]]
    out_shapes = (
        jax.ShapeDtypeStruct((n_tok, D_CONV), BF16),
        jax.ShapeDtypeStruct((D_MLSTM, n_tok), BF16),
        jax.ShapeDtypeStruct((n_tok, D_MLSTM), BF16),
        jax.ShapeDtypeStruct((D_MLSTM, n_tok), BF16),
        jax.ShapeDtypeStruct((D_MLSTM, n_tok), BF16),
        jax.ShapeDtypeStruct((n_tok, LANES), F32),
        jax.ShapeDtypeStruct((n_tok // CHUNK, 3 * N_CHAINS, CHUNK), F32),
    )
    return pl.pallas_call(
        functools.partial(_in_proj_kernel, row_len),
        grid=(n_tiles,),
        in_specs=[
            pl.BlockSpec((tt, D_MODEL), tok),
            pl.BlockSpec((None, 6, D_MODEL), lambda i: (mod_row(i), 0, 0)),
        ] + [pl.BlockSpec(a.shape, const) for a in weights],
        out_specs=[
            pl.BlockSpec((tt, D_CONV), tok),
            pl.BlockSpec((D_MLSTM, tt), tok_t),
            pl.BlockSpec((tt, D_MLSTM), tok),
            pl.BlockSpec((D_MLSTM, tt), tok_t),
            pl.BlockSpec((D_MLSTM, tt), tok_t),
            pl.BlockSpec((tt, LANES), tok),
            pl.BlockSpec((tt // CHUNK, 3 * N_CHAINS, CHUNK), lambda i: (i, 0, 0)),
        ],
        out_shape=out_shapes,
        compiler_params=_cparams(1),
        name="in_proj",
    )(x2d, mod3, *weights)


def _mlstm_kernel(cg, nb, has_init, want_state, *refs):
    refs = list(refs)
    fwd_refs, bwd_refs = refs[0:5], refs[5:10]
    refs = refs[10:]
    if has_init:
        c0_ref, n0_ref, m0_ref = refs[:3]
        refs = refs[3:]
    htf_ref, htb_ref = refs[:2]
    refs = refs[2:]
    if want_state:
        cout_ref, nout_ref, mout_ref = refs[:3]
        refs = refs[3:]
    e_ref, m_ref = refs

    g = pl.program_id(1)

    @pl.when(g == 0)
    def _():
        if has_init:
            for d in range(2):
                for hd in range(N_HEADS):
                    j = d * N_HEADS + hd
                    e_ref[j, 0:HEAD_DIM] = c0_ref[d, hd]
                    e_ref[j, HEAD_DIM:] = jnp.broadcast_to(n0_ref[d, hd], (BF16_ROWS, HEAD_DIM))
                    m_ref[j] = jnp.broadcast_to(m0_ref[d, hd], (SUBLANES, LANES))
        else:
            e_ref[...] = jnp.zeros(e_ref.shape, F32)
            m_ref[...] = jnp.zeros(m_ref.shape, F32)

    s_idx = lax.broadcasted_iota(jnp.int32, (CHUNK, CHUNK), 0)
    t_idx = lax.broadcasted_iota(jnp.int32, (CHUNK, CHUNK), 1)
    ones_rows = jnp.ones((BF16_ROWS, CHUNK), BF16)

    def stage_scores(c, d, hd, seq):
        qt_ref, k_ref, vt_ref, row_ref, col_ref = fwd_refs if d == 0 else bwd_refs
        j = d * N_HEADS + hd
        js = seq * N_CHAINS + j
        toks = slice(c * CHUNK, (c + 1) * CHUNK)
        feat = slice(hd * HEAD_DIM, (hd + 1) * HEAD_DIM)
        last = CHUNK - 1 if d == 0 else 0

        qtc = qt_ref[feat, toks]
        kc = k_ref[toks, feat]
        vtc = vt_ref[feat, toks]
        cm_row = row_ref[c, j:j + 1, :]
        r_row = row_ref[c, N_CHAINS + j:N_CHAINS + j + 1, :]
        b_row = row_ref[c, 2 * N_CHAINS + j:2 * N_CHAINS + j + 1, :]
        m_row = m_ref[js, 0:1, :]
        m_t = m_row[:, 0:1]
        big_m_row = jnp.maximum(m_t, cm_row)
        m_last = big_m_row[:, last:last + 1]

        e = e_ref[js]
        kq = _dot(jnp.concatenate([kc, e.astype(BF16)], axis=0), qtc)

        wg = jnp.exp(r_row - m_last)
        lhs = jnp.concatenate([(vtc.astype(F32) * wg).astype(BF16),
                               jnp.broadcast_to(wg, (BF16_ROWS, CHUNK)).astype(BF16)], axis=0)
        e_ref[js] = jnp.exp(m_row - m_last) * e + _dot(lhs, kc)
        m_ref[js] = jnp.broadcast_to(b_row[:, last:last + 1] + m_last, (SUBLANES, LANES))
        return kq, vtc, big_m_row, m_t, b_row

    def stage_output(c, d, hd, seq, kq, vtc, big_m_row, m_t, b_row):
        col_ref = (fwd_refs if d == 0 else bwd_refs)[4]
        ht_ref = htf_ref if d == 0 else htb_ref
        j = d * N_HEADS + hd
        toks = slice(c * CHUNK, (c + 1) * CHUNK)
        feat = slice(hd * HEAD_DIM, (hd + 1) * HEAD_DIM)
        mask = (s_idx <= t_idx) if d == 0 else (s_idx >= t_idx)
        r_col = col_ref[toks, j:j + 1]

        w_t = jnp.exp(jnp.where(mask, r_col - big_m_row, -jnp.inf))
        s_t = (kq[0:CHUNK] * w_t).astype(BF16)
        num_t = _dot(jnp.concatenate([vtc, ones_rows], axis=0), s_t)
        nd = num_t + jnp.exp(m_t - big_m_row) * kq[CHUNK:]
        inv = 1.0 / jnp.maximum(jnp.abs(nd[HEAD_DIM:HEAD_DIM + 1]), jnp.exp(-(b_row + big_m_row)))
        ht_ref[feat, toks] = (nd[0:HEAD_DIM] * inv).astype(BF16)

    tasks = [(seq * cg + (c if d == 0 else cg - 1 - c), d, hd, seq)
             for seq in range(nb) for c in range(cg) for d in range(2) for hd in range(N_HEADS)]
    pending = []
    for task in tasks:
        pending.append(task + stage_scores(*task))
        if len(pending) > SKEW:
            stage_output(*pending.pop(0))
    for args in pending:
        stage_output(*args)

    if want_state:
        @pl.when(g == pl.num_programs(1) - 1)
        def _():
            for seq in range(nb):
                for d in range(2):
                    for hd in range(N_HEADS):
                        js = seq * N_CHAINS + d * N_HEADS + hd
                        cout_ref[seq, d, hd] = e_ref[js, 0:HEAD_DIM]
                        nout_ref[seq, d, hd] = e_ref[js, HEAD_DIM:HEAD_DIM + 1]
                        mout_ref[seq, d, hd] = m_ref[js, 0:1, :]


def _mlstm_call(bsz, t, cg, nb, qt, k, vt, grow, rcol, init, want_state):
    n_tok = bsz * t
    n_groups = t // (cg * CHUNK)
    assert nb == 1 or (n_groups == 1 and init is None)
    tg = nb * cg * CHUNK
    fwd = lambda b, g: b * n_groups + g
    bwd = lambda b, g: b * n_groups + (n_groups - 1 - g)

    def stream_specs(pos):
        return [
            pl.BlockSpec((D_MLSTM, tg), lambda b, g: (0, pos(b, g))),
            pl.BlockSpec((tg, D_MLSTM), lambda b, g: (pos(b, g), 0)),
            pl.BlockSpec((D_MLSTM, tg), lambda b, g: (0, pos(b, g))),
            pl.BlockSpec((nb * cg, 3 * N_CHAINS, CHUNK), lambda b, g: (pos(b, g), 0, 0)),
            pl.BlockSpec((tg, LANES), lambda b, g: (pos(b, g), 0)),
        ]

    in_specs = stream_specs(fwd) + stream_specs(bwd)
    args = [qt, k, vt, grow, rcol] * 2
    if init is not None:
        c0, n0, m0 = init
        in_specs += [
            pl.BlockSpec((None, 2, N_HEADS, HEAD_DIM, HEAD_DIM), lambda b, g: (b, 0, 0, 0, 0)),
            pl.BlockSpec((None, 2, N_HEADS, 1, HEAD_DIM), lambda b, g: (b, 0, 0, 0, 0)),
            pl.BlockSpec((None, 2, N_HEADS, 1, LANES), lambda b, g: (b, 0, 0, 0, 0)),
        ]
        args += [c0, n0, m0]
    out_specs = [
        pl.BlockSpec((D_MLSTM, tg), lambda b, g: (0, fwd(b, g))),
        pl.BlockSpec((D_MLSTM, tg), lambda b, g: (0, bwd(b, g))),
    ]
    out_shape = [jax.ShapeDtypeStruct((D_MLSTM, n_tok), BF16)] * 2
    if want_state:
        out_specs += [
            pl.BlockSpec((nb, None, 2, N_HEADS, HEAD_DIM, HEAD_DIM), lambda b, g: (b, 0, 0, 0, 0, 0)),
            pl.BlockSpec((nb, None, 2, N_HEADS, 1, HEAD_DIM), lambda b, g: (b, 0, 0, 0, 0, 0)),
            pl.BlockSpec((nb, None, 2, N_HEADS, 1, LANES), lambda b, g: (b, 0, 0, 0, 0, 0)),
        ]
        out_shape += [
            jax.ShapeDtypeStruct((bsz, 1, 2, N_HEADS, HEAD_DIM, HEAD_DIM), F32),
            jax.ShapeDtypeStruct((bsz, 1, 2, N_HEADS, 1, HEAD_DIM), F32),
            jax.ShapeDtypeStruct((bsz, 1, 2, N_HEADS, 1, LANES), F32),
        ]
    return pl.pallas_call(
        functools.partial(_mlstm_kernel, cg, nb, init is not None, want_state),
        grid=(bsz // nb, n_groups),
        in_specs=in_specs,
        out_specs=out_specs,
        out_shape=out_shape,
        scratch_shapes=[
            pltpu.VMEM((nb * N_CHAINS, E_ROWS, HEAD_DIM), F32),
            pltpu.VMEM((nb * N_CHAINS, SUBLANES, LANES), F32),
        ],
        compiler_params=_cparams(2),
        name="mlstm",
    )(*args)


def _out_proj_kernel(x_ref, ysc_ref, htf_ref, htb_ref, sot_ref, mh_ref, mod_ref, n2_ref, wa_ref, wb_ref,
                     x1_ref, h2_ref):
    mod = mod_ref[...]
    gate1, shift2, scale2 = mod[2:3], mod[3:4], mod[4:5]
    hs = htf_ref[...].astype(F32) + htb_ref[...].astype(F32)
    heads = []
    for hd in range(N_HEADS):
        blk = hs[hd * HEAD_DIM:(hd + 1) * HEAD_DIM]
        heads.append(blk * lax.rsqrt(jnp.mean(blk * blk, axis=0, keepdims=True) + EPS))
    hm_t = ((jnp.concatenate(heads, axis=0) * mh_ref[...]) * sot_ref[...].astype(F32)).astype(BF16)
    mix = _dot(ysc_ref[...], wa_ref[...]) + _dot_tn(hm_t, wb_ref[...])
    x1 = x_ref[...] + gate1 * mix
    x1_ref[...] = x1
    h2_ref[...] = ((_rms_scale(x1) * n2_ref[...]) * (1.0 + scale2) + shift2).astype(BF16)


def _out_proj_call(x2d, ysc, htf, htb, sot, mod3, mod_row, tt, w):
    n_tok = x2d.shape[0]
    const = lambda i: (0, 0)
    tok = lambda i: (i, 0)
    tok_t = lambda i: (0, i)
    return pl.pallas_call(
        _out_proj_kernel,
        grid=(n_tok // tt,),
        in_specs=[
            pl.BlockSpec((tt, D_MODEL), tok),
            pl.BlockSpec((tt, D_CONV), tok),
            pl.BlockSpec((D_MLSTM, tt), tok_t),
            pl.BlockSpec((D_MLSTM, tt), tok_t),
            pl.BlockSpec((D_MLSTM, tt), tok_t),
            pl.BlockSpec((D_MLSTM, 1), const),
            pl.BlockSpec((None, 6, D_MODEL), lambda i: (mod_row(i), 0, 0)),
            pl.BlockSpec((1, D_MODEL), const),
            pl.BlockSpec(w["w_out_a"].shape, const),
            pl.BlockSpec(w["w_out_b"].shape, const),
        ],
        out_specs=[pl.BlockSpec((tt, D_MODEL), tok), pl.BlockSpec((tt, D_MODEL), tok)],
        out_shape=[jax.ShapeDtypeStruct((n_tok, D_MODEL), F32),
                   jax.ShapeDtypeStruct((n_tok, D_MODEL), BF16)],
        compiler_params=_cparams(1),
        name="out_proj",
    )(x2d, ysc, htf, htb, sot, w["mh_norm_col"], mod3, w["norm2"], w["w_out_a"], w["w_out_b"])


def _ffn_kernel(shift, seq_len, h2_ref, x1_ref, mod_ref, fn_ref, wu_ref, cw_ref, cb_ref, wd_ref, y_ref,
                act_ref, h2s_ref, *slot_refs):
    slots = tuple(zip(slot_refs[0::2], slot_refs[1::2]))
    tt = act_ref.shape[0]
    pad = _ffn_pad(shift)
    nr = CONV_ROWS
    row = lax.broadcasted_iota(jnp.int32, (nr, FF_BLK), 0)

    def up_proj(blk, slot):
        a_ref, g_ref = slots[slot]
        off_a = pl.multiple_of(blk * FF_BLK, FF_BLK)
        off_g = pl.multiple_of(blk * FF_BLK + D_FF, FF_BLK)
        hin = h2s_ref[...]
        a_ref[pad:pad + tt] = _dot(hin, wu_ref[:, pl.ds(off_a, FF_BLK)])
        g_ref[pad:pad + tt] = _dot(hin, wu_ref[:, pl.ds(off_g, FF_BLK)])

    def conv_act(blk, slot):
        a_ref, g_ref = slots[slot]
        off_a = pl.multiple_of(blk * FF_BLK, FF_BLK)
        off_g = pl.multiple_of(blk * FF_BLK + D_FF, FF_BLK)

        def conv(z_ref, off, r0):
            cw = cw_ref[:, pl.ds(off, FF_BLK)]
            cb = cb_ref[:, pl.ds(off, FF_BLK)]
            prev = z_ref[pad - shift + r0:pad - shift + r0 + nr]
            nxt = z_ref[pad + shift + r0:pad + shift + r0 + nr]
            if seq_len is not None:
                if r0 % seq_len == 0:
                    prev = jnp.where(row == 0, 0.0, prev)
                if (r0 + nr) % seq_len == 0:
                    nxt = jnp.where(row == nr - 1, 0.0, nxt)
            return cw[0:1] * prev + cw[1:2] * z_ref[pad + r0:pad + r0 + nr] + cw[2:3] * nxt + cb

        for r0 in range(0, tt, nr):
            ac = conv(a_ref, off_a, r0)
            gc = conv(g_ref, off_g, r0)
            act_ref[r0:r0 + nr, pl.ds(off_a, FF_BLK)] = (gc * jax.nn.sigmoid(gc) * ac).astype(BF16)

    h2s_ref[...] = h2_ref[...].reshape(tt, D_MODEL)
    for a_ref, g_ref in slots:
        for z_ref in (a_ref, g_ref):
            z_ref[0:pad] = jnp.zeros((pad, FF_BLK), F32)
            z_ref[pad + tt:] = jnp.zeros((pad, FF_BLK), F32)

    n_slots = len(slots)
    for blk in range(SKEW_FF):
        up_proj(blk, blk % n_slots)

    def block_group(p, carry):
        for s in range(n_slots):
            blk = n_slots * p + SKEW_FF + s
            up_proj(blk, (SKEW_FF + s) % n_slots)
            conv_act(blk - SKEW_FF, s % n_slots)
        return carry

    n_groups = (N_FF_BLK - SKEW_FF) // n_slots
    lax.fori_loop(0, n_groups, block_group, 0)
    done = n_slots * n_groups + SKEW_FF
    for blk in range(done, N_FF_BLK):
        up_proj(blk, blk % n_slots)
        conv_act(blk - SKEW_FF, (blk - SKEW_FF) % n_slots)

    k0 = (N_FF_BLK - SKEW_FF) * FF_BLK
    part = _dot(act_ref[:, 0:k0], wd_ref[0:k0, :])
    for blk in range(N_FF_BLK - SKEW_FF, N_FF_BLK):
        conv_act(blk, blk % n_slots)
    ffn = part + _dot(act_ref[:, k0:], wd_ref[k0:, :])
    x2 = x1_ref[...].reshape(tt, D_MODEL) + mod_ref[5:6] * ffn
    y_ref[...] = (_rms_scale(x2) * fn_ref[...]).reshape(y_ref.shape)


def _ffn_pad(shift):
    return -(-shift // SUBLANES) * SUBLANES


def _ffn_call(h2, x1, mod3, mod_row_of_batch, grid_w, w):
    bsz, t, _ = x1.shape
    const = lambda i: (0, 0)
    resident = dict(pipeline_mode=pl.Buffered(1))
    if grid_w is None:
        tt, shift, seq_len = FFN_TILE, 1, t
        per_tile = tt // t
        blk = (per_tile, t, D_MODEL)
        n_tiles = bsz // per_tile
        tile_idx = lambda i: (i, 0, 0)
        batch_of = lambda i: i * per_tile
        as_blocks = lambda a: a
    else:
        rows = t // grid_w
        tt, shift, seq_len = rows * FFN_COLS, FFN_COLS, None
        strips = grid_w // FFN_COLS
        blk = (None, rows, FFN_COLS, D_MODEL)
        n_tiles = bsz * strips
        tile_idx = lambda i: (i // strips, 0, i % strips, 0)
        batch_of = lambda i: i // strips
        as_blocks = lambda a: a.reshape(bsz, rows, grid_w, D_MODEL)
    up_rows = tt + 2 * _ffn_pad(shift)
    scratch = ([pltpu.VMEM((tt, D_FF), BF16), pltpu.VMEM((tt, D_MODEL), BF16)]
               + [pltpu.VMEM((up_rows, FF_BLK), F32)] * (2 * (SKEW_FF + 1)))
    in_specs = [
        pl.BlockSpec(blk, tile_idx),
        pl.BlockSpec(blk, tile_idx),
        pl.BlockSpec((None, 6, D_MODEL), lambda i: (mod_row_of_batch(batch_of(i)), 0, 0)),
        pl.BlockSpec((1, D_MODEL), const),
        pl.BlockSpec((D_MODEL, 2 * D_FF), const, **resident),
        pl.BlockSpec((3, 2 * D_FF), const),
        pl.BlockSpec((1, 2 * D_FF), const),
        pl.BlockSpec((D_FF, D_MODEL), const, **resident),
    ]
    y = pl.pallas_call(
        functools.partial(_ffn_kernel, shift, seq_len),
        grid=(n_tiles,),
        in_specs=in_specs,
        out_specs=pl.BlockSpec(blk, tile_idx),
        out_shape=jax.ShapeDtypeStruct(as_blocks(x1).shape, F32),
        scratch_shapes=scratch,
        compiler_params=_cparams(1),
        name="ffn",
    )(as_blocks(h2), as_blocks(x1), mod3, w["final_norm"], w["w_up"], w["conv_ffn_w"], w["conv_ffn_b"],
      w["w_down"])
    return y.reshape(bsz, t, D_MODEL)


def _trunk(x, mod3, mod_row_of_batch, row_len, ffn_grid_w, init, want_state, tt_out, tt_in, cg, nb, w):
    bsz, t, _ = x.shape
    x2d = x.reshape(bsz * t, D_MODEL)

    def mod_row(tile_tokens):
        return lambda i: mod_row_of_batch((i * tile_tokens) // t)

    ysc, qt, k, vt, sot, rcol, grow = _in_proj_call(x2d, mod3, mod_row(tt_in), row_len, tt_in, w)
    outs = _mlstm_call(bsz, t, cg, nb, qt, k, vt, grow, rcol, init, want_state)
    x1, h2 = _out_proj_call(x2d, ysc, outs[0], outs[1], sot, mod3, mod_row(tt_out), tt_out, w)
    y = _ffn_call(h2.reshape(bsz, t, D_MODEL), x1.reshape(bsz, t, D_MODEL), mod3, mod_row_of_batch, ffn_grid_w, w)
    return y, outs[2:]


def kernel(x_prompt, x_sample, state_C, state_n, state_m, c, c_ctx, w_mod, b_mod, norm1, w_in, b_gate,
           conv_sc_w, conv_sc_b, mh_norm, w_out, norm2, w_up, conv_ffn_w, conv_ffn_b, w_down, final_norm):
    n_lat = c.shape[0]
    n_ctx = x_prompt.shape[0]
    c8 = jnp.concatenate([c, c_ctx[None], jnp.zeros((8 - n_lat - 1, D_MODEL), F32)], axis=0)
    l = 0
    mod3 = _mod_call(c8, w_mod[l], b_mod[l][None]).reshape(8, 6, D_MODEL)

    wi = w_in[l]
    q0 = 3 * D_CONV
    g0 = q0 + 4 * D_MLSTM
    cols = lambda a: wi[:, g0 + a * N_HEADS:g0 + (a + 1) * N_HEADS]
    bg = b_gate[l].astype(F32)
    b_g = jnp.concatenate([bg[0], bg[2], bg[1], bg[3]])[:, None]
    w = dict(
        norm1=norm1[l][None], norm2=norm2[l][None], final_norm=final_norm[None],
        w_sc=wi[:, 0:q0].astype(BF16),
        w_k=wi[:, q0 + D_MLSTM:q0 + 2 * D_MLSTM].astype(BF16),
        w_qvogt=jnp.concatenate([wi[:, q0:q0 + D_MLSTM], wi[:, q0 + 2 * D_MLSTM:g0],
                                 cols(0), cols(2), cols(1), cols(3)], axis=1).T.astype(BF16),
        b_g=b_g,
        conv_sc_w=conv_sc_w[l], conv_sc_b=conv_sc_b[l][None],
        mh_norm_col=mh_norm[l][:, None],
        w_out_a=w_out[l][0:D_CONV].astype(BF16), w_out_b=w_out[l][D_CONV:].astype(BF16),
        w_up=w_up[l].astype(BF16), conv_ffn_w=conv_ffn_w[l], conv_ffn_b=conv_ffn_b[l][None],
        w_down=w_down[l].astype(BF16),
    )

    seq = x_prompt.shape[1]
    y_prompt, (new_c, new_n, new_m) = _trunk(
        x_prompt, mod3, lambda b: n_lat, seq, None, None, True, 512, 1024, 1, 4, w)

    init = (state_C[:, l], state_n[:, l][:, :, :, None, :],
            jnp.broadcast_to(state_m[:, l][:, :, :, None, None], state_m[:, l].shape + (1, LANES)))
    y_sample, _ = _trunk(x_sample, mod3, lambda b: b, GRID_W, GRID_W, init, False, 512, 1024, 2, 1, w)

    new_n = new_n.reshape(n_ctx, 1, 2, N_HEADS, HEAD_DIM)
    new_m = new_m[..., 0, 0]
    return y_prompt, y_sample, new_c, new_n, new_m
```

```python
import functools

import jax
import jax.numpy as jnp
from jax import lax
from jax.experimental import pallas as pl
from jax.experimental.pallas import tpu as pltpu

F32 = jnp.float32
BF16 = jnp.bfloat16

D_MODEL = 1024
GRID_W = 64
D_CONV = 512
D_MLSTM = 512
N_HEADS = 4
HEAD_DIM = 128
D_FF = 2816
CHUNK = 256
EPS = 1e-6

LANES = 128
SUBLANES = 8
BF16_ROWS = 16
FF_BLK = 256
N_FF_BLK = D_FF // FF_BLK
CONV_ROWS = 128
FFN_TILE = 1024
FFN_COLS = 16
VMEM_LIMIT = 60 * 1024 * 1024
N_CHAINS = 2 * N_HEADS
E_ROWS = HEAD_DIM + BF16_ROWS
SKEW = 3
IN_SUB = 512
SKEW_FF = 2


def _cparams(n_axes):
    return pltpu.CompilerParams(
        dimension_semantics=("arbitrary",) * n_axes, vmem_limit_bytes=VMEM_LIMIT)


def _rms_scale(x):
    return x * lax.rsqrt(jnp.mean(x * x, axis=-1, keepdims=True) + EPS)


def _dot(a, b):
    return jnp.dot(a, b, preferred_element_type=F32)


def _dot_nt(a, b):
    return lax.dot_general(a, b, (((1,), (1,)), ((), ())), preferred_element_type=F32)


def _dot_tn(a, b):
    return lax.dot_general(a, b, (((0,), (0,)), ((), ())), preferred_element_type=F32)


def _mod_kernel(c_ref, w_ref, b_ref, o_ref):
    c = c_ref[...]
    s = c * jax.nn.sigmoid(c)
    o_ref[...] = _dot(s.astype(BF16), w_ref[...].astype(BF16)) + b_ref[...]


def _mod_call(c8, w_mod, b_mod):
    n_out = w_mod.shape[1]
    blk = 1024
    return pl.pallas_call(
        _mod_kernel,
        grid=(n_out // blk,),
        in_specs=[
            pl.BlockSpec((8, D_MODEL), lambda i: (0, 0)),
            pl.BlockSpec((D_MODEL, blk), lambda i: (0, i)),
            pl.BlockSpec((1, blk), lambda i: (0, i)),
        ],
        out_specs=pl.BlockSpec((8, blk), lambda i: (0, i)),
        out_shape=jax.ShapeDtypeStruct((8, n_out), F32),
        compiler_params=_cparams(1),
        name="mod",
    )(c8, w_mod, b_mod)


def _chunk_scan(x, pos, op, fill, reverse):
    n = x.shape[1]
    k = 1
    while k < CHUNK:
        if reverse:
            shifted = pltpu.roll(x, n - k, axis=1)
            ok = pos < CHUNK - k
        else:
            shifted = pltpu.roll(x, k, axis=1)
            ok = pos >= k
        x = op(x, jnp.where(ok, shifted, fill))
        k *= 2
    return x


def _in_proj_kernel(row_len, x_ref, mod_ref, n1_ref, wsc_ref, wk_ref, wqvogt_ref, bg_ref, cw_ref, cb_ref,
                    ysc_ref, qt_ref, k_ref, vt_ref, sot_ref, rcol_ref, grow_ref):
    tt = x_ref.shape[0]
    sub = min(tt, IN_SUB)
    mod = mod_ref[...]
    shift1, scale1 = mod[0:1], mod[1:2]

    def prep(r0):
        x = x_ref[r0:r0 + sub]
        h = (_rms_scale(x) * n1_ref[...]) * (1.0 + scale1) + shift1
        return h.astype(BF16)

    def project_mlstm(r0, hb):
        rows = slice(r0, r0 + sub)
        qvog_t = _dot_nt(wqvogt_ref[...], hb)
        qt_ref[:, rows] = (qvog_t[0:D_MLSTM] * (HEAD_DIM ** -0.5)).astype(BF16)
        vt_ref[:, rows] = qvog_t[D_MLSTM:2 * D_MLSTM].astype(BF16)
        sot_ref[:, rows] = jax.nn.sigmoid(qvog_t[2 * D_MLSTM:3 * D_MLSTM]).astype(BF16)

        g_t = qvog_t[3 * D_MLSTM:3 * D_MLSTM + 2 * N_CHAINS] + bg_ref[...]
        gi = jnp.concatenate([g_t[0:N_HEADS], g_t[2 * N_HEADS:3 * N_HEADS]], axis=0)
        gf = jnp.concatenate([g_t[N_HEADS:2 * N_HEADS], g_t[3 * N_HEADS:4 * N_HEADS]], axis=0)
        logf = jnp.minimum(gf, 0.0) - jnp.log1p(jnp.exp(-jnp.abs(gf)))
        fwd = lax.broadcasted_iota(jnp.int32, gi.shape, 0) < N_HEADS
        cpos = lax.broadcasted_iota(jnp.int32, gi.shape, 1) % CHUNK
        b = jnp.where(fwd, _chunk_scan(logf, cpos, jnp.add, 0.0, False),
                      _chunk_scan(logf, cpos, jnp.add, 0.0, True))
        r = gi - b
        cm = jnp.where(fwd, _chunk_scan(r, cpos, jnp.maximum, -jnp.inf, False),
                       _chunk_scan(r, cpos, jnp.maximum, -jnp.inf, True))
        for c in range(sub // CHUNK):
            toks = slice(c * CHUNK, (c + 1) * CHUNK)
            grow_ref[r0 // CHUNK + c] = jnp.concatenate([cm[:, toks], r[:, toks], b[:, toks]], axis=0)
        rcol_ref[rows] = jnp.concatenate([r, jnp.zeros((LANES - N_CHAINS, sub), F32)], axis=0).T

    def project_conv(r0, hb):
        rows = slice(r0, r0 + sub)
        zb = _dot(hb, wsc_ref[:, 0:D_CONV])
        zc = _dot(hb, wsc_ref[:, D_CONV:2 * D_CONV])
        zx = _dot(hb, wsc_ref[:, 2 * D_CONV:3 * D_CONV])
        u = zc * zx
        pos = lax.broadcasted_iota(jnp.int32, u.shape, 0) % row_len
        prev = jnp.where(pos == 0, 0.0, pltpu.roll(u, 1, axis=0))
        nxt = jnp.where(pos == row_len - 1, 0.0, pltpu.roll(u, sub - 1, axis=0))
        cw = cw_ref[...]
        conv = cw[0:1] * prev + cw[1:2] * u + cw[2:3] * nxt + cb_ref[...]
        ysc_ref[rows] = (zb * conv).astype(BF16)

        k_ref[rows] = _dot(hb, wk_ref[...]).astype(BF16)

    starts = list(range(0, tt, sub))
    hb = prep(starts[0])
    for idx, r0 in enumerate(starts):
        project_mlstm(r0, hb)
        hb_next = prep(starts[idx + 1]) if idx + 1 < len(starts) else None
        project_conv(r0, hb)
        hb = hb_next


def _in_proj_call(x2d, mod3, mod_row, row_len, tt, w):
    n_tok = x2d.shape[0]
    n_tiles = n_tok // tt
    const = lambda i: (0, 0)
    tok = lambda i: (i, 0)
    tok_t = lambda i: (0, i)
    weights = [w["norm1"], w["w_sc"], w["w_k"], w["w_qvogt"], w["b_g"], w["conv_sc_w"], w["conv_sc_b"]]
    out_shapes = (
        jax.ShapeDtypeStruct((n_tok, D_CONV), BF16),
        jax.ShapeDtypeStruct((D_MLSTM, n_tok), BF16),
        jax.ShapeDtypeStruct((n_tok, D_MLSTM), BF16),
        jax.ShapeDtypeStruct((D_MLSTM, n_tok), BF16),
        jax.ShapeDtypeStruct((D_MLSTM, n_tok), BF16),
        jax.ShapeDtypeStruct((n_tok, LANES), F32),
        jax.ShapeDtypeStruct((n_tok // CHUNK, 3 * N_CHAINS, CHUNK), F32),
    )
    return pl.pallas_call(
        functools.partial(_in_proj_kernel, row_len),
        grid=(n_tiles,),
        in_specs=[
            pl.BlockSpec((tt, D_MODEL), tok),
            pl.BlockSpec((None, 6, D_MODEL), lambda i: (mod_row(i), 0, 0)),
        ] + [pl.BlockSpec(a.shape, const) for a in weights],
        out_specs=[
            pl.BlockSpec((tt, D_CONV), tok),
            pl.BlockSpec((D_MLSTM, tt), tok_t),
            pl.BlockSpec((tt, D_MLSTM), tok),
            pl.BlockSpec((D_MLSTM, tt), tok_t),
            pl.BlockSpec((D_MLSTM, tt), tok_t),
            pl.BlockSpec((tt, LANES), tok),
            pl.BlockSpec((tt // CHUNK, 3 * N_CHAINS, CHUNK), lambda i: (i, 0, 0)),
        ],
        out_shape=out_shapes,
        compiler_params=_cparams(1),
        name="in_proj",
    )(x2d, mod3, *weights)


def _mlstm_kernel(cg, nb, has_init, want_state, *refs):
    refs = list(refs)
    fwd_refs, bwd_refs = refs[0:5], refs[5:10]
    refs = refs[10:]
    if has_init:
        c0_ref, n0_ref, m0_ref = refs[:3]
        refs = refs[3:]
    htf_ref, htb_ref = refs[:2]
    refs = refs[2:]
    if want_state:
        cout_ref, nout_ref, mout_ref = refs[:3]
        refs = refs[3:]
    e_ref, m_ref = refs

    g = pl.program_id(1)

    @pl.when(g == 0)
    def _():
        if has_init:
            for d in range(2):
                for hd in range(N_HEADS):
                    j = d * N_HEADS + hd
                    e_ref[j, 0:HEAD_DIM] = c0_ref[d, hd]
                    e_ref[j, HEAD_DIM:] = jnp.broadcast_to(n0_ref[d, hd], (BF16_ROWS, HEAD_DIM))
                    m_ref[j] = jnp.broadcast_to(m0_ref[d, hd], (SUBLANES, LANES))
        else:
            e_ref[...] = jnp.zeros(e_ref.shape, F32)
            m_ref[...] = jnp.zeros(m_ref.shape, F32)

    s_idx = lax.broadcasted_iota(jnp.int32, (CHUNK, CHUNK), 0)
    t_idx = lax.broadcasted_iota(jnp.int32, (CHUNK, CHUNK), 1)
    ones_rows = jnp.ones((BF16_ROWS, CHUNK), BF16)

    def stage_scores(c, d, hd, seq):
        qt_ref, k_ref, vt_ref, row_ref, col_ref = fwd_refs if d == 0 else bwd_refs
        j = d * N_HEADS + hd
        js = seq * N_CHAINS + j
        toks = slice(c * CHUNK, (c + 1) * CHUNK)
        feat = slice(hd * HEAD_DIM, (hd + 1) * HEAD_DIM)
        last = CHUNK - 1 if d == 0 else 0

        qtc = qt_ref[feat, toks]
        kc = k_ref[toks, feat]
        vtc = vt_ref[feat, toks]
        cm_row = row_ref[c, j:j + 1, :]
        r_row = row_ref[c, N_CHAINS + j:N_CHAINS + j + 1, :]
        b_row = row_ref[c, 2 * N_CHAINS + j:2 * N_CHAINS + j + 1, :]
        m_row = m_ref[js, 0:1, :]
        m_t = m_row[:, 0:1]
        big_m_row = jnp.maximum(m_t, cm_row)
        m_last = big_m_row[:, last:last + 1]

        e = e_ref[js]
        kq = _dot(jnp.concatenate([kc, e.astype(BF16)], axis=0), qtc)

        wg = jnp.exp(r_row - m_last)
        lhs = jnp.concatenate([(vtc.astype(F32) * wg).astype(BF16),
                               jnp.broadcast_to(wg, (BF16_ROWS, CHUNK)).astype(BF16)], axis=0)
        e_ref[js] = jnp.exp(m_row - m_last) * e + _dot(lhs, kc)
        m_ref[js] = jnp.broadcast_to(b_row[:, last:last + 1] + m_last, (SUBLANES, LANES))
        return kq, vtc, big_m_row, m_t, b_row

    def stage_output(c, d, hd, seq, kq, vtc, big_m_row, m_t, b_row):
        col_ref = (fwd_refs if d == 0 else bwd_refs)[4]
        ht_ref = htf_ref if d == 0 else htb_ref
        j = d * N_HEADS + hd
        toks = slice(c * CHUNK, (c + 1) * CHUNK)
        feat = slice(hd * HEAD_DIM, (hd + 1) * HEAD_DIM)
        mask = (s_idx <= t_idx) if d == 0 else (s_idx >= t_idx)
        r_col = col_ref[toks, j:j + 1]

        w_t = jnp.exp(jnp.where(mask, r_col - big_m_row, -jnp.inf))
        s_t = (kq[0:CHUNK] * w_t).astype(BF16)
        num_t = _dot(jnp.concatenate([vtc, ones_rows], axis=0), s_t)
        nd = num_t + jnp.exp(m_t - big_m_row) * kq[CHUNK:]
        inv = 1.0 / jnp.maximum(jnp.abs(nd[HEAD_DIM:HEAD_DIM + 1]), jnp.exp(-(b_row + big_m_row)))
        ht_ref[feat, toks] = (nd[0:HEAD_DIM] * inv).astype(BF16)

    tasks = [(seq * cg + (c if d == 0 else cg - 1 - c), d, hd, seq)
             for seq in range(nb) for c in range(cg) for d in range(2) for hd in range(N_HEADS)]
    pending = []
    for task in tasks:
        pending.append(task + stage_scores(*task))
        if len(pending) > SKEW:
            stage_output(*pending.pop(0))
    for args in pending:
        stage_output(*args)

    if want_state:
        @pl.when(g == pl.num_programs(1) - 1)
        def _():
            for seq in range(nb):
                for d in range(2):
                    for hd in range(N_HEADS):
                        js = seq * N_CHAINS + d * N_HEADS + hd
                        cout_ref[seq, d, hd] = e_ref[js, 0:HEAD_DIM]
                        nout_ref[seq, d, hd] = e_ref[js, HEAD_DIM:HEAD_DIM + 1]
                        mout_ref[seq, d, hd] = m_ref[js, 0:1, :]


def _mlstm_call(bsz, t, cg, nb, qt, k, vt, grow, rcol, init, want_state):
    n_tok = bsz * t
    n_groups = t // (cg * CHUNK)
    assert nb == 1 or (n_groups == 1 and init is None)
    tg = nb * cg * CHUNK
    fwd = lambda b, g: b * n_groups + g
    bwd = lambda b, g: b * n_groups + (n_groups - 1 - g)

    def stream_specs(pos):
        return [
            pl.BlockSpec((D_MLSTM, tg), lambda b, g: (0, pos(b, g))),
            pl.BlockSpec((tg, D_MLSTM), lambda b, g: (pos(b, g), 0)),
            pl.BlockSpec((D_MLSTM, tg), lambda b, g: (0, pos(b, g))),
            pl.BlockSpec((nb * cg, 3 * N_CHAINS, CHUNK), lambda b, g: (pos(b, g), 0, 0)),
            pl.BlockSpec((tg, LANES), lambda b, g: (pos(b, g), 0)),
        ]

    in_specs = stream_specs(fwd) + stream_specs(bwd)
    args = [qt, k, vt, grow, rcol] * 2
    if init is not None:
        c0, n0, m0 = init
        in_specs += [
            pl.BlockSpec((None, 2, N_HEADS, HEAD_DIM, HEAD_DIM), lambda b, g: (b, 0, 0, 0, 0)),
            pl.BlockSpec((None, 2, N_HEADS, 1, HEAD_DIM), lambda b, g: (b, 0, 0, 0, 0)),
            pl.BlockSpec((None, 2, N_HEADS, 1, LANES), lambda b, g: (b, 0, 0, 0, 0)),
        ]
        args += [c0, n0, m0]
    out_specs = [
        pl.BlockSpec((D_MLSTM, tg), lambda b, g: (0, fwd(b, g))),
        pl.BlockSpec((D_MLSTM, tg), lambda b, g: (0, bwd(b, g))),
    ]
    out_shape = [jax.ShapeDtypeStruct((D_MLSTM, n_tok), BF16)] * 2
    if want_state:
        out_specs += [
            pl.BlockSpec((nb, None, 2, N_HEADS, HEAD_DIM, HEAD_DIM), lambda b, g: (b, 0, 0, 0, 0, 0)),
            pl.BlockSpec((nb, None, 2, N_HEADS, 1, HEAD_DIM), lambda b, g: (b, 0, 0, 0, 0, 0)),
            pl.BlockSpec((nb, None, 2, N_HEADS, 1, LANES), lambda b, g: (b, 0, 0, 0, 0, 0)),
        ]
        out_shape += [
            jax.ShapeDtypeStruct((bsz, 1, 2, N_HEADS, HEAD_DIM, HEAD_DIM), F32),
            jax.ShapeDtypeStruct((bsz, 1, 2, N_HEADS, 1, HEAD_DIM), F32),
            jax.ShapeDtypeStruct((bsz, 1, 2, N_HEADS, 1, LANES), F32),
        ]
    return pl.pallas_call(
        functools.partial(_mlstm_kernel, cg, nb, init is not None, want_state),
        grid=(bsz // nb, n_groups),
        in_specs=in_specs,
        out_specs=out_specs,
        out_shape=out_shape,
        scratch_shapes=[
            pltpu.VMEM((nb * N_CHAINS, E_ROWS, HEAD_DIM), F32),
            pltpu.VMEM((nb * N_CHAINS, SUBLANES, LANES), F32),
        ],
        compiler_params=_cparams(2),
        name="mlstm",
    )(*args)


def _out_proj_kernel(x_ref, ysc_ref, htf_ref, htb_ref, sot_ref, mh_ref, mod_ref, n2_ref, wa_ref, wb_ref,
                     x1_ref, h2_ref):
    mod = mod_ref[...]
    gate1, shift2, scale2 = mod[2:3], mod[3:4], mod[4:5]
    hs = htf_ref[...].astype(F32) + htb_ref[...].astype(F32)
    heads = []
    for hd in range(N_HEADS):
        blk = hs[hd * HEAD_DIM:(hd + 1) * HEAD_DIM]
        heads.append(blk * lax.rsqrt(jnp.mean(blk * blk, axis=0, keepdims=True) + EPS))
    hm_t = ((jnp.concatenate(heads, axis=0) * mh_ref[...]) * sot_ref[...].astype(F32)).astype(BF16)
    mix = _dot(ysc_ref[...], wa_ref[...]) + _dot_tn(hm_t, wb_ref[...])
    x1 = x_ref[...] + gate1 * mix
    x1_ref[...] = x1
    h2_ref[...] = ((_rms_scale(x1) * n2_ref[...]) * (1.0 + scale2) + shift2).astype(BF16)


def _out_proj_call(x2d, ysc, htf, htb, sot, mod3, mod_row, tt, w):
    n_tok = x2d.shape[0]
    const = lambda i: (0, 0)
    tok = lambda i: (i, 0)
    tok_t = lambda i: (0, i)
    return pl.pallas_call(
        _out_proj_kernel,
        grid=(n_tok // tt,),
        in_specs=[
            pl.BlockSpec((tt, D_MODEL), tok),
            pl.BlockSpec((tt, D_CONV), tok),
            pl.BlockSpec((D_MLSTM, tt), tok_t),
            pl.BlockSpec((D_MLSTM, tt), tok_t),
            pl.BlockSpec((D_MLSTM, tt), tok_t),
            pl.BlockSpec((D_MLSTM, 1), const),
            pl.BlockSpec((None, 6, D_MODEL), lambda i: (mod_row(i), 0, 0)),
            pl.BlockSpec((1, D_MODEL), const),
            pl.BlockSpec(w["w_out_a"].shape, const),
            pl.BlockSpec(w["w_out_b"].shape, const),
        ],
        out_specs=[pl.BlockSpec((tt, D_MODEL), tok), pl.BlockSpec((tt, D_MODEL), tok)],
        out_shape=[jax.ShapeDtypeStruct((n_tok, D_MODEL), F32),
                   jax.ShapeDtypeStruct((n_tok, D_MODEL), BF16)],
        compiler_params=_cparams(1),
        name="out_proj",
    )(x2d, ysc, htf, htb, sot, w["mh_norm_col"], mod3, w["norm2"], w["w_out_a"], w["w_out_b"])


def _ffn_kernel(shift, seq_len, h2_ref, x1_ref, mod_ref, fn_ref, wu_ref, cw_ref, cb_ref, wd_ref, y_ref,
                act_ref, h2s_ref, *slot_refs):
    slots = tuple(zip(slot_refs[0::2], slot_refs[1::2]))
    tt = act_ref.shape[0]
    pad = _ffn_pad(shift)
    nr = CONV_ROWS
    row = lax.broadcasted_iota(jnp.int32, (nr, FF_BLK), 0)

    def up_proj(blk, slot):
        a_ref, g_ref = slots[slot]
        off_a = pl.multiple_of(blk * FF_BLK, FF_BLK)
        off_g = pl.multiple_of(blk * FF_BLK + D_FF, FF_BLK)
        hin = h2s_ref[...]
        a_ref[pad:pad + tt] = _dot(hin, wu_ref[:, pl.ds(off_a, FF_BLK)])
        g_ref[pad:pad + tt] = _dot(hin, wu_ref[:, pl.ds(off_g, FF_BLK)])

    def conv_act(blk, slot):
        a_ref, g_ref = slots[slot]
        off_a = pl.multiple_of(blk * FF_BLK, FF_BLK)
        off_g = pl.multiple_of(blk * FF_BLK + D_FF, FF_BLK)

        def conv(z_ref, off, r0):
            cw = cw_ref[:, pl.ds(off, FF_BLK)]
            cb = cb_ref[:, pl.ds(off, FF_BLK)]
            prev = z_ref[pad - shift + r0:pad - shift + r0 + nr]
            nxt = z_ref[pad + shift + r0:pad + shift + r0 + nr]
            if seq_len is not None:
                if r0 % seq_len == 0:
                    prev = jnp.where(row == 0, 0.0, prev)
                if (r0 + nr) % seq_len == 0:
                    nxt = jnp.where(row == nr - 1, 0.0, nxt)
            return cw[0:1] * prev + cw[1:2] * z_ref[pad + r0:pad + r0 + nr] + cw[2:3] * nxt + cb

        for r0 in range(0, tt, nr):
            ac = conv(a_ref, off_a, r0)
            gc = conv(g_ref, off_g, r0)
            act_ref[r0:r0 + nr, pl.ds(off_a, FF_BLK)] = (gc * jax.nn.sigmoid(gc) * ac).astype(BF16)

    h2s_ref[...] = h2_ref[...].reshape(tt, D_MODEL)
    for a_ref, g_ref in slots:
        for z_ref in (a_ref, g_ref):
            z_ref[0:pad] = jnp.zeros((pad, FF_BLK), F32)
            z_ref[pad + tt:] = jnp.zeros((pad, FF_BLK), F32)

    n_slots = len(slots)
    for blk in range(SKEW_FF):
        up_proj(blk, blk % n_slots)

    def block_group(p, carry):
        for s in range(n_slots):
            blk = n_slots * p + SKEW_FF + s
            up_proj(blk, (SKEW_FF + s) % n_slots)
            conv_act(blk - SKEW_FF, s % n_slots)
        return carry

    n_groups = (N_FF_BLK - SKEW_FF) // n_slots
    lax.fori_loop(0, n_groups, block_group, 0)
    done = n_slots * n_groups + SKEW_FF
    for blk in range(done, N_FF_BLK):
        up_proj(blk, blk % n_slots)
        conv_act(blk - SKEW_FF, (blk - SKEW_FF) % n_slots)

    k0 = (N_FF_BLK - SKEW_FF) * FF_BLK
    part = _dot(act_ref[:, 0:k0], wd_ref[0:k0, :])
    for blk in range(N_FF_BLK - SKEW_FF, N_FF_BLK):
        conv_act(blk, blk % n_slots)
    ffn = part + _dot(act_ref[:, k0:], wd_ref[k0:, :])
    x2 = x1_ref[...].reshape(tt, D_MODEL) + mod_ref[5:6] * ffn
    y_ref[...] = (_rms_scale(x2) * fn_ref[...]).reshape(y_ref.shape)


def _ffn_pad(shift):
    return -(-shift // SUBLANES) * SUBLANES


def _ffn_call(h2, x1, mod3, mod_row_of_batch, grid_w, w):
    bsz, t, _ = x1.shape
    const = lambda i: (0, 0)
    resident = dict(pipeline_mode=pl.Buffered(1))
    if grid_w is None:
        tt, shift, seq_len = FFN_TILE, 1, t
        per_tile = tt // t
        blk = (per_tile, t, D_MODEL)
        n_tiles = bsz // per_tile
        tile_idx = lambda i: (i, 0, 0)
        batch_of = lambda i: i * per_tile
        as_blocks = lambda a: a
    else:
        rows = t // grid_w
        tt, shift, seq_len = rows * FFN_COLS, FFN_COLS, None
        strips = grid_w // FFN_COLS
        blk = (None, rows, FFN_COLS, D_MODEL)
        n_tiles = bsz * strips
        tile_idx = lambda i: (i // strips, 0, i % strips, 0)
        batch_of = lambda i: i // strips
        as_blocks = lambda a: a.reshape(bsz, rows, grid_w, D_MODEL)
    up_rows = tt + 2 * _ffn_pad(shift)
    scratch = ([pltpu.VMEM((tt, D_FF), BF16), pltpu.VMEM((tt, D_MODEL), BF16)]
               + [pltpu.VMEM((up_rows, FF_BLK), F32)] * (2 * (SKEW_FF + 1)))
    in_specs = [
        pl.BlockSpec(blk, tile_idx),
        pl.BlockSpec(blk, tile_idx),
        pl.BlockSpec((None, 6, D_MODEL), lambda i: (mod_row_of_batch(batch_of(i)), 0, 0)),
        pl.BlockSpec((1, D_MODEL), const),
        pl.BlockSpec((D_MODEL, 2 * D_FF), const, **resident),
        pl.BlockSpec((3, 2 * D_FF), const),
        pl.BlockSpec((1, 2 * D_FF), const),
        pl.BlockSpec((D_FF, D_MODEL), const, **resident),
    ]
    y = pl.pallas_call(
        functools.partial(_ffn_kernel, shift, seq_len),
        grid=(n_tiles,),
        in_specs=in_specs,
        out_specs=pl.BlockSpec(blk, tile_idx),
        out_shape=jax.ShapeDtypeStruct(as_blocks(x1).shape, F32),
        scratch_shapes=scratch,
        compiler_params=_cparams(1),
        name="ffn",
    )(as_blocks(h2), as_blocks(x1), mod3, w["final_norm"], w["w_up"], w["conv_ffn_w"], w["conv_ffn_b"],
      w["w_down"])
    return y.reshape(bsz, t, D_MODEL)


def _trunk(x, mod3, mod_row_of_batch, row_len, ffn_grid_w, init, want_state, tt_out, tt_in, cg, nb, w):
    bsz, t, _ = x.shape
    x2d = x.reshape(bsz * t, D_MODEL)

    def mod_row(tile_tokens):
        return lambda i: mod_row_of_batch((i * tile_tokens) // t)

    ysc, qt, k, vt, sot, rcol, grow = _in_proj_call(x2d, mod3, mod_row(tt_in), row_len, tt_in, w)
    outs = _mlstm_call(bsz, t, cg, nb, qt, k, vt, grow, rcol, init, want_state)
    x1, h2 = _out_proj_call(x2d, ysc, outs[0], outs[1], sot, mod3, mod_row(tt_out), tt_out, w)
    y = _ffn_call(h2.reshape(bsz, t, D_MODEL), x1.reshape(bsz, t, D_MODEL), mod3, mod_row_of_batch, ffn_grid_w, w)
    return y, outs[2:]


def kernel(x_prompt, x_sample, state_C, state_n, state_m, c, c_ctx, w_mod, b_mod, norm1, w_in, b_gate,
           conv_sc_w, conv_sc_b, mh_norm, w_out, norm2, w_up, conv_ffn_w, conv_ffn_b, w_down, final_norm):
    n_lat = c.shape[0]
    n_ctx = x_prompt.shape[0]
    c8 = jnp.concatenate([c, c_ctx[None], jnp.zeros((8 - n_lat - 1, D_MODEL), F32)], axis=0)
    l = 0
    mod3 = _mod_call(c8, w_mod[l], b_mod[l][None]).reshape(8, 6, D_MODEL)

    wi = w_in[l]
    q0 = 3 * D_CONV
    w = dict(
        norm1=norm1[l][None], norm2=norm2[l][None], final_norm=final_norm[None],
        w_sc=wi[:, 0:q0].astype(BF16),
        w_k=wi[:, q0 + D_MLSTM:q0 + 2 * D_MLSTM].astype(BF16),
        w_qvogt=jnp.concatenate([wi[:, q0:q0 + D_MLSTM], wi[:, q0 + 2 * D_MLSTM:]], axis=1).T.astype(BF16),
        b_g=b_gate[l].astype(F32).reshape(2 * N_CHAINS, 1),
        conv_sc_w=conv_sc_w[l], conv_sc_b=conv_sc_b[l][None],
        mh_norm_col=mh_norm[l][:, None],
        w_out_a=w_out[l][0:D_CONV].astype(BF16), w_out_b=w_out[l][D_CONV:].astype(BF16),
        w_up=w_up[l].astype(BF16), conv_ffn_w=conv_ffn_w[l], conv_ffn_b=conv_ffn_b[l][None],
        w_down=w_down[l].astype(BF16),
    )

    seq = x_prompt.shape[1]
    y_prompt, (new_c, new_n, new_m) = _trunk(
        x_prompt, mod3, lambda b: n_lat, seq, None, None, True, 512, 1024, 1, 4, w)

    init = (state_C[:, l], state_n[:, l][:, :, :, None, :],
            jnp.broadcast_to(state_m[:, l][:, :, :, None, None], state_m[:, l].shape + (1, LANES)))
    y_sample, _ = _trunk(x_sample, mod3, lambda b: b, GRID_W, GRID_W, init, False, 512, 1024, 2, 1, w)

    new_n = new_n.reshape(n_ctx, 1, 2, N_HEADS, HEAD_DIM)
    new_m = new_m[..., 0, 0]
    return y_prompt, y_sample, new_c, new_n, new_m
```

```python
import functools

import jax
import jax.numpy as jnp
from jax import lax
from jax.experimental import pallas as pl
from jax.experimental.pallas import tpu as pltpu

F32 = jnp.float32
BF16 = jnp.bfloat16

D_MODEL = 1024
GRID_W = 64
D_CONV = 512
D_MLSTM = 512
N_HEADS = 4
HEAD_DIM = 128
D_FF = 2816
CHUNK = 256
EPS = 1e-6

LANES = 128
SUBLANES = 8
BF16_ROWS = 16
FF_BLK = 256
N_FF_BLK = D_FF // FF_BLK
CONV_ROWS = 128
FFN_TILE = 1024
FFN_COLS = 16
VMEM_LIMIT = 60 * 1024 * 1024
N_CHAINS = 2 * N_HEADS
E_ROWS = HEAD_DIM + BF16_ROWS
SKEW = 3
IN_SUB = 512
SKEW_FF = 2


def _cparams(n_axes):
    return pltpu.CompilerParams(
        dimension_semantics=("arbitrary",) * n_axes, vmem_limit_bytes=VMEM_LIMIT)


def _rms_scale(x):
    return x * lax.rsqrt(jnp.mean(x * x, axis=-1, keepdims=True) + EPS)


def _dot(a, b):
    return jnp.dot(a, b, preferred_element_type=F32)


def _dot_nt(a, b):
    return lax.dot_general(a, b, (((1,), (1,)), ((), ())), preferred_element_type=F32)


def _dot_tn(a, b):
    return lax.dot_general(a, b, (((0,), (0,)), ((), ())), preferred_element_type=F32)


def _mod_kernel(c_ref, w_ref, b_ref, o_ref):
    c = c_ref[...]
    s = c * jax.nn.sigmoid(c)
    o_ref[...] = _dot(s.astype(BF16), w_ref[...].astype(BF16)) + b_ref[...]


def _mod_call(c8, w_mod, b_mod):
    n_out = w_mod.shape[1]
    blk = 1024
    return pl.pallas_call(
        _mod_kernel,
        grid=(n_out // blk,),
        in_specs=[
            pl.BlockSpec((8, D_MODEL), lambda i: (0, 0)),
            pl.BlockSpec((D_MODEL, blk), lambda i: (0, i)),
            pl.BlockSpec((1, blk), lambda i: (0, i)),
        ],
        out_specs=pl.BlockSpec((8, blk), lambda i: (0, i)),
        out_shape=jax.ShapeDtypeStruct((8, n_out), F32),
        compiler_params=_cparams(1),
        name="mod",
    )(c8, w_mod, b_mod)


def _chunk_scan(x, pos, op, fill, reverse):
    n = x.shape[1]
    k = 1
    while k < CHUNK:
        if reverse:
            shifted = pltpu.roll(x, n - k, axis=1)
            ok = pos < CHUNK - k
        else:
            shifted = pltpu.roll(x, k, axis=1)
            ok = pos >= k
        x = op(x, jnp.where(ok, shifted, fill))
        k *= 2
    return x


def _in_proj_kernel(row_len, x_ref, mod_ref, n1_ref, wsc_ref, wk_ref, wqvogt_ref, bg_ref, cw_ref, cb_ref,
                    ysc_ref, qt_ref, k_ref, vt_ref, sot_ref, rcol_ref, grow_ref):
    tt = x_ref.shape[0]
    sub = min(tt, IN_SUB)
    mod = mod_ref[...]
    shift1, scale1 = mod[0:1], mod[1:2]

    def prep(r0):
        x = x_ref[r0:r0 + sub]
        h = (_rms_scale(x) * n1_ref[...]) * (1.0 + scale1) + shift1
        return h.astype(BF16)

    def project_mlstm(r0, hb):
        rows = slice(r0, r0 + sub)
        qvog_t = _dot_nt(wqvogt_ref[...], hb)
        qt_ref[:, rows] = (qvog_t[0:D_MLSTM] * (HEAD_DIM ** -0.5)).astype(BF16)
        vt_ref[:, rows] = qvog_t[D_MLSTM:2 * D_MLSTM].astype(BF16)
        sot_ref[:, rows] = jax.nn.sigmoid(qvog_t[2 * D_MLSTM:3 * D_MLSTM]).astype(BF16)

        g_t = qvog_t[3 * D_MLSTM:3 * D_MLSTM + 2 * N_CHAINS] + bg_ref[...]
        gi = jnp.concatenate([g_t[0:N_HEADS], g_t[2 * N_HEADS:3 * N_HEADS]], axis=0)
        gf = jnp.concatenate([g_t[N_HEADS:2 * N_HEADS], g_t[3 * N_HEADS:4 * N_HEADS]], axis=0)
        logf = jnp.minimum(gf, 0.0) - jnp.log1p(jnp.exp(-jnp.abs(gf)))
        fwd = lax.broadcasted_iota(jnp.int32, gi.shape, 0) < N_HEADS
        cpos = lax.broadcasted_iota(jnp.int32, gi.shape, 1) % CHUNK
        b = jnp.where(fwd, _chunk_scan(logf, cpos, jnp.add, 0.0, False),
                      _chunk_scan(logf, cpos, jnp.add, 0.0, True))
        r = gi - b
        cm = jnp.where(fwd, _chunk_scan(r, cpos, jnp.maximum, -jnp.inf, False),
                       _chunk_scan(r, cpos, jnp.maximum, -jnp.inf, True))
        for c in range(sub // CHUNK):
            toks = slice(c * CHUNK, (c + 1) * CHUNK)
            grow_ref[r0 // CHUNK + c] = jnp.concatenate([cm[:, toks], r[:, toks], b[:, toks]], axis=0)
        rcol_ref[rows] = jnp.concatenate([r, jnp.zeros((LANES - N_CHAINS, sub), F32)], axis=0).T

    def project_conv(r0, hb):
        rows = slice(r0, r0 + sub)
        zb = _dot(hb, wsc_ref[:, 0:D_CONV])
        zc = _dot(hb, wsc_ref[:, D_CONV:2 * D_CONV])
        zx = _dot(hb, wsc_ref[:, 2 * D_CONV:3 * D_CONV])
        u = zc * zx
        pos = lax.broadcasted_iota(jnp.int32, u.shape, 0) % row_len
        prev = jnp.where(pos == 0, 0.0, pltpu.roll(u, 1, axis=0))
        nxt = jnp.where(pos == row_len - 1, 0.0, pltpu.roll(u, sub - 1, axis=0))
        cw = cw_ref[...]
        conv = cw[0:1] * prev + cw[1:2] * u + cw[2:3] * nxt + cb_ref[...]
        ysc_ref[rows] = (zb * conv).astype(BF16)

        k_ref[rows] = _dot(hb, wk_ref[...]).astype(BF16)

    starts = list(range(0, tt, sub))
    hb = prep(starts[0])
    for idx, r0 in enumerate(starts):
        project_mlstm(r0, hb)
        hb_next = prep(starts[idx + 1]) if idx + 1 < len(starts) else None
        project_conv(r0, hb)
        hb = hb_next


def _in_proj_call(x2d, mod3, mod_row, row_len, tt, w):
    n_tok = x2d.shape[0]
    n_tiles = n_tok // tt
    const = lambda i: (0, 0)
    tok = lambda i: (i, 0)
    tok_t = lambda i: (0, i)
    weights = [w["norm1"], w["w_sc"], w["w_k"], w["w_qvogt"], w["b_g"], w["conv_sc_w"], w["conv_sc_b"]]
    out_shapes = (
        jax.ShapeDtypeStruct((n_tok, D_CONV), BF16),
        jax.ShapeDtypeStruct((D_MLSTM, n_tok), BF16),
        jax.ShapeDtypeStruct((n_tok, D_MLSTM), BF16),
        jax.ShapeDtypeStruct((D_MLSTM, n_tok), BF16),
        jax.ShapeDtypeStruct((D_MLSTM, n_tok), BF16),
        jax.ShapeDtypeStruct((n_tok, LANES), F32),
        jax.ShapeDtypeStruct((n_tok // CHUNK, 3 * N_CHAINS, CHUNK), F32),
    )
    return pl.pallas_call(
        functools.partial(_in_proj_kernel, row_len),
        grid=(n_tiles,),
        in_specs=[
            pl.BlockSpec((tt, D_MODEL), tok),
            pl.BlockSpec((None, 6, D_MODEL), lambda i: (mod_row(i), 0, 0)),
        ] + [pl.BlockSpec(a.shape, const) for a in weights],
        out_specs=[
            pl.BlockSpec((tt, D_CONV), tok),
            pl.BlockSpec((D_MLSTM, tt), tok_t),
            pl.BlockSpec((tt, D_MLSTM), tok),
            pl.BlockSpec((D_MLSTM, tt), tok_t),
            pl.BlockSpec((D_MLSTM, tt), tok_t),
            pl.BlockSpec((tt, LANES), tok),
            pl.BlockSpec((tt // CHUNK, 3 * N_CHAINS, CHUNK), lambda i: (i, 0, 0)),
        ],
        out_shape=out_shapes,
        compiler_params=_cparams(1),
        name="in_proj",
    )(x2d, mod3, *weights)


def _mlstm_kernel(cg, nb, has_init, want_state, *refs):
    refs = list(refs)
    fwd_refs, bwd_refs = refs[0:5], refs[5:10]
    refs = refs[10:]
    if has_init:
        c0_ref, n0_ref, m0_ref = refs[:3]
        refs = refs[3:]
    htf_ref, htb_ref = refs[:2]
    refs = refs[2:]
    if want_state:
        cout_ref, nout_ref, mout_ref = refs[:3]
        refs = refs[3:]
    e_ref, m_ref = refs

    g = pl.program_id(1)

    @pl.when(g == 0)
    def _():
        if has_init:
            for d in range(2):
                for hd in range(N_HEADS):
                    j = d * N_HEADS + hd
                    e_ref[j, 0:HEAD_DIM] = c0_ref[d, hd]
                    e_ref[j, HEAD_DIM:] = jnp.broadcast_to(n0_ref[d, hd], (BF16_ROWS, HEAD_DIM))
                    m_ref[j] = jnp.broadcast_to(m0_ref[d, hd], (SUBLANES, LANES))
        else:
            e_ref[...] = jnp.zeros(e_ref.shape, F32)
            m_ref[...] = jnp.zeros(m_ref.shape, F32)

    s_idx = lax.broadcasted_iota(jnp.int32, (CHUNK, CHUNK), 0)
    t_idx = lax.broadcasted_iota(jnp.int32, (CHUNK, CHUNK), 1)
    ones_rows = jnp.ones((BF16_ROWS, CHUNK), BF16)

    def stage_scores(c, d, hd, seq):
        qt_ref, k_ref, vt_ref, row_ref, col_ref = fwd_refs if d == 0 else bwd_refs
        j = d * N_HEADS + hd
        js = seq * N_CHAINS + j
        toks = slice(c * CHUNK, (c + 1) * CHUNK)
        feat = slice(hd * HEAD_DIM, (hd + 1) * HEAD_DIM)
        last = CHUNK - 1 if d == 0 else 0

        qtc = qt_ref[feat, toks]
        kc = k_ref[toks, feat]
        vtc = vt_ref[feat, toks]
        cm_row = row_ref[c, j:j + 1, :]
        r_row = row_ref[c, N_CHAINS + j:N_CHAINS + j + 1, :]
        b_row = row_ref[c, 2 * N_CHAINS + j:2 * N_CHAINS + j + 1, :]
        m_row = m_ref[js, 0:1, :]
        m_t = m_row[:, 0:1]
        big_m_row = jnp.maximum(m_t, cm_row)
        m_last = big_m_row[:, last:last + 1]

        e = e_ref[js]
        kq = _dot(jnp.concatenate([kc, e.astype(BF16)], axis=0), qtc)

        wg = jnp.exp(r_row - m_last)
        lhs = jnp.concatenate([(vtc.astype(F32) * wg).astype(BF16),
                               jnp.broadcast_to(wg, (BF16_ROWS, CHUNK)).astype(BF16)], axis=0)
        e_ref[js] = jnp.exp(m_row - m_last) * e + _dot(lhs, kc)
        m_ref[js] = jnp.broadcast_to(b_row[:, last:last + 1] + m_last, (SUBLANES, LANES))
        return kq, vtc, big_m_row, m_t, b_row

    def stage_output(c, d, hd, seq, kq, vtc, big_m_row, m_t, b_row):
        col_ref = (fwd_refs if d == 0 else bwd_refs)[4]
        ht_ref = htf_ref if d == 0 else htb_ref
        j = d * N_HEADS + hd
        toks = slice(c * CHUNK, (c + 1) * CHUNK)
        feat = slice(hd * HEAD_DIM, (hd + 1) * HEAD_DIM)
        mask = (s_idx <= t_idx) if d == 0 else (s_idx >= t_idx)
        r_col = col_ref[toks, j:j + 1]

        w_t = jnp.exp(jnp.where(mask, r_col - big_m_row, -jnp.inf))
        s_t = (kq[0:CHUNK] * w_t).astype(BF16)
        num_t = _dot(jnp.concatenate([vtc, ones_rows], axis=0), s_t)
        nd = num_t + jnp.exp(m_t - big_m_row) * kq[CHUNK:]
        inv = 1.0 / jnp.maximum(jnp.abs(nd[HEAD_DIM:HEAD_DIM + 1]), jnp.exp(-(b_row + big_m_row)))
        ht_ref[feat, toks] = (nd[0:HEAD_DIM] * inv).astype(BF16)

    tasks = [(seq * cg + (c if d == 0 else cg - 1 - c), d, hd, seq)
             for seq in range(nb) for c in range(cg) for d in range(2) for hd in range(N_HEADS)]
    pending = []
    for task in tasks:
        pending.append(task + stage_scores(*task))
        if len(pending) > SKEW:
            stage_output(*pending.pop(0))
    for args in pending:
        stage_output(*args)

    if want_state:
        @pl.when(g == pl.num_programs(1) - 1)
        def _():
            for seq in range(nb):
                for d in range(2):
                    for hd in range(N_HEADS):
                        js = seq * N_CHAINS + d * N_HEADS + hd
                        cout_ref[seq, d, hd] = e_ref[js, 0:HEAD_DIM]
                        nout_ref[seq, d, hd] = e_ref[js, HEAD_DIM:HEAD_DIM + 1]
                        mout_ref[seq, d, hd] = m_ref[js, 0:1, :]


def _mlstm_call(bsz, t, cg, nb, qt, k, vt, grow, rcol, init, want_state):
    n_tok = bsz * t
    n_groups = t // (cg * CHUNK)
    assert nb == 1 or (n_groups == 1 and init is None)
    tg = nb * cg * CHUNK
    fwd = lambda b, g: b * n_groups + g
    bwd = lambda b, g: b * n_groups + (n_groups - 1 - g)

    def stream_specs(pos):
        return [
            pl.BlockSpec((D_MLSTM, tg), lambda b, g: (0, pos(b, g))),
            pl.BlockSpec((tg, D_MLSTM), lambda b, g: (pos(b, g), 0)),
            pl.BlockSpec((D_MLSTM, tg), lambda b, g: (0, pos(b, g))),
            pl.BlockSpec((nb * cg, 3 * N_CHAINS, CHUNK), lambda b, g: (pos(b, g), 0, 0)),
            pl.BlockSpec((tg, LANES), lambda b, g: (pos(b, g), 0)),
        ]

    in_specs = stream_specs(fwd) + stream_specs(bwd)
    args = [qt, k, vt, grow, rcol] * 2
    if init is not None:
        c0, n0, m0 = init
        in_specs += [
            pl.BlockSpec((None, 2, N_HEADS, HEAD_DIM, HEAD_DIM), lambda b, g: (b, 0, 0, 0, 0)),
            pl.BlockSpec((None, 2, N_HEADS, 1, HEAD_DIM), lambda b, g: (b, 0, 0, 0, 0)),
            pl.BlockSpec((None, 2, N_HEADS, 1, LANES), lambda b, g: (b, 0, 0, 0, 0)),
        ]
        args += [c0, n0, m0]
    out_specs = [
        pl.BlockSpec((D_MLSTM, tg), lambda b, g: (0, fwd(b, g))),
        pl.BlockSpec((D_MLSTM, tg), lambda b, g: (0, bwd(b, g))),
    ]
    out_shape = [jax.ShapeDtypeStruct((D_MLSTM, n_tok), BF16)] * 2
    if want_state:
        out_specs += [
            pl.BlockSpec((nb, None, 2, N_HEADS, HEAD_DIM, HEAD_DIM), lambda b, g: (b, 0, 0, 0, 0, 0)),
            pl.BlockSpec((nb, None, 2, N_HEADS, 1, HEAD_DIM), lambda b, g: (b, 0, 0, 0, 0, 0)),
            pl.BlockSpec((nb, None, 2, N_HEADS, 1, LANES), lambda b, g: (b, 0, 0, 0, 0, 0)),
        ]
        out_shape += [
            jax.ShapeDtypeStruct((bsz, 1, 2, N_HEADS, HEAD_DIM, HEAD_DIM), F32),
            jax.ShapeDtypeStruct((bsz, 1, 2, N_HEADS, 1, HEAD_DIM), F32),
            jax.ShapeDtypeStruct((bsz, 1, 2, N_HEADS, 1, LANES), F32),
        ]
    return pl.pallas_call(
        functools.partial(_mlstm_kernel, cg, nb, init is not None, want_state),
        grid=(bsz // nb, n_groups),
        in_specs=in_specs,
        out_specs=out_specs,
        out_shape=out_shape,
        scratch_shapes=[
            pltpu.VMEM((nb * N_CHAINS, E_ROWS, HEAD_DIM), F32),
            pltpu.VMEM((nb * N_CHAINS, SUBLANES, LANES), F32),
        ],
        compiler_params=_cparams(2),
        name="mlstm",
    )(*args)


def _out_proj_kernel(x_ref, ysc_ref, htf_ref, htb_ref, sot_ref, mh_ref, mod_ref, n2_ref, wa_ref, wb_ref,
                     x1_ref, h2_ref):
    mod = mod_ref[...]
    gate1, shift2, scale2 = mod[2:3], mod[3:4], mod[4:5]
    hs = htf_ref[...].astype(F32) + htb_ref[...].astype(F32)
    heads = []
    for hd in range(N_HEADS):
        blk = hs[hd * HEAD_DIM:(hd + 1) * HEAD_DIM]
        heads.append(blk * lax.rsqrt(jnp.mean(blk * blk, axis=0, keepdims=True) + EPS))
    hm_t = ((jnp.concatenate(heads, axis=0) * mh_ref[...]) * sot_ref[...].astype(F32)).astype(BF16)
    mix = _dot(ysc_ref[...], wa_ref[...]) + _dot_tn(hm_t, wb_ref[...])
    x1 = x_ref[...] + gate1 * mix
    x1_ref[...] = x1
    h2_ref[...] = ((_rms_scale(x1) * n2_ref[...]) * (1.0 + scale2) + shift2).astype(BF16)


def _out_proj_call(x2d, ysc, htf, htb, sot, mod3, mod_row, tt, w):
    n_tok = x2d.shape[0]
    const = lambda i: (0, 0)
    tok = lambda i: (i, 0)
    tok_t = lambda i: (0, i)
    return pl.pallas_call(
        _out_proj_kernel,
        grid=(n_tok // tt,),
        in_specs=[
            pl.BlockSpec((tt, D_MODEL), tok),
            pl.BlockSpec((tt, D_CONV), tok),
            pl.BlockSpec((D_MLSTM, tt), tok_t),
            pl.BlockSpec((D_MLSTM, tt), tok_t),
            pl.BlockSpec((D_MLSTM, tt), tok_t),
            pl.BlockSpec((D_MLSTM, 1), const),
            pl.BlockSpec((None, 6, D_MODEL), lambda i: (mod_row(i), 0, 0)),
            pl.BlockSpec((1, D_MODEL), const),
            pl.BlockSpec(w["w_out_a"].shape, const),
            pl.BlockSpec(w["w_out_b"].shape, const),
        ],
        out_specs=[pl.BlockSpec((tt, D_MODEL), tok), pl.BlockSpec((tt, D_MODEL), tok)],
        out_shape=[jax.ShapeDtypeStruct((n_tok, D_MODEL), F32),
                   jax.ShapeDtypeStruct((n_tok, D_MODEL), BF16)],
        compiler_params=_cparams(1),
        name="out_proj",
    )(x2d, ysc, htf, htb, sot, w["mh_norm_col"], mod3, w["norm2"], w["w_out_a"], w["w_out_b"])


def _ffn_kernel(shift, seq_len, h2_ref, x1_ref, mod_ref, fn_ref, wu_ref, cw_ref, cb_ref, wd_ref, y_ref,
                act_ref, h2s_ref, *slot_refs):
    slots = tuple(zip(slot_refs[0::2], slot_refs[1::2]))
    tt = act_ref.shape[0]
    pad = _ffn_pad(shift)
    nr = CONV_ROWS if seq_len is None else seq_len
    row = lax.broadcasted_iota(jnp.int32, (nr, FF_BLK), 0)

    def up_proj(blk, slot):
        a_ref, g_ref = slots[slot]
        off_a = pl.multiple_of(blk * FF_BLK, FF_BLK)
        off_g = pl.multiple_of(blk * FF_BLK + D_FF, FF_BLK)
        hin = h2s_ref[...]
        a_ref[pad:pad + tt] = _dot(hin, wu_ref[:, pl.ds(off_a, FF_BLK)])
        g_ref[pad:pad + tt] = _dot(hin, wu_ref[:, pl.ds(off_g, FF_BLK)])

    def conv_act(blk, slot):
        a_ref, g_ref = slots[slot]
        off_a = pl.multiple_of(blk * FF_BLK, FF_BLK)
        off_g = pl.multiple_of(blk * FF_BLK + D_FF, FF_BLK)

        def conv(z_ref, off, r0):
            cw = cw_ref[:, pl.ds(off, FF_BLK)]
            cb = cb_ref[:, pl.ds(off, FF_BLK)]
            prev = z_ref[pad - shift + r0:pad - shift + r0 + nr]
            nxt = z_ref[pad + shift + r0:pad + shift + r0 + nr]
            if seq_len is not None:
                if r0 % seq_len == 0:
                    prev = jnp.where(row == 0, 0.0, prev)
                if (r0 + nr) % seq_len == 0:
                    nxt = jnp.where(row == nr - 1, 0.0, nxt)
            return cw[0:1] * prev + cw[1:2] * z_ref[pad + r0:pad + r0 + nr] + cw[2:3] * nxt + cb

        for r0 in range(0, tt, nr):
            ac = conv(a_ref, off_a, r0)
            gc = conv(g_ref, off_g, r0)
            act_ref[r0:r0 + nr, pl.ds(off_a, FF_BLK)] = (gc * jax.nn.sigmoid(gc) * ac).astype(BF16)

    h2s_ref[...] = h2_ref[...].reshape(tt, D_MODEL)
    for a_ref, g_ref in slots:
        for z_ref in (a_ref, g_ref):
            z_ref[0:pad] = jnp.zeros((pad, FF_BLK), F32)
            z_ref[pad + tt:] = jnp.zeros((pad, FF_BLK), F32)

    n_slots = len(slots)
    for blk in range(SKEW_FF):
        up_proj(blk, blk % n_slots)

    def block_group(p, carry):
        for s in range(n_slots):
            blk = n_slots * p + SKEW_FF + s
            up_proj(blk, (SKEW_FF + s) % n_slots)
            conv_act(blk - SKEW_FF, s % n_slots)
        return carry

    n_groups = (N_FF_BLK - SKEW_FF) // n_slots
    lax.fori_loop(0, n_groups, block_group, 0)
    done = n_slots * n_groups + SKEW_FF
    for blk in range(done, N_FF_BLK):
        up_proj(blk, blk % n_slots)
        conv_act(blk - SKEW_FF, (blk - SKEW_FF) % n_slots)

    k0 = (N_FF_BLK - SKEW_FF) * FF_BLK
    part = _dot(act_ref[:, 0:k0], wd_ref[0:k0, :])
    for blk in range(N_FF_BLK - SKEW_FF, N_FF_BLK):
        conv_act(blk, blk % n_slots)
    ffn = part + _dot(act_ref[:, k0:], wd_ref[k0:, :])
    x2 = x1_ref[...].reshape(tt, D_MODEL) + mod_ref[5:6] * ffn
    y_ref[...] = (_rms_scale(x2) * fn_ref[...]).reshape(y_ref.shape)


def _ffn_pad(shift):
    return -(-shift // SUBLANES) * SUBLANES


def _ffn_call(h2, x1, mod3, mod_row_of_batch, grid_w, w):
    bsz, t, _ = x1.shape
    const = lambda i: (0, 0)
    resident = dict(pipeline_mode=pl.Buffered(1))
    if grid_w is None:
        tt, shift, seq_len = FFN_TILE, 1, t
        per_tile = tt // t
        blk = (per_tile, t, D_MODEL)
        n_tiles = bsz // per_tile
        tile_idx = lambda i: (i, 0, 0)
        batch_of = lambda i: i * per_tile
        as_blocks = lambda a: a
    else:
        rows = t // grid_w
        tt, shift, seq_len = rows * FFN_COLS, FFN_COLS, None
        strips = grid_w // FFN_COLS
        blk = (None, rows, FFN_COLS, D_MODEL)
        n_tiles = bsz * strips
        tile_idx = lambda i: (i // strips, 0, i % strips, 0)
        batch_of = lambda i: i // strips
        as_blocks = lambda a: a.reshape(bsz, rows, grid_w, D_MODEL)
    up_rows = tt + 2 * _ffn_pad(shift)
    scratch = ([pltpu.VMEM((tt, D_FF), BF16), pltpu.VMEM((tt, D_MODEL), BF16)]
               + [pltpu.VMEM((up_rows, FF_BLK), F32)] * (2 * (SKEW_FF + 1)))
    in_specs = [
        pl.BlockSpec(blk, tile_idx),
        pl.BlockSpec(blk, tile_idx),
        pl.BlockSpec((None, 6, D_MODEL), lambda i: (mod_row_of_batch(batch_of(i)), 0, 0)),
        pl.BlockSpec((1, D_MODEL), const),
        pl.BlockSpec((D_MODEL, 2 * D_FF), const, **resident),
        pl.BlockSpec((3, 2 * D_FF), const),
        pl.BlockSpec((1, 2 * D_FF), const),
        pl.BlockSpec((D_FF, D_MODEL), const, **resident),
    ]
    y = pl.pallas_call(
        functools.partial(_ffn_kernel, shift, seq_len),
        grid=(n_tiles,),
        in_specs=in_specs,
        out_specs=pl.BlockSpec(blk, tile_idx),
        out_shape=jax.ShapeDtypeStruct(as_blocks(x1).shape, F32),
        scratch_shapes=scratch,
        compiler_params=_cparams(1),
        name="ffn",
    )(as_blocks(h2), as_blocks(x1), mod3, w["final_norm"], w["w_up"], w["conv_ffn_w"], w["conv_ffn_b"],
      w["w_down"])
    return y.reshape(bsz, t, D_MODEL)


def _trunk(x, mod3, mod_row_of_batch, row_len, ffn_grid_w, init, want_state, tt_out, tt_in, cg, nb, w):
    bsz, t, _ = x.shape
    x2d = x.reshape(bsz * t, D_MODEL)

    def mod_row(tile_tokens):
        return lambda i: mod_row_of_batch((i * tile_tokens) // t)

    ysc, qt, k, vt, sot, rcol, grow = _in_proj_call(x2d, mod3, mod_row(tt_in), row_len, tt_in, w)
    outs = _mlstm_call(bsz, t, cg, nb, qt, k, vt, grow, rcol, init, want_state)
    x1, h2 = _out_proj_call(x2d, ysc, outs[0], outs[1], sot, mod3, mod_row(tt_out), tt_out, w)
    y = _ffn_call(h2.reshape(bsz, t, D_MODEL), x1.reshape(bsz, t, D_MODEL), mod3, mod_row_of_batch, ffn_grid_w, w)
    return y, outs[2:]


def kernel(x_prompt, x_sample, state_C, state_n, state_m, c, c_ctx, w_mod, b_mod, norm1, w_in, b_gate,
           conv_sc_w, conv_sc_b, mh_norm, w_out, norm2, w_up, conv_ffn_w, conv_ffn_b, w_down, final_norm):
    n_lat = c.shape[0]
    n_ctx = x_prompt.shape[0]
    c8 = jnp.concatenate([c, c_ctx[None], jnp.zeros((8 - n_lat - 1, D_MODEL), F32)], axis=0)
    l = 0
    mod3 = _mod_call(c8, w_mod[l], b_mod[l][None]).reshape(8, 6, D_MODEL)

    wi = w_in[l]
    q0 = 3 * D_CONV
    w = dict(
        norm1=norm1[l][None], norm2=norm2[l][None], final_norm=final_norm[None],
        w_sc=wi[:, 0:q0].astype(BF16),
        w_k=wi[:, q0 + D_MLSTM:q0 + 2 * D_MLSTM].astype(BF16),
        w_qvogt=jnp.concatenate([wi[:, q0:q0 + D_MLSTM], wi[:, q0 + 2 * D_MLSTM:]], axis=1).T.astype(BF16),
        b_g=b_gate[l].astype(F32).reshape(2 * N_CHAINS, 1),
        conv_sc_w=conv_sc_w[l], conv_sc_b=conv_sc_b[l][None],
        mh_norm_col=mh_norm[l][:, None],
        w_out_a=w_out[l][0:D_CONV].astype(BF16), w_out_b=w_out[l][D_CONV:].astype(BF16),
        w_up=w_up[l].astype(BF16), conv_ffn_w=conv_ffn_w[l], conv_ffn_b=conv_ffn_b[l][None],
        w_down=w_down[l].astype(BF16),
    )

    seq = x_prompt.shape[1]
    y_prompt, (new_c, new_n, new_m) = _trunk(
        x_prompt, mod3, lambda b: n_lat, seq, None, None, True, 512, 1024, 1, 4, w)

    init = (state_C[:, l], state_n[:, l][:, :, :, None, :],
            jnp.broadcast_to(state_m[:, l][:, :, :, None, None], state_m[:, l].shape + (1, LANES)))
    y_sample, _ = _trunk(x_sample, mod3, lambda b: b, GRID_W, GRID_W, init, False, 512, 1024, 2, 1, w)

    new_n = new_n.reshape(n_ctx, 1, 2, N_HEADS, HEAD_DIM)
    new_m = new_m[..., 0, 0]
    return y_prompt, y_sample, new_c, new_n, new_m
```

```python
import functools

import jax
import jax.numpy as jnp
from jax import lax
from jax.experimental import pallas as pl
from jax.experimental.pallas import tpu as pltpu

F32 = jnp.float32
BF16 = jnp.bfloat16

D_MODEL = 1024
GRID_W = 64
D_CONV = 512
D_MLSTM = 512
N_HEADS = 4
HEAD_DIM = 128
D_FF = 2816
CHUNK = 256
EPS = 1e-6
LOG2E = 1.4426950408889634

LANES = 128
SUBLANES = 8
BF16_ROWS = 16
FF_BLK = 256
N_FF_BLK = D_FF // FF_BLK
CONV_ROWS = 128
FFN_TILE = 1024
FFN_COLS = 16
VMEM_LIMIT = 60 * 1024 * 1024
N_CHAINS = 2 * N_HEADS
E_ROWS = HEAD_DIM + BF16_ROWS
SKEW = 3
IN_SUB = 512
SKEW_FF = 2


def _cparams(n_axes):
    return pltpu.CompilerParams(
        dimension_semantics=("arbitrary",) * n_axes, vmem_limit_bytes=VMEM_LIMIT)


def _rms_scale(x):
    return x * lax.rsqrt(jnp.mean(x * x, axis=-1, keepdims=True) + EPS)


def _dot(a, b):
    return jnp.dot(a, b, preferred_element_type=F32)


def _dot_nt(a, b):
    return lax.dot_general(a, b, (((1,), (1,)), ((), ())), preferred_element_type=F32)


def _dot_tn(a, b):
    return lax.dot_general(a, b, (((0,), (0,)), ((), ())), preferred_element_type=F32)


def _mod_kernel(c_ref, w_ref, b_ref, o_ref):
    c = c_ref[...]
    s = c * jax.nn.sigmoid(c)
    o_ref[...] = _dot(s.astype(BF16), w_ref[...].astype(BF16)) + b_ref[...]


def _mod_call(c8, w_mod, b_mod):
    n_out = w_mod.shape[1]
    blk = 1024
    return pl.pallas_call(
        _mod_kernel,
        grid=(n_out // blk,),
        in_specs=[
            pl.BlockSpec((8, D_MODEL), lambda i: (0, 0)),
            pl.BlockSpec((D_MODEL, blk), lambda i: (0, i)),
            pl.BlockSpec((1, blk), lambda i: (0, i)),
        ],
        out_specs=pl.BlockSpec((8, blk), lambda i: (0, i)),
        out_shape=jax.ShapeDtypeStruct((8, n_out), F32),
        compiler_params=_cparams(1),
        name="mod",
    )(c8, w_mod, b_mod)


def _chunk_scan(x, pos, op, fill, reverse):
    n = x.shape[1]
    k = 1
    while k < CHUNK:
        if reverse:
            shifted = pltpu.roll(x, n - k, axis=1)
            ok = pos < CHUNK - k
        else:
            shifted = pltpu.roll(x, k, axis=1)
            ok = pos >= k
        x = op(x, jnp.where(ok, shifted, fill))
        k *= 2
    return x


def _in_proj_kernel(row_len, x_ref, mod_ref, n1_ref, wsc_ref, wk_ref, wqvogt_ref, bg_ref, cw_ref, cb_ref,
                    ysc_ref, qt_ref, k_ref, vt_ref, sot_ref, rcol_ref, grow_ref):
    tt = x_ref.shape[0]
    sub = min(tt, IN_SUB)
    mod = mod_ref[...]
    shift1, scale1 = mod[0:1], mod[1:2]

    def prep(r0):
        x = x_ref[r0:r0 + sub]
        h = (_rms_scale(x) * n1_ref[...]) * (1.0 + scale1) + shift1
        return h.astype(BF16)

    def project_mlstm(r0, hb):
        rows = slice(r0, r0 + sub)
        qvog_t = _dot_nt(wqvogt_ref[...], hb)
        qt_ref[:, rows] = (qvog_t[0:D_MLSTM] * (HEAD_DIM ** -0.5)).astype(BF16)
        vt_ref[:, rows] = qvog_t[D_MLSTM:2 * D_MLSTM].astype(BF16)
        sot_ref[:, rows] = jax.nn.sigmoid(qvog_t[2 * D_MLSTM:3 * D_MLSTM]).astype(BF16)

        g_t = qvog_t[3 * D_MLSTM:3 * D_MLSTM + 2 * N_CHAINS] + bg_ref[...]
        gi = jnp.concatenate([g_t[0:N_HEADS], g_t[2 * N_HEADS:3 * N_HEADS]], axis=0)
        gf = jnp.concatenate([g_t[N_HEADS:2 * N_HEADS], g_t[3 * N_HEADS:4 * N_HEADS]], axis=0)
        logf = jnp.minimum(gf, 0.0) - jnp.log1p(jnp.exp(-jnp.abs(gf)))
        fwd = lax.broadcasted_iota(jnp.int32, gi.shape, 0) < N_HEADS
        cpos = lax.broadcasted_iota(jnp.int32, gi.shape, 1) % CHUNK
        b = jnp.where(fwd, _chunk_scan(logf, cpos, jnp.add, 0.0, False),
                      _chunk_scan(logf, cpos, jnp.add, 0.0, True))
        r = gi - b
        cm = jnp.where(fwd, _chunk_scan(r, cpos, jnp.maximum, -jnp.inf, False),
                       _chunk_scan(r, cpos, jnp.maximum, -jnp.inf, True))
        b, r, cm = b * LOG2E, r * LOG2E, cm * LOG2E
        for c in range(sub // CHUNK):
            toks = slice(c * CHUNK, (c + 1) * CHUNK)
            grow_ref[r0 // CHUNK + c] = jnp.concatenate([cm[:, toks], r[:, toks], b[:, toks]], axis=0)
        rcol_ref[rows] = jnp.concatenate([r, jnp.zeros((LANES - N_CHAINS, sub), F32)], axis=0).T

    def project_conv(r0, hb):
        rows = slice(r0, r0 + sub)
        zb = _dot(hb, wsc_ref[:, 0:D_CONV])
        zc = _dot(hb, wsc_ref[:, D_CONV:2 * D_CONV])
        zx = _dot(hb, wsc_ref[:, 2 * D_CONV:3 * D_CONV])
        u = zc * zx
        pos = lax.broadcasted_iota(jnp.int32, u.shape, 0) % row_len
        prev = jnp.where(pos == 0, 0.0, pltpu.roll(u, 1, axis=0))
        nxt = jnp.where(pos == row_len - 1, 0.0, pltpu.roll(u, sub - 1, axis=0))
        cw = cw_ref[...]
        conv = cw[0:1] * prev + cw[1:2] * u + cw[2:3] * nxt + cb_ref[...]
        ysc_ref[rows] = (zb * conv).astype(BF16)

        k_ref[rows] = _dot(hb, wk_ref[...]).astype(BF16)

    starts = list(range(0, tt, sub))
    hb = prep(starts[0])
    for idx, r0 in enumerate(starts):
        project_mlstm(r0, hb)
        hb_next = prep(starts[idx + 1]) if idx + 1 < len(starts) else None
        project_conv(r0, hb)
        hb = hb_next


def _in_proj_call(x2d, mod3, mod_row, row_len, tt, w):
    n_tok = x2d.shape[0]
    n_tiles = n_tok // tt
    const = lambda i: (0, 0)
    tok = lambda i: (i, 0)
    tok_t = lambda i: (0, i)
    weights = [w["norm1"], w["w_sc"], w["w_k"], w["w_qvogt"], w["b_g"], w["conv_sc_w"], w["conv_sc_b"]]
    out_shapes = (
        jax.ShapeDtypeStruct((n_tok, D_CONV), BF16),
        jax.ShapeDtypeStruct((D_MLSTM, n_tok), BF16),
        jax.ShapeDtypeStruct((n_tok, D_MLSTM), BF16),
        jax.ShapeDtypeStruct((D_MLSTM, n_tok), BF16),
        jax.ShapeDtypeStruct((D_MLSTM, n_tok), BF16),
        jax.ShapeDtypeStruct((n_tok, LANES), F32),
        jax.ShapeDtypeStruct((n_tok // CHUNK, 3 * N_CHAINS, CHUNK), F32),
    )
    return pl.pallas_call(
        functools.partial(_in_proj_kernel, row_len),
        grid=(n_tiles,),
        in_specs=[
            pl.BlockSpec((tt, D_MODEL), tok),
            pl.BlockSpec((None, 6, D_MODEL), lambda i: (mod_row(i), 0, 0)),
        ] + [pl.BlockSpec(a.shape, const) for a in weights],
        out_specs=[
            pl.BlockSpec((tt, D_CONV), tok),
            pl.BlockSpec((D_MLSTM, tt), tok_t),
            pl.BlockSpec((tt, D_MLSTM), tok),
            pl.BlockSpec((D_MLSTM, tt), tok_t),
            pl.BlockSpec((D_MLSTM, tt), tok_t),
            pl.BlockSpec((tt, LANES), tok),
            pl.BlockSpec((tt // CHUNK, 3 * N_CHAINS, CHUNK), lambda i: (i, 0, 0)),
        ],
        out_shape=out_shapes,
        compiler_params=_cparams(1),
        name="in_proj",
    )(x2d, mod3, *weights)


def _mlstm_kernel(cg, nb, has_init, want_state, *refs):
    refs = list(refs)
    fwd_refs, bwd_refs = refs[0:5], refs[5:10]
    refs = refs[10:]
    if has_init:
        c0_ref, n0_ref, m0_ref = refs[:3]
        refs = refs[3:]
    htf_ref, htb_ref = refs[:2]
    refs = refs[2:]
    if want_state:
        cout_ref, nout_ref, mout_ref = refs[:3]
        refs = refs[3:]
    e_ref, m_ref = refs

    g = pl.program_id(1)

    @pl.when(g == 0)
    def _():
        if has_init:
            for d in range(2):
                for hd in range(N_HEADS):
                    j = d * N_HEADS + hd
                    e_ref[j, 0:HEAD_DIM] = c0_ref[d, hd]
                    e_ref[j, HEAD_DIM:] = jnp.broadcast_to(n0_ref[d, hd], (BF16_ROWS, HEAD_DIM))
                    m_ref[j] = jnp.broadcast_to(m0_ref[d, hd] * LOG2E, (SUBLANES, LANES))
        else:
            e_ref[...] = jnp.zeros(e_ref.shape, F32)
            m_ref[...] = jnp.zeros(m_ref.shape, F32)

    s_idx = lax.broadcasted_iota(jnp.int32, (CHUNK, CHUNK), 0)
    t_idx = lax.broadcasted_iota(jnp.int32, (CHUNK, CHUNK), 1)
    ones_rows = jnp.ones((BF16_ROWS, CHUNK), BF16)

    def stage_scores(c, d, hd, seq):
        qt_ref, k_ref, vt_ref, row_ref, col_ref = fwd_refs if d == 0 else bwd_refs
        j = d * N_HEADS + hd
        js = seq * N_CHAINS + j
        toks = slice(c * CHUNK, (c + 1) * CHUNK)
        feat = slice(hd * HEAD_DIM, (hd + 1) * HEAD_DIM)
        last = CHUNK - 1 if d == 0 else 0

        qtc = qt_ref[feat, toks]
        kc = k_ref[toks, feat]
        vtc = vt_ref[feat, toks]
        cm_row = row_ref[c, j:j + 1, :]
        r_row = row_ref[c, N_CHAINS + j:N_CHAINS + j + 1, :]
        b_row = row_ref[c, 2 * N_CHAINS + j:2 * N_CHAINS + j + 1, :]
        m_row = m_ref[js, 0:1, :]
        m_t = m_row[:, 0:1]
        big_m_row = jnp.maximum(m_t, cm_row)
        m_last = big_m_row[:, last:last + 1]

        e = e_ref[js]
        kq = _dot(jnp.concatenate([kc, e.astype(BF16)], axis=0), qtc)

        wg = jnp.exp2(r_row - m_last)
        lhs = jnp.concatenate([(vtc.astype(F32) * wg).astype(BF16),
                               jnp.broadcast_to(wg, (BF16_ROWS, CHUNK)).astype(BF16)], axis=0)
        e_ref[js] = jnp.exp2(m_row - m_last) * e + _dot(lhs, kc)
        m_ref[js] = jnp.broadcast_to(b_row[:, last:last + 1] + m_last, (SUBLANES, LANES))
        return kq, vtc, big_m_row, m_t, b_row

    def stage_output(c, d, hd, seq, kq, vtc, big_m_row, m_t, b_row):
        col_ref = (fwd_refs if d == 0 else bwd_refs)[4]
        ht_ref = htf_ref if d == 0 else htb_ref
        j = d * N_HEADS + hd
        toks = slice(c * CHUNK, (c + 1) * CHUNK)
        feat = slice(hd * HEAD_DIM, (hd + 1) * HEAD_DIM)
        mask = (s_idx <= t_idx) if d == 0 else (s_idx >= t_idx)
        r_col = col_ref[toks, j:j + 1]

        w_t = jnp.exp2(jnp.where(mask, r_col - big_m_row, -jnp.inf))
        s_t = (kq[0:CHUNK] * w_t).astype(BF16)
        num_t = _dot(jnp.concatenate([vtc, ones_rows], axis=0), s_t)
        nd = num_t + jnp.exp2(m_t - big_m_row) * kq[CHUNK:]
        inv = 1.0 / jnp.maximum(jnp.abs(nd[HEAD_DIM:HEAD_DIM + 1]), jnp.exp2(-(b_row + big_m_row)))
        ht_ref[feat, toks] = (nd[0:HEAD_DIM] * inv).astype(BF16)

    tasks = [(seq * cg + (c if d == 0 else cg - 1 - c), d, hd, seq)
             for seq in range(nb) for c in range(cg) for d in range(2) for hd in range(N_HEADS)]
    pending = []
    for task in tasks:
        pending.append(task + stage_scores(*task))
        if len(pending) > SKEW:
            stage_output(*pending.pop(0))
    for args in pending:
        stage_output(*args)

    if want_state:
        @pl.when(g == pl.num_programs(1) - 1)
        def _():
            for seq in range(nb):
                for d in range(2):
                    for hd in range(N_HEADS):
                        js = seq * N_CHAINS + d * N_HEADS + hd
                        cout_ref[seq, d, hd] = e_ref[js, 0:HEAD_DIM]
                        nout_ref[seq, d, hd] = e_ref[js, HEAD_DIM:HEAD_DIM + 1]
                        mout_ref[seq, d, hd] = m_ref[js, 0:1, :] * (1.0 / LOG2E)


def _mlstm_call(bsz, t, cg, nb, qt, k, vt, grow, rcol, init, want_state):
    n_tok = bsz * t
    n_groups = t // (cg * CHUNK)
    assert nb == 1 or (n_groups == 1 and init is None)
    tg = nb * cg * CHUNK
    fwd = lambda b, g: b * n_groups + g
    bwd = lambda b, g: b * n_groups + (n_groups - 1 - g)

    def stream_specs(pos):
        return [
            pl.BlockSpec((D_MLSTM, tg), lambda b, g: (0, pos(b, g))),
            pl.BlockSpec((tg, D_MLSTM), lambda b, g: (pos(b, g), 0)),
            pl.BlockSpec((D_MLSTM, tg), lambda b, g: (0, pos(b, g))),
            pl.BlockSpec((nb * cg, 3 * N_CHAINS, CHUNK), lambda b, g: (pos(b, g), 0, 0)),
            pl.BlockSpec((tg, LANES), lambda b, g: (pos(b, g), 0)),
        ]

    in_specs = stream_specs(fwd) + stream_specs(bwd)
    args = [qt, k, vt, grow, rcol] * 2
    if init is not None:
        c0, n0, m0 = init
        in_specs += [
            pl.BlockSpec((None, 2, N_HEADS, HEAD_DIM, HEAD_DIM), lambda b, g: (b, 0, 0, 0, 0)),
            pl.BlockSpec((None, 2, N_HEADS, 1, HEAD_DIM), lambda b, g: (b, 0, 0, 0, 0)),
            pl.BlockSpec((None, 2, N_HEADS, 1, LANES), lambda b, g: (b, 0, 0, 0, 0)),
        ]
        args += [c0, n0, m0]
    out_specs = [
        pl.BlockSpec((D_MLSTM, tg), lambda b, g: (0, fwd(b, g))),
        pl.BlockSpec((D_MLSTM, tg), lambda b, g: (0, bwd(b, g))),
    ]
    out_shape = [jax.ShapeDtypeStruct((D_MLSTM, n_tok), BF16)] * 2
    if want_state:
        out_specs += [
            pl.BlockSpec((nb, None, 2, N_HEADS, HEAD_DIM, HEAD_DIM), lambda b, g: (b, 0, 0, 0, 0, 0)),
            pl.BlockSpec((nb, None, 2, N_HEADS, 1, HEAD_DIM), lambda b, g: (b, 0, 0, 0, 0, 0)),
            pl.BlockSpec((nb, None, 2, N_HEADS, 1, LANES), lambda b, g: (b, 0, 0, 0, 0, 0)),
        ]
        out_shape += [
            jax.ShapeDtypeStruct((bsz, 1, 2, N_HEADS, HEAD_DIM, HEAD_DIM), F32),
            jax.ShapeDtypeStruct((bsz, 1, 2, N_HEADS, 1, HEAD_DIM), F32),
            jax.ShapeDtypeStruct((bsz, 1, 2, N_HEADS, 1, LANES), F32),
        ]
    return pl.pallas_call(
        functools.partial(_mlstm_kernel, cg, nb, init is not None, want_state),
        grid=(bsz // nb, n_groups),
        in_specs=in_specs,
        out_specs=out_specs,
        out_shape=out_shape,
        scratch_shapes=[
            pltpu.VMEM((nb * N_CHAINS, E_ROWS, HEAD_DIM), F32),
            pltpu.VMEM((nb * N_CHAINS, SUBLANES, LANES), F32),
        ],
        compiler_params=_cparams(2),
        name="mlstm",
    )(*args)


def _out_proj_kernel(x_ref, ysc_ref, htf_ref, htb_ref, sot_ref, mh_ref, mod_ref, n2_ref, wa_ref, wb_ref,
                     x1_ref, h2_ref):
    mod = mod_ref[...]
    gate1, shift2, scale2 = mod[2:3], mod[3:4], mod[4:5]
    hs = htf_ref[...].astype(F32) + htb_ref[...].astype(F32)
    heads = []
    for hd in range(N_HEADS):
        blk = hs[hd * HEAD_DIM:(hd + 1) * HEAD_DIM]
        heads.append(blk * lax.rsqrt(jnp.mean(blk * blk, axis=0, keepdims=True) + EPS))
    hm_t = ((jnp.concatenate(heads, axis=0) * mh_ref[...]) * sot_ref[...].astype(F32)).astype(BF16)
    mix = _dot(ysc_ref[...], wa_ref[...]) + _dot_tn(hm_t, wb_ref[...])
    x1 = x_ref[...] + gate1 * mix
    x1_ref[...] = x1
    h2_ref[...] = ((_rms_scale(x1) * n2_ref[...]) * (1.0 + scale2) + shift2).astype(BF16)


def _out_proj_call(x2d, ysc, htf, htb, sot, mod3, mod_row, tt, w):
    n_tok = x2d.shape[0]
    const = lambda i: (0, 0)
    tok = lambda i: (i, 0)
    tok_t = lambda i: (0, i)
    return pl.pallas_call(
        _out_proj_kernel,
        grid=(n_tok // tt,),
        in_specs=[
            pl.BlockSpec((tt, D_MODEL), tok),
            pl.BlockSpec((tt, D_CONV), tok),
            pl.BlockSpec((D_MLSTM, tt), tok_t),
            pl.BlockSpec((D_MLSTM, tt), tok_t),
            pl.BlockSpec((D_MLSTM, tt), tok_t),
            pl.BlockSpec((D_MLSTM, 1), const),
            pl.BlockSpec((None, 6, D_MODEL), lambda i: (mod_row(i), 0, 0)),
            pl.BlockSpec((1, D_MODEL), const),
            pl.BlockSpec(w["w_out_a"].shape, const),
            pl.BlockSpec(w["w_out_b"].shape, const),
        ],
        out_specs=[pl.BlockSpec((tt, D_MODEL), tok), pl.BlockSpec((tt, D_MODEL), tok)],
        out_shape=[jax.ShapeDtypeStruct((n_tok, D_MODEL), F32),
                   jax.ShapeDtypeStruct((n_tok, D_MODEL), BF16)],
        compiler_params=_cparams(1),
        name="out_proj",
    )(x2d, ysc, htf, htb, sot, w["mh_norm_col"], mod3, w["norm2"], w["w_out_a"], w["w_out_b"])


def _ffn_kernel(shift, seq_len, h2_ref, x1_ref, mod_ref, fn_ref, wu_ref, cw_ref, cb_ref, wd_ref, y_ref,
                act_ref, h2s_ref, *slot_refs):
    slots = tuple(zip(slot_refs[0::2], slot_refs[1::2]))
    tt = act_ref.shape[0]
    pad = _ffn_pad(shift)
    nr = CONV_ROWS if seq_len is None else seq_len
    row = lax.broadcasted_iota(jnp.int32, (nr, FF_BLK), 0)

    def up_proj(blk, slot):
        a_ref, g_ref = slots[slot]
        off_a = pl.multiple_of(blk * FF_BLK, FF_BLK)
        off_g = pl.multiple_of(blk * FF_BLK + D_FF, FF_BLK)
        hin = h2s_ref[...]
        a_ref[pad:pad + tt] = _dot(hin, wu_ref[:, pl.ds(off_a, FF_BLK)])
        g_ref[pad:pad + tt] = _dot(hin, wu_ref[:, pl.ds(off_g, FF_BLK)])

    def conv_act(blk, slot):
        a_ref, g_ref = slots[slot]
        off_a = pl.multiple_of(blk * FF_BLK, FF_BLK)
        off_g = pl.multiple_of(blk * FF_BLK + D_FF, FF_BLK)

        def conv(z_ref, off, r0):
            cw = cw_ref[:, pl.ds(off, FF_BLK)]
            cb = cb_ref[:, pl.ds(off, FF_BLK)]
            prev = z_ref[pad - shift + r0:pad - shift + r0 + nr]
            nxt = z_ref[pad + shift + r0:pad + shift + r0 + nr]
            if seq_len is not None:
                if r0 % seq_len == 0:
                    prev = jnp.where(row == 0, 0.0, prev)
                if (r0 + nr) % seq_len == 0:
                    nxt = jnp.where(row == nr - 1, 0.0, nxt)
            return cw[0:1] * prev + cw[1:2] * z_ref[pad + r0:pad + r0 + nr] + cw[2:3] * nxt + cb

        for r0 in range(0, tt, nr):
            ac = conv(a_ref, off_a, r0)
            gc = conv(g_ref, off_g, r0)
            act_ref[r0:r0 + nr, pl.ds(off_a, FF_BLK)] = (gc * jax.nn.sigmoid(gc) * ac).astype(BF16)

    h2s_ref[...] = h2_ref[...].reshape(tt, D_MODEL)
    for a_ref, g_ref in slots:
        for z_ref in (a_ref, g_ref):
            z_ref[0:pad] = jnp.zeros((pad, FF_BLK), F32)
            z_ref[pad + tt:] = jnp.zeros((pad, FF_BLK), F32)

    n_slots = len(slots)
    for blk in range(SKEW_FF):
        up_proj(blk, blk % n_slots)

    def block_group(p, carry):
        for s in range(n_slots):
            blk = n_slots * p + SKEW_FF + s
            up_proj(blk, (SKEW_FF + s) % n_slots)
            conv_act(blk - SKEW_FF, s % n_slots)
        return carry

    n_groups = (N_FF_BLK - SKEW_FF) // n_slots
    lax.fori_loop(0, n_groups, block_group, 0)
    done = n_slots * n_groups + SKEW_FF
    for blk in range(done, N_FF_BLK):
        up_proj(blk, blk % n_slots)
        conv_act(blk - SKEW_FF, (blk - SKEW_FF) % n_slots)

    k0 = (N_FF_BLK - SKEW_FF) * FF_BLK
    part = _dot(act_ref[:, 0:k0], wd_ref[0:k0, :])
    for blk in range(N_FF_BLK - SKEW_FF, N_FF_BLK):
        conv_act(blk, blk % n_slots)
    ffn = part + _dot(act_ref[:, k0:], wd_ref[k0:, :])
    x2 = x1_ref[...].reshape(tt, D_MODEL) + mod_ref[5:6] * ffn
    y_ref[...] = (_rms_scale(x2) * fn_ref[...]).reshape(y_ref.shape)


def _ffn_pad(shift):
    return -(-shift // SUBLANES) * SUBLANES


def _ffn_call(h2, x1, mod3, mod_row_of_batch, grid_w, w):
    bsz, t, _ = x1.shape
    const = lambda i: (0, 0)
    resident = dict(pipeline_mode=pl.Buffered(1))
    if grid_w is None:
        tt, shift, seq_len = FFN_TILE, 1, t
        per_tile = tt // t
        blk = (per_tile, t, D_MODEL)
        n_tiles = bsz // per_tile
        tile_idx = lambda i: (i, 0, 0)
        batch_of = lambda i: i * per_tile
        as_blocks = lambda a: a
    else:
        rows = t // grid_w
        tt, shift, seq_len = rows * FFN_COLS, FFN_COLS, None
        strips = grid_w // FFN_COLS
        blk = (None, rows, FFN_COLS, D_MODEL)
        n_tiles = bsz * strips
        tile_idx = lambda i: (i // strips, 0, i % strips, 0)
        batch_of = lambda i: i // strips
        as_blocks = lambda a: a.reshape(bsz, rows, grid_w, D_MODEL)
    up_rows = tt + 2 * _ffn_pad(shift)
    scratch = ([pltpu.VMEM((tt, D_FF), BF16), pltpu.VMEM((tt, D_MODEL), BF16)]
               + [pltpu.VMEM((up_rows, FF_BLK), F32)] * (2 * (SKEW_FF + 1)))
    in_specs = [
        pl.BlockSpec(blk, tile_idx),
        pl.BlockSpec(blk, tile_idx),
        pl.BlockSpec((None, 6, D_MODEL), lambda i: (mod_row_of_batch(batch_of(i)), 0, 0)),
        pl.BlockSpec((1, D_MODEL), const),
        pl.BlockSpec((D_MODEL, 2 * D_FF), const, **resident),
        pl.BlockSpec((3, 2 * D_FF), const),
        pl.BlockSpec((1, 2 * D_FF), const),
        pl.BlockSpec((D_FF, D_MODEL), const, **resident),
    ]
    y = pl.pallas_call(
        functools.partial(_ffn_kernel, shift, seq_len),
        grid=(n_tiles,),
        in_specs=in_specs,
        out_specs=pl.BlockSpec(blk, tile_idx),
        out_shape=jax.ShapeDtypeStruct(as_blocks(x1).shape, F32),
        scratch_shapes=scratch,
        compiler_params=_cparams(1),
        name="ffn",
    )(as_blocks(h2), as_blocks(x1), mod3, w["final_norm"], w["w_up"], w["conv_ffn_w"], w["conv_ffn_b"],
      w["w_down"])
    return y.reshape(bsz, t, D_MODEL)


def _trunk(x, mod3, mod_row_of_batch, row_len, ffn_grid_w, init, want_state, tt_out, tt_in, cg, nb, w):
    bsz, t, _ = x.shape
    x2d = x.reshape(bsz * t, D_MODEL)

    def mod_row(tile_tokens):
        return lambda i: mod_row_of_batch((i * tile_tokens) // t)

    ysc, qt, k, vt, sot, rcol, grow = _in_proj_call(x2d, mod3, mod_row(tt_in), row_len, tt_in, w)
    outs = _mlstm_call(bsz, t, cg, nb, qt, k, vt, grow, rcol, init, want_state)
    x1, h2 = _out_proj_call(x2d, ysc, outs[0], outs[1], sot, mod3, mod_row(tt_out), tt_out, w)
    y = _ffn_call(h2.reshape(bsz, t, D_MODEL), x1.reshape(bsz, t, D_MODEL), mod3, mod_row_of_batch, ffn_grid_w, w)
    return y, outs[2:]


def kernel(x_prompt, x_sample, state_C, state_n, state_m, c, c_ctx, w_mod, b_mod, norm1, w_in, b_gate,
           conv_sc_w, conv_sc_b, mh_norm, w_out, norm2, w_up, conv_ffn_w, conv_ffn_b, w_down, final_norm):
    n_lat = c.shape[0]
    n_ctx = x_prompt.shape[0]
    c8 = jnp.concatenate([c, c_ctx[None], jnp.zeros((8 - n_lat - 1, D_MODEL), F32)], axis=0)
    l = 0
    mod3 = _mod_call(c8, w_mod[l], b_mod[l][None]).reshape(8, 6, D_MODEL)

    wi = w_in[l]
    q0 = 3 * D_CONV
    w = dict(
        norm1=norm1[l][None], norm2=norm2[l][None], final_norm=final_norm[None],
        w_sc=wi[:, 0:q0].astype(BF16),
        w_k=wi[:, q0 + D_MLSTM:q0 + 2 * D_MLSTM].astype(BF16),
        w_qvogt=jnp.concatenate([wi[:, q0:q0 + D_MLSTM], wi[:, q0 + 2 * D_MLSTM:]], axis=1).T.astype(BF16),
        b_g=b_gate[l].astype(F32).reshape(2 * N_CHAINS, 1),
        conv_sc_w=conv_sc_w[l], conv_sc_b=conv_sc_b[l][None],
        mh_norm_col=mh_norm[l][:, None],
        w_out_a=w_out[l][0:D_CONV].astype(BF16), w_out_b=w_out[l][D_CONV:].astype(BF16),
        w_up=w_up[l].astype(BF16), conv_ffn_w=conv_ffn_w[l], conv_ffn_b=conv_ffn_b[l][None],
        w_down=w_down[l].astype(BF16),
    )

    seq = x_prompt.shape[1]
    y_prompt, (new_c, new_n, new_m) = _trunk(
        x_prompt, mod3, lambda b: n_lat, seq, None, None, True, 512, 1024, 1, 4, w)

    init = (state_C[:, l], state_n[:, l][:, :, :, None, :],
            jnp.broadcast_to(state_m[:, l][:, :, :, None, None], state_m[:, l].shape + (1, LANES)))
    y_sample, _ = _trunk(x_sample, mod3, lambda b: b, GRID_W, GRID_W, init, False, 512, 1024, 2, 1, w)

    new_n = new_n.reshape(n_ctx, 1, 2, N_HEADS, HEAD_DIM)
    new_m = new_m[..., 0, 0]
    return y_prompt, y_sample, new_c, new_n, new_m
```
